```python
import jax
import jax.numpy as jnp
from jax import lax
import numpy as np

D_MODEL = 1024
BATCH = 8
SEQ = 2048
DEPTH = 2
DEC_BATCH = 32
DEC_SEQ = 2048
PAST_LEN = 128

RG_WIDTH = D_MODEL // 2
RG_BLOCKS = 8
RG_BLOCK_DIM = RG_WIDTH // RG_BLOCKS
CONV_WIDTH = 4
RG_C = 8.0
RW_HEADS = 8
RW_HEAD_DIM = (D_MODEL // 2) // RW_HEADS
RW_WIDTH = RW_HEADS * RW_HEAD_DIM
DECAY_LORA = 32
ICL_LORA = 32
GATE_LORA = 64
RW_IN = 3 * RW_WIDTH + DECAY_LORA + ICL_LORA + GATE_LORA
REC_IN = 2 * RG_WIDTH + RW_IN
RW_GN_EPS = 64e-5
ATT_HEADS = 16
ATT_KV_HEADS = 4
ATT_GROUP = ATT_HEADS // ATT_KV_HEADS
ATT_HEAD_DIM = D_MODEL // ATT_HEADS
WINDOW = 128
ATT_BLOCK = 128
KEY_SPAN = ATT_BLOCK + 2 * WINDOW
ATT_IN = (ATT_HEADS + 2 * ATT_KV_HEADS) * ATT_HEAD_DIM
N_GROUPS = 4
EXPERTS_PER_GROUP = 8
N_EXPERTS = N_GROUPS * EXPERTS_PER_GROUP
TOP_K = 2
EXPERT_FF = 512
MOE_BLOCK = 128

RMS_EPS = 1e-6
NEG_INF = -1e30
N_REC = (DEPTH + 1) // 2
N_ATT = DEPTH // 2

kernel_name = "hybrid_bidir_rglru_rwkv7_swa_hmoe"


def rms_norm(x, g):
    xf = x.astype(jnp.float32)
    y = xf * lax.rsqrt(jnp.mean(xf * xf, -1, keepdims=True) + RMS_EPS)
    return (y * g.astype(jnp.float32)).astype(x.dtype)


def head_rms(x, g):
    xf = x.astype(jnp.float32)
    y = xf * lax.rsqrt(jnp.mean(xf * xf, -1, keepdims=True) + RMS_EPS)
    return (y * g.astype(jnp.float32)).astype(x.dtype)


def depthwise_conv(x, w, b):
    left = CONV_WIDTH // 2
    y = lax.conv_general_dilated(
        x, w[:, None, :].astype(x.dtype), window_strides=(1,),
        padding=[(left, CONV_WIDTH - 1 - left)],
        dimension_numbers=('NWC', 'WIO', 'NWC'), feature_group_count=x.shape[-1])
    return y + b


def _lin_combine(left, right):
    a_l, b_l = left
    a_r, b_r = right
    return a_l * a_r, a_r * b_l + b_r


def rglru(xc, wa, ba, wx, bx, lam):
    B, S, _ = xc.shape
    f32 = jnp.float32
    xf = xc.astype(f32)
    xb = xf.reshape(B, S, RG_BLOCKS, RG_BLOCK_DIM)
    r = jax.nn.sigmoid(jnp.einsum('bsnc,ncd->bsnd', xb, wa.astype(f32)).reshape(B, S, RG_WIDTH) + ba)
    i = jax.nn.sigmoid(jnp.einsum('bsnc,ncd->bsnd', xb, wx.astype(f32)).reshape(B, S, RG_WIDTH) + bx)
    log_a = -RG_C * r * jax.nn.softplus(-lam.astype(f32))
    a = jnp.exp(log_a)
    b = jnp.sqrt(-jnp.expm1(2.0 * log_a)) * (i * xf)
    return lax.associative_scan(_lin_combine, (a, b), axis=1)[1]


def centred_shift_mix(u, mu_l, mu_r):
    prev = jnp.pad(u, ((0, 0), (1, 0), (0, 0)))[:, :-1]
    nxt = jnp.pad(u, ((0, 0), (0, 1), (0, 0)))[:, 1:]
    return u + mu_l * (prev - u) + mu_r * (nxt - u)


def wkv_scan(r, w, k, v, kk, a, reverse):
    B, S, H, N = r.shape
    xs = tuple(jnp.moveaxis(t, 1, 0) for t in (r, w, k, v, kk, a))

    def step(state, inp):
        r_t, w_t, k_t, v_t, kk_t, a_t = inp
        sa = jnp.einsum('bhij,bhj->bhi', state, -kk_t)
        state = (state * w_t[:, :, None, :]
                 + sa[..., :, None] * (kk_t * a_t)[:, :, None, :]
                 + v_t[..., :, None] * k_t[:, :, None, :])
        y = jnp.einsum('bhij,bhj->bhi', state, r_t)
        return state, y

    state0 = jnp.zeros((B, H, N, N), jnp.float32)
    _, ys = lax.scan(step, state0, xs, reverse=reverse)
    return jnp.moveaxis(ys, 0, 1)


def recurrent_mixer(h, w_in, conv_w, conv_b, rg_wa, rg_ba, rg_wx, rg_bx, rg_lambda,
                    mu_l, mu_r, w0, w_up, a0, a_up, g_up, k_k, k_a, r_k, ln_w, ln_b, w_out):
    B, S, _ = h.shape
    f32 = jnp.float32
    u = h @ w_in
    rg_x, rg_g, rw = jnp.split(u, [RG_WIDTH, 2 * RG_WIDTH], axis=-1)

    xc = depthwise_conv(rg_x, conv_w, conv_b)
    h_fwd = rglru(xc, rg_wa[0], rg_ba[0], rg_wx[0], rg_bx[0], rg_lambda[0])
    h_bwd = jnp.flip(rglru(jnp.flip(xc, 1), rg_wa[1], rg_ba[1], rg_wx[1], rg_bx[1], rg_lambda[1]), 1)
    rg_out = (h_fwd + h_bwd) * jax.nn.gelu(rg_g.astype(f32))

    rw = centred_shift_mix(rw, mu_l, mu_r).astype(f32)
    r, k, v, wd, ad, gd = jnp.split(
        rw, [RW_WIDTH, 2 * RW_WIDTH, 3 * RW_WIDTH, 3 * RW_WIDTH + DECAY_LORA,
             3 * RW_WIDTH + DECAY_LORA + ICL_LORA], axis=-1)

    def heads(t):
        return t.reshape(B, S, RW_HEADS, RW_HEAD_DIM)

    kk = heads(k * k_k)
    kk = kk / jnp.maximum(jnp.sqrt(jnp.sum(kk * kk, -1, keepdims=True)), 1e-12)
    tw = jnp.tanh(wd)

    def direction(d, reverse):
        w_log = -jax.nn.softplus(-(w0[d] + tw @ w_up[d].astype(f32))) - 0.5
        decay = jnp.exp(-jnp.exp(w_log))
        a = jax.nn.sigmoid(a0[d] + ad @ a_up[d].astype(f32))
        kd = k * (1.0 + (a - 1.0) * k_a)
        y = wkv_scan(heads(r), heads(decay), heads(kd), heads(v), kk, heads(a), reverse)
        return y, kd

    y_f, k_f = direction(0, False)
    y_b, k_b = direction(1, True)
    y = y_f + y_b
    mu = jnp.mean(y, -1, keepdims=True)
    var = jnp.mean(jnp.square(y - mu), -1, keepdims=True)
    yn = ((y - mu) * lax.rsqrt(var + RW_GN_EPS)).reshape(B, S, RW_WIDTH) * ln_w + ln_b
    bonus = jnp.sum(heads(r) * heads(k_f + k_b) * r_k, -1, keepdims=True) * heads(v)
    g = jax.nn.sigmoid(gd) @ g_up.astype(f32)
    rw_out = (yn + bonus.reshape(B, S, RW_WIDTH)) * g

    mixed = jnp.concatenate([rg_out, rw_out], axis=-1).astype(h.dtype)
    return mixed @ w_out


def alibi_slopes():
    return 2.0 ** (-8.0 * jnp.arange(1, ATT_HEADS + 1, dtype=jnp.float32) / ATT_HEADS)


def window_attention(h, w_in, q_norm, k_norm, sink, w_out):
    B, S, _ = h.shape
    f32 = jnp.float32
    nb = S // ATT_BLOCK
    qkv = h @ w_in
    q, k, v = jnp.split(qkv, [ATT_HEADS * ATT_HEAD_DIM, (ATT_HEADS + ATT_KV_HEADS) * ATT_HEAD_DIM], axis=-1)
    q = head_rms(q.reshape(B, S, ATT_KV_HEADS, ATT_GROUP, ATT_HEAD_DIM), q_norm)
    k = head_rms(k.reshape(B, S, ATT_KV_HEADS, ATT_HEAD_DIM), k_norm)
    v = v.reshape(B, S, ATT_KV_HEADS, ATT_HEAD_DIM)
    pad = ((0, 0), (WINDOW, WINDOW), (0, 0), (0, 0))
    k_pad = jnp.pad(k, pad)
    v_pad = jnp.pad(v, pad)
    q_blocks = jnp.moveaxis(q.reshape(B, nb, ATT_BLOCK, ATT_KV_HEADS, ATT_GROUP, ATT_HEAD_DIM), 1, 0)

    rel = (jnp.arange(KEY_SPAN)[None, :] - WINDOW) - jnp.arange(ATT_BLOCK)[:, None]
    band = jnp.abs(rel) <= WINDOW
    slopes = alibi_slopes().reshape(ATT_KV_HEADS, ATT_GROUP)
    alibi = -slopes[:, :, None, None] * jnp.abs(rel).astype(f32)
    sink_logit = sink.astype(f32).reshape(ATT_KV_HEADS, ATT_GROUP)[None, :, :, None, None]
    scale = ATT_HEAD_DIM ** -0.5

    def block(args):
        i, q_i = args
        start = i * ATT_BLOCK
        k_i = lax.dynamic_slice_in_dim(k_pad, start, KEY_SPAN, axis=1)
        v_i = lax.dynamic_slice_in_dim(v_pad, start, KEY_SPAN, axis=1)
        kpos = start - WINDOW + jnp.arange(KEY_SPAN)
        valid = band & ((kpos >= 0) & (kpos < S))[None, :]
        s = jnp.einsum('bqhgd,bkhd->bhgqk', q_i, k_i).astype(f32) * scale + alibi
        s = jnp.where(valid, s, NEG_INF)
        sk = jnp.broadcast_to(sink_logit, s.shape[:-1] + (1,))
        p = jax.nn.softmax(jnp.concatenate([s, sk], axis=-1), axis=-1)[..., :KEY_SPAN]
        return jnp.einsum('bhgqk,bkhd->bqhgd', p.astype(v_i.dtype), v_i)

    o = lax.map(block, (jnp.arange(nb), q_blocks))
    o = jnp.moveaxis(o, 0, 1).reshape(B, S, ATT_HEADS * ATT_HEAD_DIM)
    return o @ w_out


def hier_moe(x, wg1, bg1, wg2, bg2, w_gate, w_up, w_down):
    T, D = x.shape
    f32 = jnp.float32
    group_prob = jax.nn.softmax((x @ wg1).astype(f32) + bg1, axis=-1)
    group = jnp.argmax(group_prob, axis=-1)
    p_group = jnp.max(group_prob, axis=-1)
    fine = jnp.einsum('td,gde->tge', x, wg2).astype(f32) + bg2
    fine = jnp.take_along_axis(fine, group[:, None, None], axis=1)[:, 0]
    top_p, top_e = lax.top_k(jax.nn.softmax(fine, axis=-1), TOP_K)
    gate = top_p / jnp.sum(top_p, -1, keepdims=True) * p_group[:, None]
    expert = group[:, None] * EXPERTS_PER_GROUP + top_e

    A = T * TOP_K
    eid = expert.reshape(A).astype(jnp.int32)
    tok = jnp.repeat(jnp.arange(T, dtype=jnp.int32), TOP_K)
    wt = gate.reshape(A)
    order = jnp.argsort(eid)
    eid_s, tok_s, wt_s = eid[order], tok[order], wt[order]
    counts = jnp.zeros((N_EXPERTS,), jnp.int32).at[eid].add(1)
    starts = jnp.cumsum(counts) - counts
    padded = (counts + MOE_BLOCK - 1) // MOE_BLOCK * MOE_BLOCK
    p_end = jnp.cumsum(padded)
    p_start = p_end - padded
    dest = p_start[eid_s] + jnp.arange(A, dtype=jnp.int32) - starts[eid_s]
    nblk = -(-A // MOE_BLOCK) + N_EXPERTS
    rows = nblk * MOE_BLOCK
    row_tok = jnp.full((rows,), T, jnp.int32).at[dest].set(tok_s)
    row_wt = jnp.zeros((rows,), f32).at[dest].set(wt_s)
    blk_exp = jnp.minimum(
        jnp.searchsorted(p_end, jnp.arange(nblk, dtype=jnp.int32) * MOE_BLOCK, side='right'),
        N_EXPERTS - 1)
    x_pad = jnp.concatenate([x, jnp.zeros((1, D), x.dtype)], axis=0)

    def expert_block(args):
        toks, e = args
        xb = x_pad[toks]
        hb = jax.nn.silu(xb @ w_gate[e]) * (xb @ w_up[e])
        return hb @ w_down[e]

    out = lax.map(expert_block, (row_tok.reshape(nblk, MOE_BLOCK), blk_exp))
    out = out.reshape(rows, D).astype(f32) * row_wt[:, None]
    y = jnp.zeros((T + 1, D), f32).at[row_tok].add(out)[:T]
    return y.astype(x.dtype)


def setup_inputs(seed: int = 0) -> dict:
    key = jax.random.key(seed)
    ks = jax.random.split(key, 37)
    f32 = jnp.float32

    def nrm(k, shape, scale):
        return jax.random.normal(k, shape, f32) * scale

    u = jax.random.uniform(ks[11], (N_REC, 2, RG_WIDTH), f32, minval=0.9, maxval=0.999)
    a_init = u ** (1.0 / RG_C)
    rg_lambda = jnp.log(a_init) - jnp.log1p(-a_init)
    return {
        "x_prompt": nrm(ks[0], (BATCH, SEQ, D_MODEL), 1.0),
        "x_sample": nrm(ks[1], (DEC_BATCH, DEC_SEQ, D_MODEL), 1.0),
        "norm_mix": 1.0 + nrm(ks[2], (DEPTH, D_MODEL), 0.05),
        "norm_ffn": 1.0 + nrm(ks[3], (DEPTH, D_MODEL), 0.05),
        "rec_w_in": nrm(ks[4], (N_REC, D_MODEL, REC_IN), D_MODEL ** -0.5),
        "rg_conv_w": nrm(ks[5], (N_REC, CONV_WIDTH, RG_WIDTH), CONV_WIDTH ** -0.5),
        "rg_conv_b": nrm(ks[6], (N_REC, RG_WIDTH), 0.02),
        "rg_wa": nrm(ks[7], (N_REC, 2, RG_BLOCKS, RG_BLOCK_DIM, RG_BLOCK_DIM), RG_BLOCK_DIM ** -0.5),
        "rg_ba": nrm(ks[8], (N_REC, 2, RG_WIDTH), 0.1),
        "rg_wx": nrm(ks[9], (N_REC, 2, RG_BLOCKS, RG_BLOCK_DIM, RG_BLOCK_DIM), RG_BLOCK_DIM ** -0.5),
        "rg_bx": nrm(ks[10], (N_REC, 2, RG_WIDTH), 0.1),
        "rg_lambda": rg_lambda,
        "rw_mu_l": jax.random.uniform(ks[12], (N_REC, RW_IN), f32, minval=0.0, maxval=0.5),
        "rw_mu_r": jax.random.uniform(ks[13], (N_REC, RW_IN), f32, minval=0.0, maxval=0.5),
        "rw_w0": nrm(ks[14], (N_REC, 2, RW_WIDTH), 1.0),
        "rw_w_up": nrm(ks[15], (N_REC, 2, DECAY_LORA, RW_WIDTH), 0.5 * DECAY_LORA ** -0.5),
        "rw_a0": nrm(ks[16], (N_REC, 2, RW_WIDTH), 0.5),
        "rw_a_up": nrm(ks[17], (N_REC, 2, ICL_LORA, RW_WIDTH), 0.5 * ICL_LORA ** -0.5),
        "rw_g_up": nrm(ks[18], (N_REC, GATE_LORA, RW_WIDTH), GATE_LORA ** -0.5),
        "rw_k_k": 1.0 + nrm(ks[19], (N_REC, RW_WIDTH), 0.1),
        "rw_k_a": 1.0 + nrm(ks[20], (N_REC, RW_WIDTH), 0.1),
        "rw_r_k": nrm(ks[21], (N_REC, RW_HEADS, RW_HEAD_DIM), 0.1),
        "rw_ln_w": 1.0 + nrm(ks[22], (N_REC, RW_WIDTH), 0.05),
        "rw_ln_b": nrm(ks[23], (N_REC, RW_WIDTH), 0.02),
        "rec_w_out": nrm(ks[24], (N_REC, RG_WIDTH + RW_WIDTH, D_MODEL), 0.5 * (RG_WIDTH + RW_WIDTH) ** -0.5),
        "att_w_in": nrm(ks[25], (N_ATT, D_MODEL, ATT_IN), D_MODEL ** -0.5),
        "att_q_norm": 1.0 + nrm(ks[26], (N_ATT, ATT_HEAD_DIM), 0.05),
        "att_k_norm": 1.0 + nrm(ks[27], (N_ATT, ATT_HEAD_DIM), 0.05),
        "att_sink": nrm(ks[28], (N_ATT, ATT_HEADS), 0.5),
        "att_w_out": nrm(ks[29], (N_ATT, ATT_HEADS * ATT_HEAD_DIM, D_MODEL), 0.5 * (ATT_HEADS * ATT_HEAD_DIM) ** -0.5),
        "moe_wg1": nrm(ks[30], (DEPTH, D_MODEL, N_GROUPS), D_MODEL ** -0.5),
        "moe_bg1": nrm(ks[31], (DEPTH, N_GROUPS), 0.01),
        "moe_wg2": nrm(ks[32], (DEPTH, N_GROUPS, D_MODEL, EXPERTS_PER_GROUP), D_MODEL ** -0.5),
        "moe_bg2": nrm(ks[33], (DEPTH, N_GROUPS, EXPERTS_PER_GROUP), 0.01),
        "moe_w_gate": nrm(ks[34], (DEPTH, N_EXPERTS, D_MODEL, EXPERT_FF), D_MODEL ** -0.5),
        "moe_w_up": nrm(ks[35], (DEPTH, N_EXPERTS, D_MODEL, EXPERT_FF), D_MODEL ** -0.5),
        "moe_w_down": nrm(ks[36], (DEPTH, N_EXPERTS, EXPERT_FF, D_MODEL), 0.5 * EXPERT_FF ** -0.5),
    }


def reference(x_prompt, x_sample, norm_mix, norm_ffn, rec_w_in, rg_conv_w, rg_conv_b, rg_wa, rg_ba,
              rg_wx, rg_bx, rg_lambda, rw_mu_l, rw_mu_r, rw_w0, rw_w_up, rw_a0, rw_a_up, rw_g_up,
              rw_k_k, rw_k_a, rw_r_k, rw_ln_w, rw_ln_b, rec_w_out, att_w_in, att_q_norm, att_k_norm,
              att_sink, att_w_out, moe_wg1, moe_bg1, moe_wg2, moe_bg2, moe_w_gate, moe_w_up, moe_w_down):
    def trunk(x):
        B, S, D = x.shape
        for i in range(DEPTH):
            j = i // 2
            h = rms_norm(x, norm_mix[i])
            if i % 2 == 0:
                mix = recurrent_mixer(h, rec_w_in[j], rg_conv_w[j], rg_conv_b[j], rg_wa[j], rg_ba[j],
                                      rg_wx[j], rg_bx[j], rg_lambda[j], rw_mu_l[j], rw_mu_r[j],
                                      rw_w0[j], rw_w_up[j], rw_a0[j], rw_a_up[j], rw_g_up[j],
                                      rw_k_k[j], rw_k_a[j], rw_r_k[j], rw_ln_w[j], rw_ln_b[j],
                                      rec_w_out[j])
            else:
                mix = window_attention(h, att_w_in[j], att_q_norm[j], att_k_norm[j], att_sink[j],
                                       att_w_out[j])
            x = x + mix.astype(x.dtype)
            h = rms_norm(x, norm_ffn[i])
            ffn = hier_moe(h.reshape(B * S, D), moe_wg1[i], moe_bg1[i], moe_wg2[i], moe_bg2[i],
                           moe_w_gate[i], moe_w_up[i], moe_w_down[i])
            x = x + ffn.reshape(B, S, D).astype(x.dtype)
        return x

    y_prompt = trunk(x_prompt)
    y_sample = trunk(x_sample)
    return (y_prompt, y_sample)
```

```python
import functools
import math

import jax
import jax.numpy as jnp
from jax import lax
from jax.experimental import pallas as pl
from jax.experimental.pallas import tpu as pltpu

F32 = jnp.float32
BF16 = jnp.bfloat16

D_MODEL = 1024
RG_WIDTH = 512
RG_BLOCKS = 8
RG_BLOCK_DIM = 64
CONV_WIDTH = 4
RG_C = 8.0
RW_HEADS = 8
RW_HEAD_DIM = 64
RW_WIDTH = 512
DECAY_LORA = 32
ICL_LORA = 32
GATE_LORA = 64
RW_IN = 3 * RW_WIDTH + DECAY_LORA + ICL_LORA + GATE_LORA
REC_IN = 2 * RG_WIDTH + RW_IN
RW_GN_EPS = 64e-5
ATT_HEADS = 16
ATT_KV_HEADS = 4
ATT_GROUP = 4
ATT_HEAD_DIM = 64
WINDOW = 128
ATT_BLOCK = 128
ATT_IN = (ATT_HEADS + 2 * ATT_KV_HEADS) * ATT_HEAD_DIM
N_GROUPS = 4
EXPERTS_PER_GROUP = 8
N_EXPERTS = 32
TOP_K = 2
EXPERT_FF = 512
RMS_EPS = 1e-6
NEG_INF = -1e30

LANES = 128
SUBLANES = 8
VMEM_LIMIT_BYTES = 56 * 1024 * 1024
ROW_TILE = 512
WKV_TILE = 256
WKV_CHUNK = 64
MOE_ROWS = 256
ROUTE_LANES = 128
DMA_ROWS = 128


def _cparams(*sem):
    return pltpu.CompilerParams(dimension_semantics=sem, vmem_limit_bytes=VMEM_LIMIT_BYTES)


def _dot(a, b):
    return jnp.dot(a.astype(BF16), b.astype(BF16), preferred_element_type=F32)


def _dot_nt(a, b):
    return lax.dot_general(a.astype(BF16), b.astype(BF16), (((1,), (1,)), ((), ())),
                           preferred_element_type=F32)


def _split3(x):
    h1 = x.astype(BF16)
    r1 = x - h1.astype(F32)
    h2 = r1.astype(BF16)
    h3 = (r1 - h2.astype(F32)).astype(BF16)
    return h1, h2, h3


def _dot_exact_rhs(x, e):
    h1, h2, h3 = _split3(x)
    return (jnp.dot(h1, e, preferred_element_type=F32) + jnp.dot(h2, e, preferred_element_type=F32)
            + jnp.dot(h3, e, preferred_element_type=F32))


def _dot_exact_lhs(e, x):
    h1, h2, h3 = _split3(x)
    return (jnp.dot(e, h1, preferred_element_type=F32) + jnp.dot(e, h2, preferred_element_type=F32)
            + jnp.dot(e, h3, preferred_element_type=F32))


def _sigmoid(x):
    return 1.0 / (1.0 + jnp.exp(-x))


def _softplus(x):
    return jnp.maximum(x, 0.0) + jnp.log(1.0 + jnp.exp(-jnp.abs(x)))


def _gelu_tanh(x):
    return 0.5 * x * (1.0 + jnp.tanh(math.sqrt(2.0 / math.pi) * (x + 0.044715 * (x * x * x))))


def _head_ones(width, head):
    r = lax.broadcasted_iota(jnp.int32, (width, width), 0) // head
    c = lax.broadcasted_iota(jnp.int32, (width, width), 1) // head
    return jnp.where(r == c, 1.0, 0.0).astype(BF16)


def _norm_proj_kernel(x_ref, g_ref, w_ref, *out_refs, splits):
    x = x_ref[...]
    h = x * lax.rsqrt(jnp.mean(x * x, -1, keepdims=True) + RMS_EPS) * g_ref[...]
    hb = h.astype(BF16)
    off = 0
    for o_ref, n in zip(out_refs, splits):
        o_ref[...] = jnp.dot(hb, w_ref[:, off:off + n], preferred_element_type=F32).astype(o_ref.dtype)
        off += n


def norm_proj(x2d, g, w, splits, out_dtype=F32):
    T, D = x2d.shape
    N = w.shape[1]
    assert sum(splits) == N and T % ROW_TILE == 0
    return pl.pallas_call(
        functools.partial(_norm_proj_kernel, splits=splits),
        grid=(T // ROW_TILE,),
        in_specs=[pl.BlockSpec((ROW_TILE, D), lambda i: (i, 0)),
                  pl.BlockSpec((1, D), lambda i: (0, 0)),
                  pl.BlockSpec((D, N), lambda i: (0, 0))],
        out_specs=[pl.BlockSpec((ROW_TILE, n), lambda i: (i, 0)) for n in splits],
        out_shape=[jax.ShapeDtypeStruct((T, n), out_dtype) for n in splits],
        compiler_params=_cparams("parallel"),
        name="norm_proj",
    )(x2d, g.reshape(1, D), w.astype(BF16))


RG_HALF = RG_WIDTH // 2
RG_ROWS = 256
RG_PAD = SUBLANES


def _rglru_kernel(x_ref, g_ref, cw_ref, cb_ref, wg_ref, bg_ref, sp_ref, o_ref,
                  xp_ref, af_ref, bf_ref, ab_ref, bb_ref):
    S = x_ref.shape[1]
    C = RG_HALF
    xp_ref[0:RG_PAD, :] = jnp.zeros((RG_PAD, C), F32)
    xp_ref[RG_PAD + S:RG_PAD + S + RG_PAD, :] = jnp.zeros((RG_PAD, C), F32)
    xp_ref[RG_PAD:RG_PAD + S, :] = x_ref[0]
    cw = cw_ref[...]
    left = CONV_WIDTH // 2
    for c in range(S // RG_ROWS):
        r0 = c * RG_ROWS
        xc = cb_ref[...] + cw[0:1] * xp_ref[RG_PAD + r0 - left:RG_PAD + r0 - left + RG_ROWS, :]
        for k in range(1, CONV_WIDTH):
            s0 = RG_PAD + r0 + k - left
            xc = xc + cw[k:k + 1] * xp_ref[s0:s0 + RG_ROWS, :]
        z = _dot(xc, wg_ref[0]) + bg_ref[0]
        for d, (a_ref, b_ref) in enumerate(((af_ref, bf_ref), (ab_ref, bb_ref))):
            r = _sigmoid(z[:, (2 * d) * C:(2 * d + 1) * C])
            i = _sigmoid(z[:, (2 * d + 1) * C:(2 * d + 2) * C])
            log_a = -RG_C * r * sp_ref[0][:, d * C:(d + 1) * C]
            a = jnp.exp(log_a)
            a_ref[r0:r0 + RG_ROWS, :] = a
            b_ref[r0:r0 + RG_ROWS, :] = jnp.sqrt(1.0 - jnp.exp(2.0 * log_a)) * (i * xc)

    def body(n, carry):
        hf, hb = carry
        r0 = pl.multiple_of(n * SUBLANES, SUBLANES)
        a8 = af_ref[pl.ds(r0, SUBLANES), :]
        b8 = bf_ref[pl.ds(r0, SUBLANES), :]
        rows = []
        for j in range(SUBLANES):
            hf = a8[j:j + 1] * hf + b8[j:j + 1]
            rows.append(hf)
        bf_ref[pl.ds(r0, SUBLANES), :] = jnp.concatenate(rows, 0)
        r1 = pl.multiple_of(S - SUBLANES - n * SUBLANES, SUBLANES)
        a8 = ab_ref[pl.ds(r1, SUBLANES), :]
        b8 = bb_ref[pl.ds(r1, SUBLANES), :]
        rows = []
        for j in range(SUBLANES - 1, -1, -1):
            hb = a8[j:j + 1] * hb + b8[j:j + 1]
            rows.append(hb)
        bb_ref[pl.ds(r1, SUBLANES), :] = jnp.concatenate(rows[::-1], 0)
        return hf, hb

    zero = jnp.zeros((1, C), F32)
    lax.fori_loop(0, S // SUBLANES, body, (zero, zero))
    for c in range(S // RG_ROWS):
        sl = slice(c * RG_ROWS, (c + 1) * RG_ROWS)
        o_ref[0, sl, :] = ((bf_ref[sl, :] + bb_ref[sl, :]) * _gelu_tanh(g_ref[0, sl, :])).astype(o_ref.dtype)


def rglru_branch(rg_x, rg_g, conv_w, conv_b, wa, ba, wx, bx, lam):
    B, S, _ = rg_x.shape
    C = RG_HALF
    nb = C // RG_BLOCK_DIM

    def bdiag(w):
        w = w.reshape(2, nb, RG_BLOCK_DIM, RG_BLOCK_DIM)
        eye = jnp.eye(nb, dtype=w.dtype)
        return jnp.einsum('hnij,nm->hnimj', w, eye).reshape(2, C, C)

    wg = jnp.concatenate([bdiag(wa[0]), bdiag(wx[0]), bdiag(wa[1]), bdiag(wx[1])], axis=-1).astype(BF16)

    def halves(v):
        return v.reshape(2, 1, C)

    bg = jnp.concatenate([halves(ba[0]), halves(bx[0]), halves(ba[1]), halves(bx[1])], axis=-1)
    sp = jax.nn.softplus(-lam.astype(F32))
    spg = jnp.concatenate([halves(sp[0]), halves(sp[1])], axis=-1)
    return pl.pallas_call(
        _rglru_kernel,
        grid=(B, 2),
        in_specs=[pl.BlockSpec((1, S, C), lambda b, c: (b, 0, c)),
                  pl.BlockSpec((1, S, C), lambda b, c: (b, 0, c)),
                  pl.BlockSpec((CONV_WIDTH, C), lambda b, c: (0, c)),
                  pl.BlockSpec((1, C), lambda b, c: (0, c)),
                  pl.BlockSpec((1, C, 4 * C), lambda b, c: (c, 0, 0)),
                  pl.BlockSpec((1, 1, 4 * C), lambda b, c: (c, 0, 0)),
                  pl.BlockSpec((1, 1, 2 * C), lambda b, c: (c, 0, 0))],
        out_specs=pl.BlockSpec((1, S, C), lambda b, c: (b, 0, c)),
        out_shape=jax.ShapeDtypeStruct((B, S, RG_WIDTH), BF16),
        scratch_shapes=[pltpu.VMEM((S + 2 * RG_PAD, C), F32)] + [pltpu.VMEM((S, C), F32)] * 4,
        compiler_params=_cparams("parallel", "parallel"),
        name="rglru",
    )(rg_x, rg_g, conv_w, conv_b.reshape(1, RG_WIDTH), wg, bg, spg)


def _rwkv_prep_kernel(u_ref, up_ref, un_ref, mul_ref, mur_ref, wl_ref, w0_ref, a0_ref, kk_ref, ka_ref,
                      rk_ref, rt_ref, kt_ref, bt_ref, kh_ref, v_ref, le_ref, bonus_ref, g_ref):
    i = pl.program_id(1)
    nt = pl.num_programs(1)
    TS = u_ref.shape[1]
    W = RW_WIDTH
    u = u_ref[0]
    prow = jnp.where(i == 0, 0.0, up_ref[0][SUBLANES - 1:SUBLANES, :])
    nrow = jnp.where(i == nt - 1, 0.0, un_ref[0][0:1, :])
    rows = lax.broadcasted_iota(jnp.int32, (TS, 1), 0)
    prev = jnp.where(rows == 0, prow, pltpu.roll(u, 1, 0))
    nxt = jnp.where(rows == TS - 1, nrow, pltpu.roll(u, TS - 1, 0))
    m = u + mul_ref[...] * (prev - u) + mur_ref[...] * (nxt - u)
    r = m[:, 0:W]
    k = m[:, W:2 * W]
    v = m[:, 2 * W:3 * W]
    tail = m[:, 3 * W:3 * W + LANES]
    lane = lax.broadcasted_iota(jnp.int32, (1, LANES), 1)
    z = jnp.where(lane < DECAY_LORA, jnp.tanh(tail),
                  jnp.where(lane < DECAY_LORA + ICL_LORA, tail, _sigmoid(tail)))
    lo = _dot(z, wl_ref[...])
    ones = _head_ones(W, RW_HEAD_DIM)
    kkr = k * kk_ref[...]
    kk = kkr / jnp.maximum(jnp.sqrt(_dot_exact_rhs(kkr * kkr, ones)), 1e-12)
    tr = lax.broadcasted_iota(jnp.int32, (TS, TS), 0)
    tc = lax.broadcasted_iota(jnp.int32, (TS, TS), 1)
    same = (tr // WKV_CHUNK) == (tc // WKV_CHUNK)
    kd_sum = jnp.zeros((TS, W), F32)
    nch = TS // WKV_CHUNK
    for d in range(2):
        wlog = -_softplus(-(w0_ref[d:d + 1, :] + lo[:, d * W:(d + 1) * W])) - 0.5
        ld = -jnp.exp(wlog)
        a = _sigmoid(a0_ref[d:d + 1, :] + lo[:, (2 + d) * W:(3 + d) * W])
        kd = k * (1.0 + (a - 1.0) * ka_ref[...])
        kd_sum = kd_sum + kd
        tri = jnp.where(same & ((tc <= tr) if d == 0 else (tc >= tr)), 1.0, 0.0).astype(BF16)
        L = _dot_exact_lhs(tri, ld)
        en = jnp.exp(-L)
        rt_ref[d, 0] = r * jnp.exp(L)
        kt_ref[d, 0] = kk * jnp.exp(L - ld)
        bt_ref[d, 0] = kk * a * en
        kh_ref[d, 0] = kd * en
        ends = [L[(c + 1) * WKV_CHUNK - 1:(c + 1) * WKV_CHUNK] if d == 0 else L[c * WKV_CHUNK:c * WKV_CHUNK + 1]
                for c in range(nch)]
        le_ref[d, 0, 0] = jnp.concatenate(ends, 0)
    v_ref[0] = v
    bonus_ref[0] = _dot_exact_rhs(r * kd_sum * rk_ref[...], ones) * v
    g_ref[0] = lo[:, 4 * W:5 * W]


def rwkv_prep(rw, mu_l, mu_r, w0, w_up, a0, a_up, g_up, k_k, k_a, r_k):
    B, S, _ = rw.shape
    TS = WKV_TILE
    W = RW_WIDTH
    nt = S // TS
    nch = TS // WKV_CHUNK
    hb = TS // SUBLANES
    wl = jnp.zeros((LANES, 5 * W), F32)
    wl = wl.at[0:DECAY_LORA, 0:W].set(w_up[0]).at[0:DECAY_LORA, W:2 * W].set(w_up[1])
    o = DECAY_LORA
    wl = wl.at[o:o + ICL_LORA, 2 * W:3 * W].set(a_up[0]).at[o:o + ICL_LORA, 3 * W:4 * W].set(a_up[1])
    o += ICL_LORA
    wl = wl.at[o:o + GATE_LORA, 4 * W:5 * W].set(g_up)
    vec = lambda n: pl.BlockSpec((1, n), lambda b, i: (0, 0))
    big = lambda: pl.BlockSpec((2, 1, TS, W), lambda b, i: (0, b, i, 0))
    one = lambda: pl.BlockSpec((1, TS, W), lambda b, i: (b, i, 0))
    return pl.pallas_call(
        _rwkv_prep_kernel,
        grid=(B, nt),
        in_specs=[pl.BlockSpec((1, TS, RW_IN), lambda b, i: (b, i, 0)),
                  pl.BlockSpec((1, SUBLANES, RW_IN), lambda b, i: (b, jnp.maximum(i * hb - 1, 0), 0)),
                  pl.BlockSpec((1, SUBLANES, RW_IN), lambda b, i: (b, jnp.minimum((i + 1) * hb, S // SUBLANES - 1), 0)),
                  vec(RW_IN), vec(RW_IN),
                  pl.BlockSpec((LANES, 5 * W), lambda b, i: (0, 0)),
                  pl.BlockSpec((2, W), lambda b, i: (0, 0)),
                  pl.BlockSpec((2, W), lambda b, i: (0, 0)),
                  vec(W), vec(W), vec(W)],
        out_specs=[big(), big(), big(), big(), one(),
                   pl.BlockSpec((2, 1, 1, nch, W), lambda b, i: (0, b, i, 0, 0)),
                   one(), one()],
        out_shape=[jax.ShapeDtypeStruct((2, B, S, W), F32)] * 4
        + [jax.ShapeDtypeStruct((B, S, W), F32),
           jax.ShapeDtypeStruct((2, B, nt, nch, W), F32),
           jax.ShapeDtypeStruct((B, S, W), F32),
           jax.ShapeDtypeStruct((B, S, W), F32)],
        compiler_params=_cparams("parallel", "parallel"),
        name="rwkv_prep",
    )(rw, rw, rw, mu_l.reshape(1, RW_IN), mu_r.reshape(1, RW_IN), wl.astype(BF16), w0, a0,
      k_k.reshape(1, W), k_a.reshape(1, W), r_k.reshape(1, W))


def _wkv_kernel(rt_ref, kt_ref, bt_ref, kh_ref, v_ref, le_ref, y_ref, h_ref, *, reverse):
    i = pl.program_id(2)

    @pl.when(i == 0)
    def _():
        h_ref[...] = jnp.zeros_like(h_ref)

    TS = WKV_TILE
    C = WKV_CHUNK
    N = RW_HEAD_DIM
    rt = rt_ref[0, 0]
    kt = kt_ref[0, 0]
    bt = bt_ref[0, 0]
    kh = kh_ref[0, 0]
    v = v_ref[0]
    pc_all = jnp.exp(le_ref[0, 0, 0])
    lane = lax.broadcasted_iota(jnp.int32, (1, 2 * N), 1)
    head0 = lane < N
    tr = lax.broadcasted_iota(jnp.int32, (TS, TS), 0)
    tc = lax.broadcasted_iota(jnp.int32, (TS, TS), 1)
    same = (tr // C) == (tc // C)
    strict = same & ((tc > tr) if reverse else (tc < tr))
    incl = same & ((tc >= tr) if reverse else (tc <= tr))
    eye = tr == tc
    rhs = jnp.concatenate([bt, kh], 0)
    per_head = []
    for hmask in (head0, jnp.logical_not(head0)):
        lhs = jnp.concatenate([jnp.where(hmask, kt, 0.0), jnp.where(hmask, rt, 0.0)], 0)
        gram = _dot_nt(lhs, rhs)
        A = jnp.where(strict, gram[0:TS, 0:TS], 0.0)
        Bm = jnp.where(strict, gram[0:TS, TS:2 * TS], 0.0)
        Rb = jnp.where(incl, gram[TS:2 * TS, 0:TS], 0.0)
        Rk = jnp.where(incl, gram[TS:2 * TS, TS:2 * TS], 0.0)
        Tm = jnp.where(eye, 1.0, 0.0) - A
        Q = A
        for _ in range(int(math.log2(C)) - 1):
            Q = _dot(Q, Q)
            Tm = Tm + _dot(Tm, Q)
        x1 = _dot(Tm, jnp.concatenate([kt, _dot(Bm, v)], 1))
        per_head.append((x1[:, 0:2 * N], x1[:, 2 * N:4 * N], Rb, Rk))
    K2 = jnp.where(head0, per_head[0][0], per_head[1][0])
    V2 = jnp.where(head0, per_head[0][1], per_head[1][1])
    kv2 = jnp.concatenate([K2, V2], 1)
    outs = []
    for (_, _, Rb, Rk) in per_head:
        x2 = _dot(Rb, kv2)
        outs.append((rt - x2[:, 0:2 * N], _dot(Rk, v) - x2[:, 2 * N:4 * N]))
    R2 = jnp.where(head0, outs[0][0], outs[1][0])
    Y2 = jnp.where(head0, outs[0][1], outs[1][1])
    hr = lax.broadcasted_iota(jnp.int32, (2 * N, 2 * N), 0)
    hc = lax.broadcasted_iota(jnp.int32, (2 * N, 2 * N), 1)
    bdiag = (hr // N) == (hc // N)
    heye = hr == hc
    nch = TS // C
    order = range(nch - 1, -1, -1) if reverse else range(nch)
    H = h_ref[...]
    for c in order:
        sl = slice(c * C, (c + 1) * C)
        pc = pc_all[c:c + 1]
        bh = (bt[sl] * pc).T
        khh = (kh[sl] * pc).T
        Mc = jnp.where(heye, pc, 0.0) - jnp.where(bdiag, _dot(bh, K2[sl]), 0.0)
        Gc = jnp.where(bdiag, _dot(khh, v[sl]) - _dot(bh, V2[sl]), 0.0)
        y_ref[0, sl, :] = _dot(R2[sl], H) + Y2[sl]
        H = _dot(Mc, H) + Gc
    h_ref[...] = H


def wkv_scan_dir(rt, kt, bt, kh, v, le, d):
    _, B, S, W = rt.shape
    TS = WKV_TILE
    nt = S // TS
    nch = TS // WKV_CHUNK
    P = 2 * RW_HEAD_DIM
    reverse = d == 1
    tile = (lambda i: nt - 1 - i) if reverse else (lambda i: i)
    big = lambda: pl.BlockSpec((1, 1, TS, P), lambda b, p, i: (d, b, tile(i), p))
    return pl.pallas_call(
        functools.partial(_wkv_kernel, reverse=reverse),
        grid=(B, W // P, nt),
        in_specs=[big(), big(), big(), big(),
                  pl.BlockSpec((1, TS, P), lambda b, p, i: (b, tile(i), p)),
                  pl.BlockSpec((1, 1, 1, nch, P), lambda b, p, i: (d, b, tile(i), 0, p))],
        out_specs=pl.BlockSpec((1, TS, P), lambda b, p, i: (b, tile(i), p)),
        out_shape=jax.ShapeDtypeStruct((B, S, W), F32),
        scratch_shapes=[pltpu.VMEM((P, P), F32)],
        compiler_params=_cparams("parallel", "parallel", "arbitrary"),
        name="wkv_bwd" if reverse else "wkv_fwd",
    )(rt, kt, bt, kh, v, le)


def _rec_out_kernel(x_ref, rg_ref, yf_ref, yb_ref, bonus_ref, g_ref, lnw_ref, lnb_ref, w_ref, o_ref):
    W = RW_WIDTH
    ones = _head_ones(W, RW_HEAD_DIM)
    y = yf_ref[...] + yb_ref[...]
    inv_n = 1.0 / RW_HEAD_DIM
    mu = _dot_exact_rhs(y, ones) * inv_n
    yc = y - mu
    var = _dot_exact_rhs(yc * yc, ones) * inv_n
    yn = yc * lax.rsqrt(var + RW_GN_EPS) * lnw_ref[...] + lnb_ref[...]
    rw_out = (yn + bonus_ref[...]) * g_ref[...]
    mix = (jnp.dot(rg_ref[...], w_ref[0:RG_WIDTH, :], preferred_element_type=F32)
           + jnp.dot(rw_out.astype(BF16), w_ref[RG_WIDTH:RG_WIDTH + W, :], preferred_element_type=F32))
    o_ref[...] = x_ref[...] + mix


def rec_out(x2d, rg_out, y_f, y_b, bonus, g, ln_w, ln_b, w_out):
    T, D = x2d.shape
    W = RW_WIDTH
    row = lambda n: pl.BlockSpec((ROW_TILE, n), lambda i: (i, 0))
    vec = lambda n: pl.BlockSpec((1, n), lambda i: (0, 0))
    return pl.pallas_call(
        _rec_out_kernel,
        grid=(T // ROW_TILE,),
        in_specs=[row(D), row(RG_WIDTH), row(W), row(W), row(W), row(W), vec(W), vec(W),
                  pl.BlockSpec((RG_WIDTH + W, D), lambda i: (0, 0))],
        out_specs=row(D),
        out_shape=jax.ShapeDtypeStruct((T, D), F32),
        compiler_params=_cparams("parallel"),
        name="rec_out",
    )(x2d, rg_out, y_f, y_b, bonus, g, ln_w.reshape(1, W), ln_b.reshape(1, W), w_out.astype(BF16))


def recurrent_layer(x, norm_g, w_in, conv_w, conv_b, rg_wa, rg_ba, rg_wx, rg_bx, rg_lambda,
                    mu_l, mu_r, w0, w_up, a0, a_up, g_up, k_k, k_a, r_k, ln_w, ln_b, w_out):
    B, S, D = x.shape
    T = B * S
    x2d = x.reshape(T, D)
    rg_x, rg_g, rw = norm_proj(x2d, norm_g, w_in, (RG_WIDTH, RG_WIDTH, RW_IN))
    rg_out = rglru_branch(rg_x.reshape(B, S, RG_WIDTH), rg_g.reshape(B, S, RG_WIDTH), conv_w, conv_b,
                          rg_wa, rg_ba, rg_wx, rg_bx, rg_lambda)
    rt, kt, bt, kh, v, le, bonus, g = rwkv_prep(rw.reshape(B, S, RW_IN), mu_l, mu_r, w0, w_up, a0, a_up,
                                                g_up, k_k, k_a, r_k.reshape(-1))
    y_f = wkv_scan_dir(rt, kt, bt, kh, v, le, 0)
    y_b = wkv_scan_dir(rt, kt, bt, kh, v, le, 1)
    flat = lambda t: t.reshape(T, -1)
    out = rec_out(x2d, flat(rg_out), flat(y_f), flat(y_b), flat(bonus), flat(g), ln_w, ln_b, w_out)
    return out.reshape(B, S, D)


def _alibi_slope(h):
    return 2.0 ** (-8.0 * (h + 1) / ATT_HEADS)


def _attn_kernel(sink_ref, x_ref, q_ref, kp_ref, kc_ref, kn_ref, vp_ref, vc_ref, vn_ref, qn_ref, kn_w_ref,
                 w_ref, o_ref):
    i = pl.program_id(1)
    nb = pl.num_programs(1)
    BLK = ATT_BLOCK
    Dh = ATT_HEAD_DIM
    span = 3 * BLK
    q = q_ref[0]
    kc = jnp.concatenate([kp_ref[0], kc_ref[0], kn_ref[0]], 0)
    vc = jnp.concatenate([vp_ref[0], vc_ref[0], vn_ref[0]], 0)
    row = lax.broadcasted_iota(jnp.int32, (BLK, span), 0)
    col = lax.broadcasted_iota(jnp.int32, (BLK, span), 1)
    rel = col - WINDOW - row
    kpos = col + (i - 1) * BLK
    valid = (jnp.abs(rel) <= WINDOW) & (kpos >= 0) & (kpos < nb * BLK)
    dist = jnp.abs(rel).astype(F32)
    scale = Dh ** -0.5
    outs = []
    for g in range(ATT_KV_HEADS):
        kg = kc[:, g * Dh:(g + 1) * Dh]
        kg = (kg * lax.rsqrt(jnp.mean(kg * kg, -1, keepdims=True) + RMS_EPS) * kn_w_ref[...]).astype(BF16)
        vg = vc[:, g * Dh:(g + 1) * Dh].astype(BF16)
        for j in range(ATT_GROUP):
            h = g * ATT_GROUP + j
            qh = q[:, h * Dh:(h + 1) * Dh]
            qh = qh * lax.rsqrt(jnp.mean(qh * qh, -1, keepdims=True) + RMS_EPS) * qn_ref[...]
            s = _dot_nt(qh, kg) * scale - _alibi_slope(h) * dist
            s = jnp.where(valid, s, NEG_INF)
            sk = sink_ref[h]
            m = jnp.maximum(jnp.max(s, -1, keepdims=True), sk)
            p = jnp.exp(s - m)
            den = jnp.sum(p, -1, keepdims=True) + jnp.exp(sk - m)
            outs.append(_dot(p, vg) / den)
    o = jnp.concatenate(outs, -1).astype(BF16)
    o_ref[0] = x_ref[0] + jnp.dot(o, w_ref[...], preferred_element_type=F32)


def attention_layer(x, norm_g, w_in, q_norm, k_norm, sink, w_out):
    B, S, D = x.shape
    T = B * S
    QW = ATT_HEADS * ATT_HEAD_DIM
    KW = ATT_KV_HEADS * ATT_HEAD_DIM
    q, k, v = norm_proj(x.reshape(T, D), norm_g, w_in, (QW, KW, KW))
    q = q.reshape(B, S, QW)
    k = k.reshape(B, S, KW)
    v = v.reshape(B, S, KW)
    nb = S // ATT_BLOCK
    prev = lambda b, i: (b, jnp.maximum(i - 1, 0), 0)
    cur = lambda b, i: (b, i, 0)
    nxt = lambda b, i: (b, jnp.minimum(i + 1, nb - 1), 0)
    kv = lambda f: pl.BlockSpec((1, ATT_BLOCK, KW), f)
    return pl.pallas_call(
        _attn_kernel,
        grid=(B, nb),
        in_specs=[pl.BlockSpec(memory_space=pltpu.SMEM),
                  pl.BlockSpec((1, ATT_BLOCK, D), cur),
                  pl.BlockSpec((1, ATT_BLOCK, QW), cur),
                  kv(prev), kv(cur), kv(nxt), kv(prev), kv(cur), kv(nxt),
                  pl.BlockSpec((1, ATT_HEAD_DIM), lambda b, i: (0, 0)),
                  pl.BlockSpec((1, ATT_HEAD_DIM), lambda b, i: (0, 0)),
                  pl.BlockSpec((QW, D), lambda b, i: (0, 0))],
        out_specs=pl.BlockSpec((1, ATT_BLOCK, D), cur),
        out_shape=jax.ShapeDtypeStruct((B, S, D), F32),
        compiler_params=_cparams("parallel", "parallel"),
        name="window_attn",
    )(sink.astype(F32), x, q, k, k, k, v, v, v, q_norm.reshape(1, -1), k_norm.reshape(1, -1),
      w_out.astype(BF16))


ROUTE_OFF = N_GROUPS


def _router_kernel(x_ref, g_ref, w1_ref, w2_ref, b_ref, h_ref, ri_ref, rg_ref, cnt_ref, run_ref):
    i = pl.program_id(0)

    @pl.when(i == 0)
    def _():
        run_ref[...] = jnp.zeros_like(run_ref)

    TM = x_ref.shape[0]
    x = x_ref[...]
    h = x * lax.rsqrt(jnp.mean(x * x, -1, keepdims=True) + RMS_EPS) * g_ref[...]
    h_ref[...] = h
    h1 = h.astype(BF16)
    h2 = (h - h1.astype(F32)).astype(BF16)
    lg = (jnp.dot(h1, w1_ref[...], preferred_element_type=F32) + jnp.dot(h1, w2_ref[...], preferred_element_type=F32)
          + jnp.dot(h2, w1_ref[...], preferred_element_type=F32)) + b_ref[...]
    lane = lax.broadcasted_iota(jnp.int32, (TM, ROUTE_LANES), 1)
    far = ROUTE_LANES
    gmask = lane < N_GROUPS
    gl = jnp.where(gmask, lg, NEG_INF)
    gm = jnp.max(gl, -1, keepdims=True)
    p_group = 1.0 / jnp.sum(jnp.where(gmask, jnp.exp(gl - gm), 0.0), -1, keepdims=True)
    group = jnp.min(jnp.where(gl == gm, lane, far), -1, keepdims=True)
    fmask = (lane >= ROUTE_OFF) & (lane < ROUTE_OFF + N_EXPERTS) & ((lane - ROUTE_OFF) // EXPERTS_PER_GROUP == group)
    fl = jnp.where(fmask, lg, NEG_INF)
    m1 = jnp.max(fl, -1, keepdims=True)
    ssum = jnp.sum(jnp.where(fmask, jnp.exp(fl - m1), 0.0), -1, keepdims=True)
    i1 = jnp.min(jnp.where(fl == m1, lane, far), -1, keepdims=True)
    fl2 = jnp.where(lane == i1, NEG_INF, fl)
    m2 = jnp.max(fl2, -1, keepdims=True)
    i2 = jnp.min(jnp.where(fl2 == m2, lane, far), -1, keepdims=True)
    p1 = 1.0 / ssum
    p2 = jnp.exp(m2 - m1) / ssum
    norm = p_group / (p1 + p2)
    oh = jnp.where((lane == i1) | (lane == i2), 1.0, 0.0)
    tr = lax.broadcasted_iota(jnp.int32, (TM, TM), 0)
    tc = lax.broadcasted_iota(jnp.int32, (TM, TM), 1)
    before = jnp.where(tc < tr, 1.0, 0.0).astype(BF16)
    pre = jnp.dot(before, oh.astype(BF16), preferred_element_type=F32) + run_ref[...]
    rank1 = jnp.sum(jnp.where(lane == i1, pre, 0.0), -1, keepdims=True)
    rank2 = jnp.sum(jnp.where(lane == i2, pre, 0.0), -1, keepdims=True)
    total = run_ref[...] + jnp.sum(oh, 0, keepdims=True)
    run_ref[...] = total
    cnt_ref[...] = total.astype(jnp.int32)
    ri_ref[...] = jnp.where(lane == 0, i1 - ROUTE_OFF,
                            jnp.where(lane == 1, i2 - ROUTE_OFF,
                                      jnp.where(lane == 2, rank1.astype(jnp.int32),
                                                jnp.where(lane == 3, rank2.astype(jnp.int32), 0))))
    rg_ref[...] = jnp.where(lane == 0, p1 * norm, jnp.where(lane == 1, p2 * norm, 0.0))


def moe_router(x2d, g, wg1, bg1, wg2, bg2):
    T, D = x2d.shape
    wr = jnp.zeros((D, ROUTE_LANES), F32)
    wr = wr.at[:, 0:N_GROUPS].set(wg1)
    wr = wr.at[:, ROUTE_OFF:ROUTE_OFF + N_EXPERTS].set(jnp.moveaxis(wg2, 0, 1).reshape(D, N_EXPERTS))
    w1 = wr.astype(BF16)
    w2 = (wr - w1.astype(F32)).astype(BF16)
    bias = jnp.zeros((1, ROUTE_LANES), F32)
    bias = bias.at[0, 0:N_GROUPS].set(bg1).at[0, ROUTE_OFF:ROUTE_OFF + N_EXPERTS].set(bg2.reshape(-1))
    row = lambda n: pl.BlockSpec((ROW_TILE, n), lambda i: (i, 0))
    fixed = lambda r, n: pl.BlockSpec((r, n), lambda i: (0, 0))
    return pl.pallas_call(
        _router_kernel,
        grid=(T // ROW_TILE,),
        in_specs=[row(D), fixed(1, D), fixed(D, ROUTE_LANES), fixed(D, ROUTE_LANES), fixed(1, ROUTE_LANES)],
        out_specs=[row(D), row(ROUTE_LANES), row(ROUTE_LANES), fixed(1, ROUTE_LANES)],
        out_shape=[jax.ShapeDtypeStruct((T, D), F32),
                   jax.ShapeDtypeStruct((T, ROUTE_LANES), jnp.int32),
                   jax.ShapeDtypeStruct((T, ROUTE_LANES), F32),
                   jax.ShapeDtypeStruct((1, ROUTE_LANES), jnp.int32)],
        scratch_shapes=[pltpu.VMEM((1, ROUTE_LANES), F32)],
        compiler_params=_cparams("arbitrary"),
        name="moe_router",
    )(x2d, g.reshape(1, D), w1, w2, bias)


def _row_copy(src, src_row, dst, dst_row, sem):
    return pltpu.make_async_copy(src.at[pl.ds(src_row, 1)], dst.at[pl.ds(dst_row, 1)], sem)


def _dispatch_kernel(dest_ref, h_ref, zeros_ref, xs_ref, sem):
    del zeros_ref
    n = h_ref.shape[0]

    def issue(r, c):
        for k in range(TOP_K):
            _row_copy(h_ref, r, xs_ref, dest_ref[0, 0, TOP_K * r + k], sem).start()
        return c

    lax.fori_loop(0, n, issue, 0)

    def drain(r, c):
        for k in range(TOP_K):
            _row_copy(h_ref, r, xs_ref, dest_ref[0, 0, TOP_K * r + k], sem).wait()
        return c

    lax.fori_loop(0, n, drain, 0)


def moe_dispatch(h2d, dest, rows):
    T, D = h2d.shape
    nt = T // DMA_ROWS
    return pl.pallas_call(
        _dispatch_kernel,
        grid=(nt,),
        in_specs=[pl.BlockSpec((1, 1, TOP_K * DMA_ROWS), lambda i: (i, 0, 0), memory_space=pltpu.SMEM),
                  pl.BlockSpec((DMA_ROWS, D), lambda i: (i, 0)),
                  pl.BlockSpec(memory_space=pl.ANY)],
        out_specs=pl.BlockSpec(memory_space=pl.ANY),
        out_shape=jax.ShapeDtypeStruct((rows, D), F32),
        scratch_shapes=[pltpu.SemaphoreType.DMA(())],
        input_output_aliases={2: 0},
        compiler_params=_cparams("arbitrary"),
        name="moe_dispatch",
    )(dest.reshape(nt, 1, TOP_K * DMA_ROWS), h2d, jnp.zeros((rows, D), F32))


def _expert_kernel(be_ref, nu_ref, x_ref, wg_ref, wu_ref, wd_ref, o_ref):
    i = pl.program_id(0)

    @pl.when(i < nu_ref[0])
    def _():
        xb = x_ref[...].astype(BF16)
        hg = jnp.dot(xb, wg_ref[0], preferred_element_type=F32)
        hu = jnp.dot(xb, wu_ref[0], preferred_element_type=F32)
        hb = (hg * _sigmoid(hg) * hu).astype(BF16)
        o_ref[...] = jnp.dot(hb, wd_ref[0], preferred_element_type=F32)

    @pl.when(i >= nu_ref[0])
    def _():
        o_ref[...] = jnp.zeros_like(o_ref)


def moe_experts(xs, blk_exp, n_used, w_gate, w_up, w_down):
    rows, D = xs.shape
    nblk = rows // MOE_ROWS
    F = EXPERT_FF
    return pl.pallas_call(
        _expert_kernel,
        grid_spec=pltpu.PrefetchScalarGridSpec(
            num_scalar_prefetch=2,
            grid=(nblk,),
            in_specs=[pl.BlockSpec((MOE_ROWS, D), lambda i, be, nu: (i, 0)),
                      pl.BlockSpec((1, D, F), lambda i, be, nu: (be[i], 0, 0)),
                      pl.BlockSpec((1, D, F), lambda i, be, nu: (be[i], 0, 0)),
                      pl.BlockSpec((1, F, D), lambda i, be, nu: (be[i], 0, 0))],
            out_specs=pl.BlockSpec((MOE_ROWS, D), lambda i, be, nu: (i, 0))),
        out_shape=jax.ShapeDtypeStruct((rows, D), F32),
        compiler_params=_cparams("arbitrary"),
        name="moe_experts",
    )(blk_exp, n_used, xs, w_gate, w_up, w_down)


def _combine_kernel(dest_ref, x_ref, gate_ref, eo_ref, o_ref, buf_ref, sem):
    n = x_ref.shape[0]

    def issue(r, c):
        for k in range(TOP_K):
            _row_copy(eo_ref, dest_ref[0, 0, TOP_K * r + k], buf_ref.at[k], r, sem).start()
        return c

    lax.fori_loop(0, n, issue, 0)

    def drain(r, c):
        for k in range(TOP_K):
            _row_copy(eo_ref, dest_ref[0, 0, TOP_K * r + k], buf_ref.at[k], r, sem).wait()
        return c

    lax.fori_loop(0, n, drain, 0)
    gate = gate_ref[...]
    ffn = gate[:, 0:1] * buf_ref[0]
    for k in range(1, TOP_K):
        ffn = ffn + gate[:, k:k + 1] * buf_ref[k]
    o_ref[...] = x_ref[...] + ffn


def moe_combine(x2d, gates, dest, eo):
    T, D = x2d.shape
    nt = T // DMA_ROWS
    return pl.pallas_call(
        _combine_kernel,
        grid=(nt,),
        in_specs=[pl.BlockSpec((1, 1, TOP_K * DMA_ROWS), lambda i: (i, 0, 0), memory_space=pltpu.SMEM),
                  pl.BlockSpec((DMA_ROWS, D), lambda i: (i, 0)),
                  pl.BlockSpec((DMA_ROWS, ROUTE_LANES), lambda i: (i, 0)),
                  pl.BlockSpec(memory_space=pl.ANY)],
        out_specs=pl.BlockSpec((DMA_ROWS, D), lambda i: (i, 0)),
        out_shape=jax.ShapeDtypeStruct((T, D), F32),
        scratch_shapes=[pltpu.VMEM((TOP_K, DMA_ROWS, D), F32), pltpu.SemaphoreType.DMA(())],
        compiler_params=_cparams("arbitrary"),
        name="moe_combine",
    )(dest.reshape(nt, 1, TOP_K * DMA_ROWS), x2d, gates, eo)


def moe_layer(x, norm_g, wg1, bg1, wg2, bg2, w_gate, w_up, w_down):
    B, S, D = x.shape
    T = B * S
    A = T * TOP_K
    x2d = x.reshape(T, D)
    h, route_i, route_g, counts = moe_router(x2d, norm_g, wg1, bg1, wg2, bg2)
    counts = counts[0, ROUTE_OFF:ROUTE_OFF + N_EXPERTS]
    padded = (counts + MOE_ROWS - 1) // MOE_ROWS * MOE_ROWS
    p_end = jnp.cumsum(padded)
    p_start = p_end - padded
    dest = (p_start[route_i[:, 0:TOP_K]] + route_i[:, TOP_K:2 * TOP_K]).astype(jnp.int32)
    nblk = -(-A // MOE_ROWS) + N_EXPERTS
    blk_exp = jnp.minimum(jnp.searchsorted(p_end, jnp.arange(nblk, dtype=jnp.int32) * MOE_ROWS, side='right'),
                          N_EXPERTS - 1).astype(jnp.int32)
    n_used = (p_end[-1:] // MOE_ROWS).astype(jnp.int32)
    xs = moe_dispatch(h, dest, nblk * MOE_ROWS)
    eo = moe_experts(xs, blk_exp, n_used, w_gate.astype(BF16), w_up.astype(BF16), w_down.astype(BF16))
    return moe_combine(x2d, route_g, dest, eo).reshape(B, S, D)


def _trunk(x, p):
    x = recurrent_layer(x, p["norm_mix"][0], p["rec_w_in"][0], p["rg_conv_w"][0], p["rg_conv_b"][0],
                        p["rg_wa"][0], p["rg_ba"][0], p["rg_wx"][0], p["rg_bx"][0], p["rg_lambda"][0],
                        p["rw_mu_l"][0], p["rw_mu_r"][0], p["rw_w0"][0], p["rw_w_up"][0], p["rw_a0"][0],
                        p["rw_a_up"][0], p["rw_g_up"][0], p["rw_k_k"][0], p["rw_k_a"][0], p["rw_r_k"][0],
                        p["rw_ln_w"][0], p["rw_ln_b"][0], p["rec_w_out"][0])
    x = moe_layer(x, p["norm_ffn"][0], p["moe_wg1"][0], p["moe_bg1"][0], p["moe_wg2"][0], p["moe_bg2"][0],
                  p["moe_w_gate"][0], p["moe_w_up"][0], p["moe_w_down"][0])
    x = attention_layer(x, p["norm_mix"][1], p["att_w_in"][0], p["att_q_norm"][0], p["att_k_norm"][0],
                        p["att_sink"][0], p["att_w_out"][0])
    x = moe_layer(x, p["norm_ffn"][1], p["moe_wg1"][1], p["moe_bg1"][1], p["moe_wg2"][1], p["moe_bg2"][1],
                  p["moe_w_gate"][1], p["moe_w_up"][1], p["moe_w_down"][1])
    return x


def kernel(x_prompt, x_sample, norm_mix, norm_ffn, rec_w_in, rg_conv_w, rg_conv_b, rg_wa, rg_ba, rg_wx, rg_bx,
           rg_lambda, rw_mu_l, rw_mu_r, rw_w0, rw_w_up, rw_a0, rw_a_up, rw_g_up, rw_k_k, rw_k_a, rw_r_k,
           rw_ln_w, rw_ln_b, rec_w_out, att_w_in, att_q_norm, att_k_norm, att_sink, att_w_out, moe_wg1, moe_bg1,
           moe_wg2, moe_bg2, moe_w_gate, moe_w_up, moe_w_down):
    p = dict(norm_mix=norm_mix, norm_ffn=norm_ffn, rec_w_in=rec_w_in, rg_conv_w=rg_conv_w, rg_conv_b=rg_conv_b,
             rg_wa=rg_wa, rg_ba=rg_ba, rg_wx=rg_wx, rg_bx=rg_bx, rg_lambda=rg_lambda, rw_mu_l=rw_mu_l,
             rw_mu_r=rw_mu_r, rw_w0=rw_w0, rw_w_up=rw_w_up, rw_a0=rw_a0, rw_a_up=rw_a_up, rw_g_up=rw_g_up,
             rw_k_k=rw_k_k, rw_k_a=rw_k_a, rw_r_k=rw_r_k, rw_ln_w=rw_ln_w, rw_ln_b=rw_ln_b, rec_w_out=rec_w_out,
             att_w_in=att_w_in, att_q_norm=att_q_norm, att_k_norm=att_k_norm, att_sink=att_sink,
             att_w_out=att_w_out, moe_wg1=moe_wg1, moe_bg1=moe_bg1, moe_wg2=moe_wg2, moe_bg2=moe_bg2,
             moe_w_gate=moe_w_gate, moe_w_up=moe_w_up, moe_w_down=moe_w_down)
    return (_trunk(x_prompt, p), _trunk(x_sample, p))
```

```python
import functools
import math

import jax
import jax.numpy as jnp
from jax import lax
from jax.experimental import pallas as pl
from jax.experimental.pallas import tpu as pltpu

F32 = jnp.float32
BF16 = jnp.bfloat16

D_MODEL = 1024
RG_WIDTH = 512
RG_BLOCKS = 8
RG_BLOCK_DIM = 64
CONV_WIDTH = 4
RG_C = 8.0
RW_HEADS = 8
RW_HEAD_DIM = 64
RW_WIDTH = 512
DECAY_LORA = 32
ICL_LORA = 32
GATE_LORA = 64
RW_IN = 3 * RW_WIDTH + DECAY_LORA + ICL_LORA + GATE_LORA
REC_IN = 2 * RG_WIDTH + RW_IN
RW_GN_EPS = 64e-5
ATT_HEADS = 16
ATT_KV_HEADS = 4
ATT_GROUP = 4
ATT_HEAD_DIM = 64
WINDOW = 128
ATT_BLOCK = 128
ATT_IN = (ATT_HEADS + 2 * ATT_KV_HEADS) * ATT_HEAD_DIM
N_GROUPS = 4
EXPERTS_PER_GROUP = 8
N_EXPERTS = 32
TOP_K = 2
EXPERT_FF = 512
RMS_EPS = 1e-6
NEG_INF = -1e30

LANES = 128
SUBLANES = 8
VMEM_LIMIT_BYTES = 56 * 1024 * 1024
ROW_TILE = 512
WKV_TILE = 256
WKV_CHUNK = 64
MOE_ROWS = 256
ROUTE_LANES = 128
DMA_ROWS = 256


def _cparams(*sem):
    return pltpu.CompilerParams(dimension_semantics=sem, vmem_limit_bytes=VMEM_LIMIT_BYTES)


def _dot(a, b):
    return jnp.dot(a.astype(BF16), b.astype(BF16), preferred_element_type=F32)


def _dot_nt(a, b):
    return lax.dot_general(a.astype(BF16), b.astype(BF16), (((1,), (1,)), ((), ())),
                           preferred_element_type=F32)


def _split3(x):
    h1 = x.astype(BF16)
    r1 = x - h1.astype(F32)
    h2 = r1.astype(BF16)
    h3 = (r1 - h2.astype(F32)).astype(BF16)
    return h1, h2, h3


def _dot_exact_rhs(x, e):
    h1, h2, h3 = _split3(x)
    return (jnp.dot(h1, e, preferred_element_type=F32) + jnp.dot(h2, e, preferred_element_type=F32)
            + jnp.dot(h3, e, preferred_element_type=F32))


def _dot_exact_lhs(e, x):
    h1, h2, h3 = _split3(x)
    return (jnp.dot(e, h1, preferred_element_type=F32) + jnp.dot(e, h2, preferred_element_type=F32)
            + jnp.dot(e, h3, preferred_element_type=F32))


def _sigmoid(x):
    return 1.0 / (1.0 + jnp.exp(-x))


def _softplus(x):
    return jnp.maximum(x, 0.0) + jnp.log(1.0 + jnp.exp(-jnp.abs(x)))


def _gelu_tanh(x):
    return 0.5 * x * (1.0 + jnp.tanh(math.sqrt(2.0 / math.pi) * (x + 0.044715 * (x * x * x))))


def _head_ones(width, head):
    r = lax.broadcasted_iota(jnp.int32, (width, width), 0) // head
    c = lax.broadcasted_iota(jnp.int32, (width, width), 1) // head
    return jnp.where(r == c, 1.0, 0.0).astype(BF16)


def _norm_proj_kernel(x_ref, g_ref, w_ref, *out_refs, splits):
    x = x_ref[...]
    h = x * lax.rsqrt(jnp.mean(x * x, -1, keepdims=True) + RMS_EPS) * g_ref[...]
    hb = h.astype(BF16)
    off = 0
    for o_ref, n in zip(out_refs, splits):
        o_ref[...] = jnp.dot(hb, w_ref[:, off:off + n], preferred_element_type=F32).astype(o_ref.dtype)
        off += n


def norm_proj(x2d, g, w, splits, out_dtype=F32):
    T, D = x2d.shape
    N = w.shape[1]
    assert sum(splits) == N and T % ROW_TILE == 0
    return pl.pallas_call(
        functools.partial(_norm_proj_kernel, splits=splits),
        grid=(T // ROW_TILE,),
        in_specs=[pl.BlockSpec((ROW_TILE, D), lambda i: (i, 0)),
                  pl.BlockSpec((1, D), lambda i: (0, 0)),
                  pl.BlockSpec((D, N), lambda i: (0, 0))],
        out_specs=[pl.BlockSpec((ROW_TILE, n), lambda i: (i, 0)) for n in splits],
        out_shape=[jax.ShapeDtypeStruct((T, n), out_dtype) for n in splits],
        compiler_params=_cparams("parallel"),
        name="norm_proj",
    )(x2d, g.reshape(1, D), w.astype(BF16))


RG_HALF = RG_WIDTH // 2
RG_ROWS = 256
RG_PAD = SUBLANES


def _rglru_kernel(x_ref, g_ref, cw_ref, cb_ref, wg_ref, bg_ref, sp_ref, o_ref,
                  xp_ref, af_ref, bf_ref, ab_ref, bb_ref):
    S = x_ref.shape[1]
    C = RG_HALF
    xp_ref[0:RG_PAD, :] = jnp.zeros((RG_PAD, C), F32)
    xp_ref[RG_PAD + S:RG_PAD + S + RG_PAD, :] = jnp.zeros((RG_PAD, C), F32)
    xp_ref[RG_PAD:RG_PAD + S, :] = x_ref[0]
    cw = cw_ref[...]
    left = CONV_WIDTH // 2
    for c in range(S // RG_ROWS):
        r0 = c * RG_ROWS
        xc = cb_ref[...] + cw[0:1] * xp_ref[RG_PAD + r0 - left:RG_PAD + r0 - left + RG_ROWS, :]
        for k in range(1, CONV_WIDTH):
            s0 = RG_PAD + r0 + k - left
            xc = xc + cw[k:k + 1] * xp_ref[s0:s0 + RG_ROWS, :]
        z = _dot(xc, wg_ref[0]) + bg_ref[0]
        for d, (a_ref, b_ref) in enumerate(((af_ref, bf_ref), (ab_ref, bb_ref))):
            r = _sigmoid(z[:, (2 * d) * C:(2 * d + 1) * C])
            i = _sigmoid(z[:, (2 * d + 1) * C:(2 * d + 2) * C])
            log_a = -RG_C * r * sp_ref[0][:, d * C:(d + 1) * C]
            a = jnp.exp(log_a)
            a_ref[r0:r0 + RG_ROWS, :] = a
            b_ref[r0:r0 + RG_ROWS, :] = jnp.sqrt(1.0 - jnp.exp(2.0 * log_a)) * (i * xc)

    def body(n, carry):
        hf, hb = carry
        r0 = pl.multiple_of(n * SUBLANES, SUBLANES)
        a8 = af_ref[pl.ds(r0, SUBLANES), :]
        b8 = bf_ref[pl.ds(r0, SUBLANES), :]
        rows = []
        for j in range(SUBLANES):
            hf = a8[j:j + 1] * hf + b8[j:j + 1]
            rows.append(hf)
        bf_ref[pl.ds(r0, SUBLANES), :] = jnp.concatenate(rows, 0)
        r1 = pl.multiple_of(S - SUBLANES - n * SUBLANES, SUBLANES)
        a8 = ab_ref[pl.ds(r1, SUBLANES), :]
        b8 = bb_ref[pl.ds(r1, SUBLANES), :]
        rows = []
        for j in range(SUBLANES - 1, -1, -1):
            hb = a8[j:j + 1] * hb + b8[j:j + 1]
            rows.append(hb)
        bb_ref[pl.ds(r1, SUBLANES), :] = jnp.concatenate(rows[::-1], 0)
        return hf, hb

    zero = jnp.zeros((1, C), F32)
    lax.fori_loop(0, S // SUBLANES, body, (zero, zero))
    for c in range(S // RG_ROWS):
        sl = slice(c * RG_ROWS, (c + 1) * RG_ROWS)
        o_ref[0, sl, :] = ((bf_ref[sl, :] + bb_ref[sl, :]) * _gelu_tanh(g_ref[0, sl, :])).astype(o_ref.dtype)


def rglru_branch(rg_x, rg_g, conv_w, conv_b, wa, ba, wx, bx, lam):
    B, S, _ = rg_x.shape
    C = RG_HALF
    nb = C // RG_BLOCK_DIM

    def bdiag(w):
        w = w.reshape(2, nb, RG_BLOCK_DIM, RG_BLOCK_DIM)
        eye = jnp.eye(nb, dtype=w.dtype)
        return jnp.einsum('hnij,nm->hnimj', w, eye).reshape(2, C, C)

    wg = jnp.concatenate([bdiag(wa[0]), bdiag(wx[0]), bdiag(wa[1]), bdiag(wx[1])], axis=-1).astype(BF16)

    def halves(v):
        return v.reshape(2, 1, C)

    bg = jnp.concatenate([halves(ba[0]), halves(bx[0]), halves(ba[1]), halves(bx[1])], axis=-1)
    sp = jax.nn.softplus(-lam.astype(F32))
    spg = jnp.concatenate([halves(sp[0]), halves(sp[1])], axis=-1)
    return pl.pallas_call(
        _rglru_kernel,
        grid=(B, 2),
        in_specs=[pl.BlockSpec((1, S, C), lambda b, c: (b, 0, c)),
                  pl.BlockSpec((1, S, C), lambda b, c: (b, 0, c)),
                  pl.BlockSpec((CONV_WIDTH, C), lambda b, c: (0, c)),
                  pl.BlockSpec((1, C), lambda b, c: (0, c)),
                  pl.BlockSpec((1, C, 4 * C), lambda b, c: (c, 0, 0)),
                  pl.BlockSpec((1, 1, 4 * C), lambda b, c: (c, 0, 0)),
                  pl.BlockSpec((1, 1, 2 * C), lambda b, c: (c, 0, 0))],
        out_specs=pl.BlockSpec((1, S, C), lambda b, c: (b, 0, c)),
        out_shape=jax.ShapeDtypeStruct((B, S, RG_WIDTH), BF16),
        scratch_shapes=[pltpu.VMEM((S + 2 * RG_PAD, C), F32)] + [pltpu.VMEM((S, C), F32)] * 4,
        compiler_params=_cparams("parallel", "parallel"),
        name="rglru",
    )(rg_x, rg_g, conv_w, conv_b.reshape(1, RG_WIDTH), wg, bg, spg)


def _rwkv_prep_kernel(u_ref, up_ref, un_ref, mul_ref, mur_ref, wl_ref, w0_ref, a0_ref, kk_ref, ka_ref,
                      rk_ref, rt_ref, kt_ref, bt_ref, kh_ref, v_ref, le_ref, bonus_ref, g_ref):
    i = pl.program_id(1)
    nt = pl.num_programs(1)
    TS = u_ref.shape[1]
    W = RW_WIDTH
    u = u_ref[0]
    prow = jnp.where(i == 0, 0.0, up_ref[0][SUBLANES - 1:SUBLANES, :])
    nrow = jnp.where(i == nt - 1, 0.0, un_ref[0][0:1, :])
    rows = lax.broadcasted_iota(jnp.int32, (TS, 1), 0)
    prev = jnp.where(rows == 0, prow, pltpu.roll(u, 1, 0))
    nxt = jnp.where(rows == TS - 1, nrow, pltpu.roll(u, TS - 1, 0))
    m = u + mul_ref[...] * (prev - u) + mur_ref[...] * (nxt - u)
    r = m[:, 0:W]
    k = m[:, W:2 * W]
    v = m[:, 2 * W:3 * W]
    tail = m[:, 3 * W:3 * W + LANES]
    lane = lax.broadcasted_iota(jnp.int32, (1, LANES), 1)
    z = jnp.where(lane < DECAY_LORA, jnp.tanh(tail),
                  jnp.where(lane < DECAY_LORA + ICL_LORA, tail, _sigmoid(tail)))
    lo = _dot(z, wl_ref[...])
    ones = _head_ones(W, RW_HEAD_DIM)
    kkr = k * kk_ref[...]
    kk = kkr / jnp.maximum(jnp.sqrt(_dot_exact_rhs(kkr * kkr, ones)), 1e-12)
    tr = lax.broadcasted_iota(jnp.int32, (TS, TS), 0)
    tc = lax.broadcasted_iota(jnp.int32, (TS, TS), 1)
    same = (tr // WKV_CHUNK) == (tc // WKV_CHUNK)
    kd_sum = jnp.zeros((TS, W), F32)
    nch = TS // WKV_CHUNK
    for d in range(2):
        wlog = -_softplus(-(w0_ref[d:d + 1, :] + lo[:, d * W:(d + 1) * W])) - 0.5
        ld = -jnp.exp(wlog)
        a = _sigmoid(a0_ref[d:d + 1, :] + lo[:, (2 + d) * W:(3 + d) * W])
        kd = k * (1.0 + (a - 1.0) * ka_ref[...])
        kd_sum = kd_sum + kd
        tri = jnp.where(same & ((tc <= tr) if d == 0 else (tc >= tr)), 1.0, 0.0).astype(BF16)
        L = _dot_exact_lhs(tri, ld)
        en = jnp.exp(-L)
        rt_ref[d, 0] = (r * jnp.exp(L)).astype(BF16)
        kt_ref[d, 0] = (kk * jnp.exp(L - ld)).astype(BF16)
        bt_ref[d, 0] = (kk * a * en).astype(BF16)
        kh_ref[d, 0] = (kd * en).astype(BF16)
        ends = [L[(c + 1) * WKV_CHUNK - 1:(c + 1) * WKV_CHUNK] if d == 0 else L[c * WKV_CHUNK:c * WKV_CHUNK + 1]
                for c in range(nch)]
        le_ref[d, 0, 0] = jnp.concatenate(ends, 0)
    v_ref[0] = v.astype(BF16)
    bonus_ref[0] =_dot_exact_rhs(r * kd_sum * rk_ref[...], ones) * v
    g_ref[0] = lo[:, 4 * W:5 * W]


def rwkv_prep(rw, mu_l, mu_r, w0, w_up, a0, a_up, g_up, k_k, k_a, r_k):
    B, S, _ = rw.shape
    TS = WKV_TILE
    W = RW_WIDTH
    nt = S // TS
    nch = TS // WKV_CHUNK
    hb = TS // SUBLANES
    wl = jnp.zeros((LANES, 5 * W), F32)
    wl = wl.at[0:DECAY_LORA, 0:W].set(w_up[0]).at[0:DECAY_LORA, W:2 * W].set(w_up[1])
    o = DECAY_LORA
    wl = wl.at[o:o + ICL_LORA, 2 * W:3 * W].set(a_up[0]).at[o:o + ICL_LORA, 3 * W:4 * W].set(a_up[1])
    o += ICL_LORA
    wl = wl.at[o:o + GATE_LORA, 4 * W:5 * W].set(g_up)
    vec = lambda n: pl.BlockSpec((1, n), lambda b, i: (0, 0))
    big = lambda: pl.BlockSpec((2, 1, TS, W), lambda b, i: (0, b, i, 0))
    one = lambda: pl.BlockSpec((1, TS, W), lambda b, i: (b, i, 0))
    return pl.pallas_call(
        _rwkv_prep_kernel,
        grid=(B, nt),
        in_specs=[pl.BlockSpec((1, TS, RW_IN), lambda b, i: (b, i, 0)),
                  pl.BlockSpec((1, SUBLANES, RW_IN), lambda b, i: (b, jnp.maximum(i * hb - 1, 0), 0)),
                  pl.BlockSpec((1, SUBLANES, RW_IN), lambda b, i: (b, jnp.minimum((i + 1) * hb, S // SUBLANES - 1), 0)),
                  vec(RW_IN), vec(RW_IN),
                  pl.BlockSpec((LANES, 5 * W), lambda b, i: (0, 0)),
                  pl.BlockSpec((2, W), lambda b, i: (0, 0)),
                  pl.BlockSpec((2, W), lambda b, i: (0, 0)),
                  vec(W), vec(W), vec(W)],
        out_specs=[big(), big(), big(), big(), one(),
                   pl.BlockSpec((2, 1, 1, nch, W), lambda b, i: (0, b, i, 0, 0)),
                   one(), one()],
        out_shape=[jax.ShapeDtypeStruct((2, B, S, W), BF16)] * 4
        + [jax.ShapeDtypeStruct((B, S, W), BF16),
           jax.ShapeDtypeStruct((2, B, nt, nch, W), F32),
           jax.ShapeDtypeStruct((B, S, W), F32),
           jax.ShapeDtypeStruct((B, S, W), F32)],
        compiler_params=_cparams("parallel", "parallel"),
        name="rwkv_prep",
    )(rw, rw, rw, mu_l.reshape(1, RW_IN), mu_r.reshape(1, RW_IN), wl.astype(BF16), w0, a0,
      k_k.reshape(1, W), k_a.reshape(1, W), r_k.reshape(1, W))


def _wkv_tiles(probs):
    TS = WKV_TILE
    C = WKV_CHUNK
    N = RW_HEAD_DIM
    P = 2 * N
    nch = TS // C
    zero = jnp.zeros((), BF16)
    head0 = lax.broadcasted_iota(jnp.int32, (1, P), 1) < N
    head0_2 = (lax.broadcasted_iota(jnp.int32, (1, 2 * P), 1) % P) < N
    head0_w = (lax.broadcasted_iota(jnp.int32, (1, nch * P), 1) % P) < N
    own = (lax.broadcasted_iota(jnp.int32, (TS, nch * P), 0) // C
           == lax.broadcasted_iota(jnp.int32, (TS, nch * P), 1) // P)
    tq = lax.broadcasted_iota(jnp.int32, (C, TS), 0)
    sq = lax.broadcasted_iota(jnp.int32, (C, TS), 1) % C
    blk = (lax.broadcasted_iota(jnp.int32, (TS, TS), 0) // C
           == lax.broadcasted_iota(jnp.int32, (TS, TS), 1) // C)
    hr = lax.broadcasted_iota(jnp.int32, (P, P), 0)
    hc = lax.broadcasted_iota(jnp.int32, (P, P), 1)
    bdiag = (hr // N) == (hc // N)
    heye = hr == hc

    def both_heads(xb, m):
        return jnp.concatenate([jnp.where(m, xb, zero), jnp.where(m, zero, xb)], 0)

    def wide(xb):
        return jnp.concatenate([xb[c * C:(c + 1) * C] for c in range(nch)], 1)

    def expand(xw):
        return jnp.where(blk, jnp.concatenate([xw] * nch, 0), zero)

    st = []
    for (rt, kt, bt, kh, v, le, H, reverse) in probs:
        ktb = kt.astype(BF16)
        vb = v.astype(BF16)
        ktw = wide(ktb)
        rtw = wide(rt.astype(BF16))
        lhs = jnp.concatenate([jnp.where(head0_w, ktw, zero), jnp.where(head0_w, rtw, zero),
                               jnp.where(head0_w, zero, ktw), jnp.where(head0_w, zero, rtw)], 0)
        rhs = jnp.concatenate([jnp.where(own, jnp.concatenate([bt.astype(BF16)] * nch, 1), zero),
                               jnp.where(own, jnp.concatenate([kh.astype(BF16)] * nch, 1), zero)], 0)
        gram = lax.dot_general(lhs, rhs, (((1,), (1,)), ((), ())), preferred_element_type=F32)
        st.append(dict(ktb=ktb, vb=vb, gram=gram, vm=both_heads(vb, head0)))

    chains = []
    for s, prob in zip(st, probs):
        reverse = prob[7]
        strict = (sq > tq) if reverse else (sq < tq)
        incl = (sq >= tq) if reverse else (sq <= tq)
        s["b_bd"], s["rb_bd"], s["rk_bd"] = [], [], []
        for h in range(2):
            g0 = s["gram"][2 * h * C:(2 * h + 1) * C]
            g1 = s["gram"][(2 * h + 1) * C:(2 * h + 2) * C]
            A = jnp.where(strict, g0[:, 0:TS], 0.0)
            s["b_bd"].append(expand(jnp.where(strict, g0[:, TS:2 * TS], 0.0).astype(BF16)))
            s["rb_bd"].append(expand(jnp.where(incl, g1[:, 0:TS], 0.0).astype(BF16)))
            s["rk_bd"].append(expand(jnp.where(incl, g1[:, TS:2 * TS], 0.0).astype(BF16)))
            chains.append(dict(Tw=jnp.where(sq == tq, 1.0, 0.0) - A, Ab=A.astype(BF16)))
    for ch in chains:
        ch["Q"] = jnp.dot(ch["Ab"], expand(ch["Ab"]), preferred_element_type=F32)
    for _ in range(int(math.log2(C)) - 2):
        for ch in chains:
            Qb = ch["Q"].astype(BF16)
            out = jnp.dot(jnp.concatenate([ch["Tw"].astype(BF16), Qb], 0), expand(Qb), preferred_element_type=F32)
            ch["Tw"] = ch["Tw"] + out[0:C]
            ch["Q"] = out[C:2 * C]
    for ch in chains:
        Tw = ch["Tw"] + jnp.dot(ch["Tw"].astype(BF16), expand(ch["Q"].astype(BF16)), preferred_element_type=F32)
        ch["t_bd"] = expand(Tw.astype(BF16))

    for n, s in enumerate(st):
        s["Bv"] = jnp.dot(jnp.concatenate(s["b_bd"], 1), s["vm"], preferred_element_type=F32)
    for n, s in enumerate(st):
        wm = both_heads(jnp.concatenate([s["ktb"], s["Bv"].astype(BF16)], 1), head0_2)
        t_bd = [chains[2 * n]["t_bd"], chains[2 * n + 1]["t_bd"]]
        s["x1"] = jnp.dot(jnp.concatenate(t_bd, 1), wm, preferred_element_type=F32)
    for s, prob in zip(st, probs):
        x1 = s["x1"]
        rhs2 = jnp.concatenate([both_heads(-x1.astype(BF16), head0_2),
                                jnp.concatenate([jnp.zeros((2 * TS, P), BF16), s["vm"]], 1)], 0)
        x2 = jnp.dot(jnp.concatenate(s["rb_bd"] + s["rk_bd"], 1), rhs2, preferred_element_type=F32)
        s["K2"] = x1[:, 0:P]
        s["V2"] = x1[:, P:2 * P]
        s["R2"] = prob[0] + x2[:, 0:P]
        s["Y2"] = x2[:, P:2 * P]
        s["H"] = prob[6]
        s["pc"] = jnp.exp(prob[5])
        s["ys"] = [None] * nch

    for ci in range(nch):
        for s, prob in zip(st, probs):
            (rt, kt, bt, kh, v, le, _, reverse) = prob
            c = nch - 1 - ci if reverse else ci
            sl = slice(c * C, (c + 1) * C)
            pc = s["pc"][c:c + 1]
            bh = (bt[sl] * pc).T
            khh = (kh[sl] * pc).T
            Mc = jnp.where(heye, pc, 0.0) - jnp.where(bdiag, _dot(bh, s["K2"][sl]), 0.0)
            Gc = jnp.where(bdiag, _dot(jnp.concatenate([khh, -bh], 1),
                                       jnp.concatenate([v[sl], s["V2"][sl]], 0)), 0.0)
            s["ys"][c] = _dot(s["R2"][sl], s["H"]) + s["Y2"][sl]
            s["H"] = _dot(Mc, s["H"]) + Gc
    return [(jnp.concatenate(s["ys"], 0), s["H"]) for s in st]


WKV_PAIRS = 2


def _wkv_kernel(rtf_ref, ktf_ref, btf_ref, khf_ref, vf_ref, lef_ref,
                rtb_ref, ktb_ref, btb_ref, khb_ref, vb_ref, leb_ref, yf_ref, yb_ref, h_ref):
    i = pl.program_id(2)

    @pl.when(i == 0)
    def _():
        h_ref[...] = jnp.zeros_like(h_ref)

    P = 2 * RW_HEAD_DIM
    dirs = ((rtf_ref, ktf_ref, btf_ref, khf_ref, vf_ref, lef_ref, yf_ref),
            (rtb_ref, ktb_ref, btb_ref, khb_ref, vb_ref, leb_ref, yb_ref))
    probs, outs = [], []
    for d, (rt_ref, kt_ref, bt_ref, kh_ref, v_ref, le_ref, y_ref) in enumerate(dirs):
        for pr in range(WKV_PAIRS):
            ln = slice(pr * P, (pr + 1) * P)
            probs.append((rt_ref[0, 0, :, ln], kt_ref[0, 0, :, ln], bt_ref[0, 0, :, ln], kh_ref[0, 0, :, ln],
                          v_ref[0, :, ln], le_ref[0, 0, 0, :, ln], h_ref[d, pr], d == 1))
            outs.append((y_ref, d, pr, ln))
    for (y, H), (y_ref, d, pr, ln) in zip(_wkv_tiles(probs), outs):
        y_ref[0, :, ln] = y
        h_ref[d, pr] = H


def wkv_scan(rt, kt, bt, kh, v, le):
    _, B, S, W = rt.shape
    TS = WKV_TILE
    nt = S // TS
    nch = TS // WKV_CHUNK
    P = 2 * RW_HEAD_DIM
    PW = WKV_PAIRS * P
    tiles = (lambda i: i, lambda i: nt - 1 - i)

    def dir_specs(d):
        t = tiles[d]
        big = lambda: pl.BlockSpec((1, 1, TS, PW), lambda b, p, i: (d, b, t(i), p))
        return [big(), big(), big(), big(),
                pl.BlockSpec((1, TS, PW), lambda b, p, i: (b, t(i), p)),
                pl.BlockSpec((1, 1, 1, nch, PW), lambda b, p, i: (d, b, t(i), 0, p))]

    return pl.pallas_call(
        _wkv_kernel,
        grid=(B, W // PW, nt),
        in_specs=dir_specs(0) + dir_specs(1),
        out_specs=[pl.BlockSpec((1, TS, PW), lambda b, p, i: (b, tiles[0](i), p)),
                   pl.BlockSpec((1, TS, PW), lambda b, p, i: (b, tiles[1](i), p))],
        out_shape=[jax.ShapeDtypeStruct((B, S, W), F32)] * 2,
        scratch_shapes=[pltpu.VMEM((2, WKV_PAIRS, P, P), F32)],
        compiler_params=_cparams("parallel", "parallel", "arbitrary"),
        name="wkv",
    )(rt, kt, bt, kh, v, le, rt, kt, bt, kh, v, le)


def _rec_out_kernel(x_ref, rg_ref, yf_ref, yb_ref, bonus_ref, g_ref, lnw_ref, lnb_ref, w_ref, o_ref):
    W = RW_WIDTH
    ones = _head_ones(W, RW_HEAD_DIM)
    y = yf_ref[...] + yb_ref[...]
    inv_n = 1.0 / RW_HEAD_DIM
    mu = _dot_exact_rhs(y, ones) * inv_n
    yc = y - mu
    var = _dot_exact_rhs(yc * yc, ones) * inv_n
    yn = yc * lax.rsqrt(var + RW_GN_EPS) * lnw_ref[...] + lnb_ref[...]
    rw_out = (yn + bonus_ref[...]) * g_ref[...]
    mix = (jnp.dot(rg_ref[...], w_ref[0:RG_WIDTH, :], preferred_element_type=F32)
           + jnp.dot(rw_out.astype(BF16), w_ref[RG_WIDTH:RG_WIDTH + W, :], preferred_element_type=F32))
    o_ref[...] = x_ref[...] + mix


def rec_out(x2d, rg_out, y_f, y_b, bonus, g, ln_w, ln_b, w_out):
    T, D = x2d.shape
    W = RW_WIDTH
    row = lambda n: pl.BlockSpec((ROW_TILE, n), lambda i: (i, 0))
    vec = lambda n: pl.BlockSpec((1, n), lambda i: (0, 0))
    return pl.pallas_call(
        _rec_out_kernel,
        grid=(T // ROW_TILE,),
        in_specs=[row(D), row(RG_WIDTH), row(W), row(W), row(W), row(W), vec(W), vec(W),
                  pl.BlockSpec((RG_WIDTH + W, D), lambda i: (0, 0))],
        out_specs=row(D),
        out_shape=jax.ShapeDtypeStruct((T, D), F32),
        compiler_params=_cparams("parallel"),
        name="rec_out",
    )(x2d, rg_out, y_f, y_b, bonus, g, ln_w.reshape(1, W), ln_b.reshape(1, W), w_out.astype(BF16))


def recurrent_layer(x, norm_g, w_in, conv_w, conv_b, rg_wa, rg_ba, rg_wx, rg_bx, rg_lambda,
                    mu_l, mu_r, w0, w_up, a0, a_up, g_up, k_k, k_a, r_k, ln_w, ln_b, w_out):
    B, S, D = x.shape
    T = B * S
    x2d = x.reshape(T, D)
    rg_x, rg_g, rw = norm_proj(x2d, norm_g, w_in, (RG_WIDTH, RG_WIDTH, RW_IN))
    rg_out = rglru_branch(rg_x.reshape(B, S, RG_WIDTH), rg_g.reshape(B, S, RG_WIDTH), conv_w, conv_b,
                          rg_wa, rg_ba, rg_wx, rg_bx, rg_lambda)
    rt, kt, bt, kh, v, le, bonus, g = rwkv_prep(rw.reshape(B, S, RW_IN), mu_l, mu_r, w0, w_up, a0, a_up,
                                                g_up, k_k, k_a, r_k.reshape(-1))
    y_f, y_b = wkv_scan(rt, kt, bt, kh, v, le)
    flat = lambda t: t.reshape(T, -1)
    out = rec_out(x2d, flat(rg_out), flat(y_f), flat(y_b), flat(bonus), flat(g), ln_w, ln_b, w_out)
    return out.reshape(B, S, D)


def _alibi_slope(h):
    return 2.0 ** (-8.0 * (h + 1) / ATT_HEADS)


ATT_PAIR = 2 * ATT_HEAD_DIM
ATT_KVW = ATT_KV_HEADS * ATT_PAIR


def _qkv_proj_kernel(x_ref, g_ref, w_ref, eq_ref, ek_ref, qg_ref, kg_ref, q_ref, k_ref, v_ref):
    QW = ATT_HEADS * ATT_HEAD_DIM
    x = x_ref[...]
    h = (x * lax.rsqrt(jnp.mean(x * x, -1, keepdims=True) + RMS_EPS) * g_ref[...]).astype(BF16)
    q = jnp.dot(h, w_ref[:, 0:QW], preferred_element_type=F32)
    msq = jnp.dot((q * q).astype(BF16), eq_ref[...], preferred_element_type=F32) * (1.0 / ATT_HEAD_DIM)
    q_ref[...] = (q * lax.rsqrt(msq + RMS_EPS) * qg_ref[...]).astype(BF16)
    k = jnp.dot(h, w_ref[:, QW:QW + ATT_KVW], preferred_element_type=F32)
    msk = jnp.dot((k * k).astype(BF16), ek_ref[...], preferred_element_type=F32) * (1.0 / ATT_PAIR)
    k_ref[...] = (k * lax.rsqrt(msk + RMS_EPS) * kg_ref[...]).astype(BF16)
    v_ref[...] = jnp.dot(h, w_ref[:, QW + ATT_KVW:QW + 2 * ATT_KVW], preferred_element_type=F32).astype(BF16)


def qkv_proj(x2d, g, w_in, q_norm, k_norm):
    T, D = x2d.shape
    QW = ATT_HEADS * ATT_HEAD_DIM
    KW = ATT_KV_HEADS * ATT_HEAD_DIM
    dup = lambda w: jnp.concatenate([w.reshape(D, ATT_KV_HEADS, 1, ATT_HEAD_DIM)] * 2, 2).reshape(D, ATT_KVW)
    w = jnp.concatenate([w_in[:, :QW], dup(w_in[:, QW:QW + KW]), dup(w_in[:, QW + KW:])], 1).astype(BF16)
    N = QW + 2 * ATT_KVW
    qg = jnp.tile(q_norm.astype(F32), ATT_HEADS).reshape(1, QW) * (ATT_HEAD_DIM ** -0.5)
    kg = jnp.tile(k_norm.astype(F32), 2 * ATT_KV_HEADS).reshape(1, ATT_KVW)
    row = lambda n: pl.BlockSpec((ROW_TILE, n), lambda i: (i, 0))
    fixed = lambda r, n: pl.BlockSpec((r, n), lambda i: (0, 0))
    return pl.pallas_call(
        _qkv_proj_kernel,
        grid=(T // ROW_TILE,),
        in_specs=[row(D), fixed(1, D), fixed(D, N), fixed(QW, QW), fixed(ATT_KVW, ATT_KVW),
                  fixed(1, QW), fixed(1, ATT_KVW)],
        out_specs=[row(QW), row(ATT_KVW), row(ATT_KVW)],
        out_shape=[jax.ShapeDtypeStruct((T, QW), BF16), jax.ShapeDtypeStruct((T, ATT_KVW), BF16),
                   jax.ShapeDtypeStruct((T, ATT_KVW), BF16)],
        compiler_params=_cparams("parallel"),
        name="qkv_proj",
    )(x2d, g.reshape(1, D), w, _head_ones(QW, ATT_HEAD_DIM), _head_ones(ATT_KVW, ATT_PAIR), qg, kg)


def _attn_kernel(sink_ref, x_ref, q_ref, kp_ref, kc_ref, kn_ref, vp_ref, vc_ref, vn_ref, bias_ref,
                 w_ref, o_ref):
    i = pl.program_id(1)
    nb = pl.num_programs(1)
    BLK = ATT_BLOCK
    P = ATT_PAIR
    span = 3 * BLK
    kc = jnp.concatenate([kp_ref[0], kc_ref[0], kn_ref[0]], 0)
    vc = jnp.concatenate([vp_ref[0], vc_ref[0], vn_ref[0]], 0)
    kpos = lax.broadcasted_iota(jnp.int32, (1, span), 1) + (i - 1) * BLK
    edge = jnp.where((kpos >= 0) & (kpos < nb * BLK), 0.0, NEG_INF)
    lane = lax.broadcasted_iota(jnp.int32, (1, P), 1)
    low = lane < ATT_HEAD_DIM
    zero = jnp.zeros((), BF16)
    ones = jnp.ones((span, P), BF16)
    slabs = []
    for g in range(ATT_KV_HEADS):
        kg = kc[:, g * P:(g + 1) * P]
        k_half = (jnp.where(low, kg, zero), jnp.where(low, zero, kg))
        v_ext = jnp.concatenate([vc[:, g * P:(g + 1) * P], ones], 1)
        for pr in range(ATT_GROUP // 2):
            slab = g * (ATT_GROUP // 2) + pr
            qp = q_ref[0, :, slab * P:(slab + 1) * P]
            halves = []
            for hf in range(2):
                h = 2 * slab + hf
                s = lax.dot_general(qp, k_half[hf], (((1,), (1,)), ((), ())), preferred_element_type=F32)
                s = s + bias_ref[h] + edge
                sk = sink_ref[h]
                m = jnp.maximum(jnp.max(s, -1, keepdims=True), sk)
                p = jnp.exp(s - m).astype(BF16)
                o = jnp.dot(p, v_ext, preferred_element_type=F32)
                halves.append(o[:, 0:P] / (o[:, P:2 * P] + jnp.exp(sk - m)))
            slabs.append(jnp.where(low, halves[0], halves[1]))
    o = jnp.concatenate(slabs, -1).astype(BF16)
    o_ref[0] = x_ref[0] + jnp.dot(o, w_ref[...], preferred_element_type=F32)


def attention_layer(x, norm_g, w_in, q_norm, k_norm, sink, w_out):
    B, S, D = x.shape
    T = B * S
    QW = ATT_HEADS * ATT_HEAD_DIM
    q, k, v = qkv_proj(x.reshape(T, D), norm_g, w_in, q_norm, k_norm)
    q = q.reshape(B, S, QW)
    k = k.reshape(B, S, ATT_KVW)
    v = v.reshape(B, S, ATT_KVW)
    nb = S // ATT_BLOCK
    span = 3 * ATT_BLOCK
    rel = (jnp.arange(span)[None, :] - WINDOW) - jnp.arange(ATT_BLOCK)[:, None]
    slopes = jnp.asarray([_alibi_slope(h) for h in range(ATT_HEADS)], F32)
    bias = jnp.where(jnp.abs(rel) <= WINDOW, -slopes[:, None, None] * jnp.abs(rel).astype(F32), NEG_INF)
    prev = lambda b, i: (b, jnp.maximum(i - 1, 0), 0)
    cur = lambda b, i: (b, i, 0)
    nxt = lambda b, i: (b, jnp.minimum(i + 1, nb - 1), 0)
    kv = lambda f: pl.BlockSpec((1, ATT_BLOCK, ATT_KVW), f)
    return pl.pallas_call(
        _attn_kernel,
        grid=(B, nb),
        in_specs=[pl.BlockSpec(memory_space=pltpu.SMEM),
                  pl.BlockSpec((1, ATT_BLOCK, D), cur),
                  pl.BlockSpec((1, ATT_BLOCK, QW), cur),
                  kv(prev), kv(cur), kv(nxt), kv(prev), kv(cur), kv(nxt),
                  pl.BlockSpec((ATT_HEADS, ATT_BLOCK, span), lambda b, i: (0, 0, 0)),
                  pl.BlockSpec((QW, D), lambda b, i: (0, 0))],
        out_specs=pl.BlockSpec((1, ATT_BLOCK, D), cur),
        out_shape=jax.ShapeDtypeStruct((B, S, D), F32),
        compiler_params=_cparams("parallel", "parallel"),
        name="window_attn",
    )(sink.astype(F32), x, q, k, k, k, v, v, v, bias, w_out.astype(BF16))


ROUTE_OFF = N_GROUPS


def _router_kernel(x_ref, g_ref, w1_ref, w2_ref, b_ref, h_ref, ri_ref, rg_ref, cnt_ref, run_ref):
    i = pl.program_id(0)

    @pl.when(i == 0)
    def _():
        run_ref[...] = jnp.zeros_like(run_ref)

    TM = x_ref.shape[0]
    x = x_ref[...]
    h = x * lax.rsqrt(jnp.mean(x * x, -1, keepdims=True) + RMS_EPS) * g_ref[...]
    h_ref[...] = h
    h1 = h.astype(BF16)
    h2 = (h - h1.astype(F32)).astype(BF16)
    lg = (jnp.dot(h1, w1_ref[...], preferred_element_type=F32) + jnp.dot(h1, w2_ref[...], preferred_element_type=F32)
          + jnp.dot(h2, w1_ref[...], preferred_element_type=F32)) + b_ref[...]
    lane = lax.broadcasted_iota(jnp.int32, (TM, ROUTE_LANES), 1)
    far = ROUTE_LANES
    gmask = lane < N_GROUPS
    gl = jnp.where(gmask, lg, NEG_INF)
    gm = jnp.max(gl, -1, keepdims=True)
    p_group = 1.0 / jnp.sum(jnp.where(gmask, jnp.exp(gl - gm), 0.0), -1, keepdims=True)
    group = jnp.min(jnp.where(gl == gm, lane, far), -1, keepdims=True)
    fmask = (lane >= ROUTE_OFF) & (lane < ROUTE_OFF + N_EXPERTS) & ((lane - ROUTE_OFF) // EXPERTS_PER_GROUP == group)
    fl = jnp.where(fmask, lg, NEG_INF)
    m1 = jnp.max(fl, -1, keepdims=True)
    ssum = jnp.sum(jnp.where(fmask, jnp.exp(fl - m1), 0.0), -1, keepdims=True)
    i1 = jnp.min(jnp.where(fl == m1, lane, far), -1, keepdims=True)
    fl2 = jnp.where(lane == i1, NEG_INF, fl)
    m2 = jnp.max(fl2, -1, keepdims=True)
    i2 = jnp.min(jnp.where(fl2 == m2, lane, far), -1, keepdims=True)
    p1 = 1.0 / ssum
    p2 = jnp.exp(m2 - m1) / ssum
    norm = p_group / (p1 + p2)
    oh = jnp.where((lane == i1) | (lane == i2), 1.0, 0.0)
    tr = lax.broadcasted_iota(jnp.int32, (TM, TM), 0)
    tc = lax.broadcasted_iota(jnp.int32, (TM, TM), 1)
    before = jnp.where(tc < tr, 1.0, 0.0).astype(BF16)
    pre = jnp.dot(before, oh.astype(BF16), preferred_element_type=F32) + run_ref[...]
    rank1 = jnp.sum(jnp.where(lane == i1, pre, 0.0), -1, keepdims=True)
    rank2 = jnp.sum(jnp.where(lane == i2, pre, 0.0), -1, keepdims=True)
    total = run_ref[...] + jnp.sum(oh, 0, keepdims=True)
    run_ref[...] = total
    cnt_ref[...] = total.astype(jnp.int32)
    ri_ref[...] = jnp.where(lane == 0, i1 - ROUTE_OFF,
                            jnp.where(lane == 1, i2 - ROUTE_OFF,
                                      jnp.where(lane == 2, rank1.astype(jnp.int32),
                                                jnp.where(lane == 3, rank2.astype(jnp.int32), 0))))
    rg_ref[...] = jnp.where(lane == 0, p1 * norm, jnp.where(lane == 1, p2 * norm, 0.0))


def moe_router(x2d, g, wg1, bg1, wg2, bg2):
    T, D = x2d.shape
    wr = jnp.zeros((D, ROUTE_LANES), F32)
    wr = wr.at[:, 0:N_GROUPS].set(wg1)
    wr = wr.at[:, ROUTE_OFF:ROUTE_OFF + N_EXPERTS].set(jnp.moveaxis(wg2, 0, 1).reshape(D, N_EXPERTS))
    w1 = wr.astype(BF16)
    w2 = (wr - w1.astype(F32)).astype(BF16)
    bias = jnp.zeros((1, ROUTE_LANES), F32)
    bias = bias.at[0, 0:N_GROUPS].set(bg1).at[0, ROUTE_OFF:ROUTE_OFF + N_EXPERTS].set(bg2.reshape(-1))
    row = lambda n: pl.BlockSpec((ROW_TILE, n), lambda i: (i, 0))
    fixed = lambda r, n: pl.BlockSpec((r, n), lambda i: (0, 0))
    return pl.pallas_call(
        _router_kernel,
        grid=(T // ROW_TILE,),
        in_specs=[row(D), fixed(1, D), fixed(D, ROUTE_LANES), fixed(D, ROUTE_LANES), fixed(1, ROUTE_LANES)],
        out_specs=[row(D), row(ROUTE_LANES), row(ROUTE_LANES), fixed(1, ROUTE_LANES)],
        out_shape=[jax.ShapeDtypeStruct((T, D), F32),
                   jax.ShapeDtypeStruct((T, ROUTE_LANES), jnp.int32),
                   jax.ShapeDtypeStruct((T, ROUTE_LANES), F32),
                   jax.ShapeDtypeStruct((1, ROUTE_LANES), jnp.int32)],
        scratch_shapes=[pltpu.VMEM((1, ROUTE_LANES), F32)],
        compiler_params=_cparams("arbitrary"),
        name="moe_router",
    )(x2d, g.reshape(1, D), w1, w2, bias)


def _row_copy(src, src_row, dst, dst_row, sem):
    return pltpu.make_async_copy(src.at[pl.ds(src_row, 1)], dst.at[pl.ds(dst_row, 1)], sem)


DMA_UNROLL = 8


def _rows_wait(ref, nrows, sem):
    pltpu.make_async_copy(ref.at[pl.ds(0, nrows)], ref.at[pl.ds(0, nrows)], sem).wait()


DISPATCH_ROWS = 512


def _dispatch_kernel(pend_ref, dest_ref, h_ref, xs_ref, zero_ref, sem, zsem):
    i = pl.program_id(0)
    n = h_ref.shape[0]
    nblk = xs_ref.shape[0] // MOE_ROWS

    @pl.when(i == 0)
    def _():
        zero_ref[...] = jnp.zeros_like(zero_ref)

        def block_copy(row0):
            return pltpu.make_async_copy(zero_ref, xs_ref.at[pl.ds(pl.multiple_of(row0, MOE_ROWS), MOE_ROWS)], zsem)

        tails = [jnp.maximum(pend_ref[e] - MOE_ROWS, 0) for e in range(N_EXPERTS)]
        for t in tails:
            block_copy(t).start()
        for t in tails:
            block_copy(t).wait()
        first_unused = pend_ref[N_EXPERTS - 1] // MOE_ROWS

        def clear(j, c):
            block_copy(j * MOE_ROWS).start()
            block_copy(j * MOE_ROWS).wait()
            return c

        lax.fori_loop(first_unused, nblk, clear, 0)

    def issue(r, c):
        for k in range(TOP_K):
            _row_copy(h_ref, r, xs_ref, dest_ref[0, 0, TOP_K * r + k], sem).start()
        return c

    lax.fori_loop(0, n, issue, 0, unroll=DMA_UNROLL)
    _rows_wait(xs_ref, TOP_K * n, sem)


def moe_dispatch(h2d, dest, p_end, rows):
    T, D = h2d.shape
    nt = T // DISPATCH_ROWS
    return pl.pallas_call(
        _dispatch_kernel,
        grid_spec=pltpu.PrefetchScalarGridSpec(
            num_scalar_prefetch=1,
            grid=(nt,),
            in_specs=[pl.BlockSpec((1, 1, TOP_K * DISPATCH_ROWS), lambda i, pe: (i, 0, 0),
                                   memory_space=pltpu.SMEM),
                      pl.BlockSpec((DISPATCH_ROWS, D), lambda i, pe: (i, 0))],
            out_specs=pl.BlockSpec(memory_space=pl.ANY),
            scratch_shapes=[pltpu.VMEM((MOE_ROWS, D), F32), pltpu.SemaphoreType.DMA(()),
                            pltpu.SemaphoreType.DMA(())]),
        out_shape=jax.ShapeDtypeStruct((rows, D), F32),
        compiler_params=_cparams("arbitrary"),
        name="moe_dispatch",
    )(p_end, dest.reshape(nt, 1, TOP_K * DISPATCH_ROWS), h2d)


def _expert_kernel(be_ref, nu_ref, x_ref, wg_ref, wu_ref, wd_ref, o_ref):
    i = pl.program_id(0)

    @pl.when(i < nu_ref[0])
    def _():
        xb = x_ref[...].astype(BF16)
        hg = jnp.dot(xb, wg_ref[0], preferred_element_type=F32)
        hu = jnp.dot(xb, wu_ref[0], preferred_element_type=F32)
        hb = (hg * _sigmoid(hg) * hu).astype(BF16)
        o_ref[...] = jnp.dot(hb, wd_ref[0], preferred_element_type=F32)

    @pl.when(i >= nu_ref[0])
    def _():
        o_ref[...] = jnp.zeros_like(o_ref)


def moe_experts(xs, blk_exp, n_used, w_gate, w_up, w_down):
    rows, D = xs.shape
    nblk = rows // MOE_ROWS
    F = EXPERT_FF
    return pl.pallas_call(
        _expert_kernel,
        grid_spec=pltpu.PrefetchScalarGridSpec(
            num_scalar_prefetch=2,
            grid=(nblk,),
            in_specs=[pl.BlockSpec((MOE_ROWS, D), lambda i, be, nu: (jnp.minimum(i, nu[0] - 1), 0)),
                      pl.BlockSpec((1, D, F), lambda i, be, nu: (be[i], 0, 0)),
                      pl.BlockSpec((1, D, F), lambda i, be, nu: (be[i], 0, 0)),
                      pl.BlockSpec((1, F, D), lambda i, be, nu: (be[i], 0, 0))],
            out_specs=pl.BlockSpec((MOE_ROWS, D), lambda i, be, nu: (i, 0))),
        out_shape=jax.ShapeDtypeStruct((rows, D), F32),
        compiler_params=_cparams("arbitrary"),
        name="moe_experts",
    )(blk_exp, n_used, xs, w_gate, w_up, w_down)


def _combine_kernel(dest_ref, dnext_ref, x_ref, gate_ref, eo_ref, o_ref, buf_ref, sem):
    i = pl.program_id(0)
    nt = pl.num_programs(0)
    n = x_ref.shape[0]
    slot = i % 2

    def gather(d_ref, s):
        def issue(r, c):
            for k in range(TOP_K):
                _row_copy(eo_ref, d_ref[0, 0, TOP_K * r + k], buf_ref.at[s, k], r, sem.at[s]).start()
            return c

        lax.fori_loop(0, n, issue, 0, unroll=DMA_UNROLL)

    @pl.when(i == 0)
    def _():
        gather(dest_ref, 0)

    @pl.when(i + 1 < nt)
    def _():
        gather(dnext_ref, 1 - slot)

    pltpu.make_async_copy(buf_ref.at[slot], buf_ref.at[slot], sem.at[slot]).wait()
    gate = gate_ref[...]
    ffn = gate[:, 0:1] * buf_ref[slot, 0]
    for k in range(1, TOP_K):
        ffn = ffn + gate[:, k:k + 1] * buf_ref[slot, k]
    o_ref[...] = x_ref[...] + ffn


def moe_combine(x2d, gates, dest, eo):
    T, D = x2d.shape
    nt = T // DMA_ROWS
    dest3 = dest.reshape(nt, 1, TOP_K * DMA_ROWS)
    dspec = lambda f: pl.BlockSpec((1, 1, TOP_K * DMA_ROWS), f, memory_space=pltpu.SMEM)
    return pl.pallas_call(
        _combine_kernel,
        grid=(nt,),
        in_specs=[dspec(lambda i: (i, 0, 0)),
                  dspec(lambda i: (jnp.minimum(i + 1, nt - 1), 0, 0)),
                  pl.BlockSpec((DMA_ROWS, D), lambda i: (i, 0)),
                  pl.BlockSpec((DMA_ROWS, ROUTE_LANES), lambda i: (i, 0)),
                  pl.BlockSpec(memory_space=pl.ANY)],
        out_specs=pl.BlockSpec((DMA_ROWS, D), lambda i: (i, 0)),
        out_shape=jax.ShapeDtypeStruct((T, D), F32),
        scratch_shapes=[pltpu.VMEM((2, TOP_K, DMA_ROWS, D), F32), pltpu.SemaphoreType.DMA((2,))],
        compiler_params=_cparams("arbitrary"),
        name="moe_combine",
    )(dest3, dest3, x2d, gates, eo)


def moe_layer(x, norm_g, wg1, bg1, wg2, bg2, w_gate, w_up, w_down):
    B, S, D = x.shape
    T = B * S
    A = T * TOP_K
    x2d = x.reshape(T, D)
    h, route_i, route_g, counts = moe_router(x2d, norm_g, wg1, bg1, wg2, bg2)
    counts = counts[0, ROUTE_OFF:ROUTE_OFF + N_EXPERTS]
    padded = (counts + MOE_ROWS - 1) // MOE_ROWS * MOE_ROWS
    p_end = jnp.cumsum(padded)
    p_start = p_end - padded
    sel = route_i[:, 0:TOP_K, None] == jnp.arange(N_EXPERTS, dtype=jnp.int32)
    dest = (jnp.sum(jnp.where(sel, p_start, 0), -1) + route_i[:, TOP_K:2 * TOP_K]).astype(jnp.int32)
    nblk = -(-A // MOE_ROWS) + N_EXPERTS
    blk_row = jnp.arange(nblk, dtype=jnp.int32) * MOE_ROWS
    blk_exp = jnp.minimum(jnp.sum(p_end[None, :] <= blk_row[:, None], -1), N_EXPERTS - 1).astype(jnp.int32)
    n_used = (p_end[-1:] // MOE_ROWS).astype(jnp.int32)
    xs = moe_dispatch(h, dest, p_end.astype(jnp.int32), nblk * MOE_ROWS)
    eo = moe_experts(xs, blk_exp, n_used, w_gate.astype(BF16), w_up.astype(BF16), w_down.astype(BF16))
    return moe_combine(x2d, route_g, dest, eo).reshape(B, S, D)


def _trunk(x, p):
    x = recurrent_layer(x, p["norm_mix"][0], p["rec_w_in"][0], p["rg_conv_w"][0], p["rg_conv_b"][0],
                        p["rg_wa"][0], p["rg_ba"][0], p["rg_wx"][0], p["rg_bx"][0], p["rg_lambda"][0],
                        p["rw_mu_l"][0], p["rw_mu_r"][0], p["rw_w0"][0], p["rw_w_up"][0], p["rw_a0"][0],
                        p["rw_a_up"][0], p["rw_g_up"][0], p["rw_k_k"][0], p["rw_k_a"][0], p["rw_r_k"][0],
                        p["rw_ln_w"][0], p["rw_ln_b"][0], p["rec_w_out"][0])
    x = moe_layer(x, p["norm_ffn"][0], p["moe_wg1"][0], p["moe_bg1"][0], p["moe_wg2"][0], p["moe_bg2"][0],
                  p["moe_w_gate"][0], p["moe_w_up"][0], p["moe_w_down"][0])
    x = attention_layer(x, p["norm_mix"][1], p["att_w_in"][0], p["att_q_norm"][0], p["att_k_norm"][0],
                        p["att_sink"][0], p["att_w_out"][0])
    x = moe_layer(x, p["norm_ffn"][1], p["moe_wg1"][1], p["moe_bg1"][1], p["moe_wg2"][1], p["moe_bg2"][1],
                  p["moe_w_gate"][1], p["moe_w_up"][1], p["moe_w_down"][1])
    return x


def kernel(x_prompt, x_sample, norm_mix, norm_ffn, rec_w_in, rg_conv_w, rg_conv_b, rg_wa, rg_ba, rg_wx, rg_bx,
           rg_lambda, rw_mu_l, rw_mu_r, rw_w0, rw_w_up, rw_a0, rw_a_up, rw_g_up, rw_k_k, rw_k_a, rw_r_k,
           rw_ln_w, rw_ln_b, rec_w_out, att_w_in, att_q_norm, att_k_norm, att_sink, att_w_out, moe_wg1, moe_bg1,
           moe_wg2, moe_bg2, moe_w_gate, moe_w_up, moe_w_down):
    p = dict(norm_mix=norm_mix, norm_ffn=norm_ffn, rec_w_in=rec_w_in, rg_conv_w=rg_conv_w, rg_conv_b=rg_conv_b,
             rg_wa=rg_wa, rg_ba=rg_ba, rg_wx=rg_wx, rg_bx=rg_bx, rg_lambda=rg_lambda, rw_mu_l=rw_mu_l,
             rw_mu_r=rw_mu_r, rw_w0=rw_w0, rw_w_up=rw_w_up, rw_a0=rw_a0, rw_a_up=rw_a_up, rw_g_up=rw_g_up,
             rw_k_k=rw_k_k, rw_k_a=rw_k_a, rw_r_k=rw_r_k, rw_ln_w=rw_ln_w, rw_ln_b=rw_ln_b, rec_w_out=rec_w_out,
             att_w_in=att_w_in, att_q_norm=att_q_norm, att_k_norm=att_k_norm, att_sink=att_sink,
             att_w_out=att_w_out, moe_wg1=moe_wg1, moe_bg1=moe_bg1, moe_wg2=moe_wg2, moe_bg2=moe_bg2,
             moe_w_gate=moe_w_gate, moe_w_up=moe_w_up, moe_w_down=moe_w_down)
    return (_trunk(x_prompt, p), _trunk(x_sample, p))
```

```python
import functools
import math

import jax
import jax.numpy as jnp
from jax import lax
from jax.experimental import pallas as pl
from jax.experimental.pallas import tpu as pltpu

F32 = jnp.float32
BF16 = jnp.bfloat16

D_MODEL = 1024
RG_WIDTH = 512
RG_BLOCKS = 8
RG_BLOCK_DIM = 64
CONV_WIDTH = 4
RG_C = 8.0
RW_HEADS = 8
RW_HEAD_DIM = 64
RW_WIDTH = 512
DECAY_LORA = 32
ICL_LORA = 32
GATE_LORA = 64
RW_IN = 3 * RW_WIDTH + DECAY_LORA + ICL_LORA + GATE_LORA
REC_IN = 2 * RG_WIDTH + RW_IN
RW_GN_EPS = 64e-5
ATT_HEADS = 16
ATT_KV_HEADS = 4
ATT_GROUP = 4
ATT_HEAD_DIM = 64
WINDOW = 128
ATT_BLOCK = 128
ATT_IN = (ATT_HEADS + 2 * ATT_KV_HEADS) * ATT_HEAD_DIM
N_GROUPS = 4
EXPERTS_PER_GROUP = 8
N_EXPERTS = 32
TOP_K = 2
EXPERT_FF = 512
RMS_EPS = 1e-6
NEG_INF = -1e30

LANES = 128
SUBLANES = 8
VMEM_LIMIT_BYTES = 56 * 1024 * 1024
ROW_TILE = 512
WKV_TILE = 256
WKV_CHUNK = 64
MOE_ROWS = 512
ROUTE_LANES = 128
DMA_ROWS = 256


def _cparams(*sem):
    return pltpu.CompilerParams(dimension_semantics=sem, vmem_limit_bytes=VMEM_LIMIT_BYTES)


def _dot(a, b):
    return jnp.dot(a.astype(BF16), b.astype(BF16), preferred_element_type=F32)


def _dot_nt(a, b):
    return lax.dot_general(a.astype(BF16), b.astype(BF16), (((1,), (1,)), ((), ())),
                           preferred_element_type=F32)


def _split2(x):
    h1 = x.astype(BF16)
    return h1, (x - h1.astype(F32)).astype(BF16)


def _dot_exact_rhs(x, e):
    h1, h2 = _split2(x)
    return jnp.dot(h1, e, preferred_element_type=F32) + jnp.dot(h2, e, preferred_element_type=F32)


def _dot_exact_lhs(e, x):
    h1, h2 = _split2(x)
    return jnp.dot(e, h1, preferred_element_type=F32) + jnp.dot(e, h2, preferred_element_type=F32)


def _sigmoid_tanh(x):
    return 0.5 * jnp.tanh(0.5 * x) + 0.5


def _pack_bf16_pairs(x):
    n = x.shape[1] // 2
    hi = lax.bitcast_convert_type(x[:, :n].astype(BF16).astype(F32), jnp.int32)
    lo = lax.bitcast_convert_type(x[:, n:].astype(BF16).astype(F32), jnp.int32)
    return hi | lax.shift_right_logical(lo, 16)


def _unpack_bf16_pairs(p):
    hi = lax.bitcast_convert_type(p & jnp.int32(-65536), F32)
    lo = lax.bitcast_convert_type(lax.shift_left(p, 16), F32)
    return jnp.concatenate([hi, lo], 1)


def _gelu_tanh(x):
    return 0.5 * x * (1.0 + jnp.tanh(math.sqrt(2.0 / math.pi) * (x + 0.044715 * (x * x * x))))


def _head_ones(width, head):
    r = lax.broadcasted_iota(jnp.int32, (width, width), 0) // head
    c = lax.broadcasted_iota(jnp.int32, (width, width), 1) // head
    return jnp.where(r == c, 1.0, 0.0).astype(BF16)


def _norm_proj_kernel(x_ref, g_ref, w_ref, *out_refs, splits):
    x = x_ref[...]
    h = x * lax.rsqrt(jnp.mean(x * x, -1, keepdims=True) + RMS_EPS) * g_ref[...]
    hb = h.astype(BF16)
    off = 0
    for o_ref, n in zip(out_refs, splits):
        o_ref[...] = jnp.dot(hb, w_ref[:, off:off + n], preferred_element_type=F32).astype(o_ref.dtype)
        off += n


def norm_proj(x2d, g, w, splits, out_dtype=F32):
    T, D = x2d.shape
    N = w.shape[1]
    assert sum(splits) == N and T % ROW_TILE == 0
    return pl.pallas_call(
        functools.partial(_norm_proj_kernel, splits=splits),
        grid=(T // ROW_TILE,),
        in_specs=[pl.BlockSpec((ROW_TILE, D), lambda i: (i, 0)),
                  pl.BlockSpec((1, D), lambda i: (0, 0)),
                  pl.BlockSpec((D, N), lambda i: (0, 0))],
        out_specs=[pl.BlockSpec((ROW_TILE, n), lambda i: (i, 0)) for n in splits],
        out_shape=[jax.ShapeDtypeStruct((T, n), out_dtype) for n in splits],
        compiler_params=_cparams("parallel"),
        name="norm_proj",
    )(x2d, g.reshape(1, D), w.astype(BF16))


RG_HALF = RG_WIDTH // 2
RG_ROWS = 256
RG_PAD = SUBLANES


def _rglru_kernel(x_ref, g_ref, cw_ref, cb_ref, wg_ref, bg_ref, sp_ref, o_ref,
                  xp_ref, af_ref, bf_ref, ab_ref, bb_ref):
    S = x_ref.shape[1]
    C = RG_HALF
    xp_ref[0:RG_PAD, :] = jnp.zeros((RG_PAD, C), F32)
    xp_ref[RG_PAD + S:RG_PAD + S + RG_PAD, :] = jnp.zeros((RG_PAD, C), F32)
    xp_ref[RG_PAD:RG_PAD + S, :] = x_ref[0]
    cw = cw_ref[...]
    left = CONV_WIDTH // 2
    for c in range(S // RG_ROWS):
        r0 = c * RG_ROWS
        xc = cb_ref[...] + cw[0:1] * xp_ref[RG_PAD + r0 - left:RG_PAD + r0 - left + RG_ROWS, :]
        for k in range(1, CONV_WIDTH):
            s0 = RG_PAD + r0 + k - left
            xc = xc + cw[k:k + 1] * xp_ref[s0:s0 + RG_ROWS, :]
        z = _dot(xc, wg_ref[0]) + bg_ref[0]
        for d, (a_ref, b_ref) in enumerate(((af_ref, bf_ref), (ab_ref, bb_ref))):
            r = _sigmoid_tanh(z[:, (2 * d) * C:(2 * d + 1) * C])
            i = _sigmoid_tanh(z[:, (2 * d + 1) * C:(2 * d + 2) * C])
            a = jnp.exp(-RG_C * r * sp_ref[0][:, d * C:(d + 1) * C])
            a_ref[r0:r0 + RG_ROWS, :] = a
            y = 1.0 - a * a
            b_ref[r0:r0 + RG_ROWS, :] = jnp.where(y > 0.0, y * lax.rsqrt(y), 0.0) * (i * xc)

    row8 = lax.broadcasted_iota(jnp.int32, (SUBLANES, C), 0)

    def tile_scan(a, b, carry, reverse):
        for s in (1, 2, 4):
            keep = (row8 < SUBLANES - s) if reverse else (row8 >= s)
            shift = SUBLANES - s if reverse else s
            b = b + a * jnp.where(keep, pltpu.roll(b, shift, 0), 0.0)
            a = a * jnp.where(keep, pltpu.roll(a, shift, 0), 1.0)
        h = b + a * carry
        last = 0 if reverse else SUBLANES - 1
        return h, h[last:last + 1]

    def body(n, carry):
        hf, hb = carry
        r0 = pl.multiple_of(n * SUBLANES, SUBLANES)
        h8, hf = tile_scan(af_ref[pl.ds(r0, SUBLANES), :], bf_ref[pl.ds(r0, SUBLANES), :], hf, False)
        bf_ref[pl.ds(r0, SUBLANES), :] = h8
        r1 = pl.multiple_of(S - SUBLANES - n * SUBLANES, SUBLANES)
        h8, hb = tile_scan(ab_ref[pl.ds(r1, SUBLANES), :], bb_ref[pl.ds(r1, SUBLANES), :], hb, True)
        bb_ref[pl.ds(r1, SUBLANES), :] = h8
        return hf, hb

    zero = jnp.zeros((1, C), F32)
    lax.fori_loop(0, S // SUBLANES, body, (zero, zero), unroll=2)
    for c in range(S // RG_ROWS):
        sl = slice(c * RG_ROWS, (c + 1) * RG_ROWS)
        o_ref[0, sl, :] = ((bf_ref[sl, :] + bb_ref[sl, :]) * _gelu_tanh(g_ref[0, sl, :])).astype(o_ref.dtype)


def rglru_branch(rg_x, rg_g, conv_w, conv_b, wa, ba, wx, bx, lam):
    B, S, _ = rg_x.shape
    C = RG_HALF
    nb = C // RG_BLOCK_DIM

    def bdiag(w):
        w = w.reshape(2, nb, RG_BLOCK_DIM, RG_BLOCK_DIM)
        eye = jnp.eye(nb, dtype=w.dtype)
        return jnp.einsum('hnij,nm->hnimj', w, eye).reshape(2, C, C)

    wg = jnp.concatenate([bdiag(wa[0]), bdiag(wx[0]), bdiag(wa[1]), bdiag(wx[1])], axis=-1).astype(BF16)

    def halves(v):
        return v.reshape(2, 1, C)

    bg = jnp.concatenate([halves(ba[0]), halves(bx[0]), halves(ba[1]), halves(bx[1])], axis=-1)
    sp = jax.nn.softplus(-lam.astype(F32))
    spg = jnp.concatenate([halves(sp[0]), halves(sp[1])], axis=-1)
    return pl.pallas_call(
        _rglru_kernel,
        grid=(B, 2),
        in_specs=[pl.BlockSpec((1, S, C), lambda b, c: (b, 0, c)),
                  pl.BlockSpec((1, S, C), lambda b, c: (b, 0, c)),
                  pl.BlockSpec((CONV_WIDTH, C), lambda b, c: (0, c)),
                  pl.BlockSpec((1, C), lambda b, c: (0, c)),
                  pl.BlockSpec((1, C, 4 * C), lambda b, c: (c, 0, 0)),
                  pl.BlockSpec((1, 1, 4 * C), lambda b, c: (c, 0, 0)),
                  pl.BlockSpec((1, 1, 2 * C), lambda b, c: (c, 0, 0))],
        out_specs=pl.BlockSpec((1, S, C), lambda b, c: (b, 0, c)),
        out_shape=jax.ShapeDtypeStruct((B, S, RG_WIDTH), BF16),
        scratch_shapes=[pltpu.VMEM((S + 2 * RG_PAD, C), F32)] + [pltpu.VMEM((S, C), F32)] * 4,
        compiler_params=_cparams("parallel", "parallel"),
        name="rglru",
    )(rg_x, rg_g, conv_w, conv_b.reshape(1, RG_WIDTH), wg, bg, spg)


def _rwkv_prep_kernel(u_ref, up_ref, un_ref, mul_ref, mur_ref, wl_ref, w0_ref, a0_ref, kk_ref, ka_ref,
                      rk_ref, rt_ref, kt_ref, bt_ref, kh_ref, v_ref, le_ref, bonus_ref, g_ref):
    i = pl.program_id(1)
    nt = pl.num_programs(1)
    TS = u_ref.shape[1]
    W = RW_WIDTH
    u = u_ref[0]
    prow = jnp.where(i == 0, 0.0, up_ref[0][SUBLANES - 1:SUBLANES, :])
    nrow = jnp.where(i == nt - 1, 0.0, un_ref[0][0:1, :])
    rows = lax.broadcasted_iota(jnp.int32, (TS, 1), 0)
    prev = jnp.where(rows == 0, prow, pltpu.roll(u, 1, 0))
    nxt = jnp.where(rows == TS - 1, nrow, pltpu.roll(u, TS - 1, 0))
    m = u + mul_ref[...] * (prev - u) + mur_ref[...] * (nxt - u)
    r = m[:, 0:W]
    k = m[:, W:2 * W]
    v = m[:, 2 * W:3 * W]
    tail = m[:, 3 * W:3 * W + LANES]
    lane = lax.broadcasted_iota(jnp.int32, (1, LANES), 1)
    z = jnp.where(lane < DECAY_LORA, jnp.tanh(tail),
                  jnp.where(lane < DECAY_LORA + ICL_LORA, tail, _sigmoid_tanh(tail)))
    lo = _dot(z, wl_ref[...])
    ones = _head_ones(W, RW_HEAD_DIM)
    kkr = k * kk_ref[...]
    kk = kkr * lax.rsqrt(jnp.maximum(_dot_exact_rhs(kkr * kkr, ones), 1e-24))
    tr = lax.broadcasted_iota(jnp.int32, (TS, TS), 0)
    tc = lax.broadcasted_iota(jnp.int32, (TS, TS), 1)
    same = (tr // WKV_CHUNK) == (tc // WKV_CHUNK)
    kd_sum = jnp.zeros((TS, W), F32)
    nch = TS // WKV_CHUNK
    for d in range(2):
        ld = -math.exp(-0.5) * _sigmoid_tanh(w0_ref[d:d + 1, :] + lo[:, d * W:(d + 1) * W])
        a = _sigmoid_tanh(a0_ref[d:d + 1, :] + lo[:, (2 + d) * W:(3 + d) * W])
        kd = k * (1.0 + (a - 1.0) * ka_ref[...])
        kd_sum = kd_sum + kd
        tri = jnp.where(same & ((tc <= tr) if d == 0 else (tc >= tr)), 1.0, 0.0).astype(BF16)
        L = _dot_exact_lhs(tri, ld)
        en = jnp.exp(-L)
        rt_ref[d, 0] = (r * jnp.exp(L)).astype(BF16)
        kt_ref[d, 0] = (kk * jnp.exp(L - ld)).astype(BF16)
        bt_ref[d, 0] = (kk * a * en).astype(BF16)
        kh_ref[d, 0] = (kd * en).astype(BF16)
        ends = [L[(c + 1) * WKV_CHUNK - 1:(c + 1) * WKV_CHUNK] if d == 0 else L[c * WKV_CHUNK:c * WKV_CHUNK + 1]
                for c in range(nch)]
        le_ref[d, 0, 0] = jnp.concatenate(ends, 0)
    v_ref[0] = v.astype(BF16)
    bonus_ref[0] =_dot_exact_rhs(r * kd_sum * rk_ref[...], ones) * v
    g_ref[0] = lo[:, 4 * W:5 * W]


def rwkv_prep(rw, mu_l, mu_r, w0, w_up, a0, a_up, g_up, k_k, k_a, r_k):
    B, S, _ = rw.shape
    TS = WKV_TILE
    W = RW_WIDTH
    nt = S // TS
    nch = TS // WKV_CHUNK
    hb = TS // SUBLANES
    wl = jnp.zeros((LANES, 5 * W), F32)
    wl = wl.at[0:DECAY_LORA, 0:W].set(w_up[0]).at[0:DECAY_LORA, W:2 * W].set(w_up[1])
    o = DECAY_LORA
    wl = wl.at[o:o + ICL_LORA, 2 * W:3 * W].set(a_up[0]).at[o:o + ICL_LORA, 3 * W:4 * W].set(a_up[1])
    o += ICL_LORA
    wl = wl.at[o:o + GATE_LORA, 4 * W:5 * W].set(g_up)
    vec = lambda n: pl.BlockSpec((1, n), lambda b, i: (0, 0))
    big = lambda: pl.BlockSpec((2, 1, TS, W), lambda b, i: (0, b, i, 0))
    one = lambda: pl.BlockSpec((1, TS, W), lambda b, i: (b, i, 0))
    return pl.pallas_call(
        _rwkv_prep_kernel,
        grid=(B, nt),
        in_specs=[pl.BlockSpec((1, TS, RW_IN), lambda b, i: (b, i, 0)),
                  pl.BlockSpec((1, SUBLANES, RW_IN), lambda b, i: (b, jnp.maximum(i * hb - 1, 0), 0)),
                  pl.BlockSpec((1, SUBLANES, RW_IN), lambda b, i: (b, jnp.minimum((i + 1) * hb, S // SUBLANES - 1), 0)),
                  vec(RW_IN), vec(RW_IN),
                  pl.BlockSpec((LANES, 5 * W), lambda b, i: (0, 0)),
                  pl.BlockSpec((2, W), lambda b, i: (0, 0)),
                  pl.BlockSpec((2, W), lambda b, i: (0, 0)),
                  vec(W), vec(W), vec(W)],
        out_specs=[big(), big(), big(), big(), one(),
                   pl.BlockSpec((2, 1, 1, nch, W), lambda b, i: (0, b, i, 0, 0)),
                   one(), one()],
        out_shape=[jax.ShapeDtypeStruct((2, B, S, W), BF16)] * 4
        + [jax.ShapeDtypeStruct((B, S, W), BF16),
           jax.ShapeDtypeStruct((2, B, nt, nch, W), F32),
           jax.ShapeDtypeStruct((B, S, W), F32),
           jax.ShapeDtypeStruct((B, S, W), F32)],
        compiler_params=_cparams("parallel", "parallel"),
        name="rwkv_prep",
    )(rw, rw, rw, mu_l.reshape(1, RW_IN), mu_r.reshape(1, RW_IN), wl.astype(BF16), w0, a0,
      k_k.reshape(1, W), k_a.reshape(1, W), r_k.reshape(1, W))


def _wkv_tiles(probs):
    TS = WKV_TILE
    C = WKV_CHUNK
    N = RW_HEAD_DIM
    P = 2 * N
    nch = TS // C
    zero = jnp.zeros((), BF16)
    head0 = lax.broadcasted_iota(jnp.int32, (1, P), 1) < N
    head0_2 = (lax.broadcasted_iota(jnp.int32, (1, 2 * P), 1) % P) < N
    head0_w = (lax.broadcasted_iota(jnp.int32, (1, nch * P), 1) % P) < N
    own = (lax.broadcasted_iota(jnp.int32, (TS, nch * P), 0) // C
           == lax.broadcasted_iota(jnp.int32, (TS, nch * P), 1) // P)
    tq = lax.broadcasted_iota(jnp.int32, (C, TS), 0)
    sq = lax.broadcasted_iota(jnp.int32, (C, TS), 1) % C
    blk = (lax.broadcasted_iota(jnp.int32, (TS, TS), 0) // C
           == lax.broadcasted_iota(jnp.int32, (TS, TS), 1) // C)
    hr = lax.broadcasted_iota(jnp.int32, (P, P), 0)
    hc = lax.broadcasted_iota(jnp.int32, (P, P), 1)
    bdiag = (hr // N) == (hc // N)
    heye = hr == hc

    def both_heads(xb, m):
        return jnp.concatenate([jnp.where(m, xb, zero), jnp.where(m, zero, xb)], 0)

    def wide(xb):
        return jnp.concatenate([xb[c * C:(c + 1) * C] for c in range(nch)], 1)

    def expand(xw):
        return jnp.where(blk, jnp.concatenate([xw] * nch, 0), zero)

    st = []
    for (rt, kt, bt, kh, v, le, H, reverse) in probs:
        ktb = kt.astype(BF16)
        vb = v.astype(BF16)
        ktw = wide(ktb)
        rtw = wide(rt.astype(BF16))
        lhs = jnp.concatenate([jnp.where(head0_w, ktw, zero), jnp.where(head0_w, rtw, zero),
                               jnp.where(head0_w, zero, ktw), jnp.where(head0_w, zero, rtw)], 0)
        rhs = jnp.concatenate([jnp.where(own, jnp.concatenate([bt.astype(BF16)] * nch, 1), zero),
                               jnp.where(own, jnp.concatenate([kh.astype(BF16)] * nch, 1), zero)], 0)
        gram = lax.dot_general(lhs, rhs, (((1,), (1,)), ((), ())), preferred_element_type=F32)
        st.append(dict(ktb=ktb, vb=vb, gram=gram, vm=both_heads(vb, head0)))

    chains = []
    for s, prob in zip(st, probs):
        reverse = prob[7]
        strict = (sq > tq) if reverse else (sq < tq)
        incl = (sq >= tq) if reverse else (sq <= tq)
        s["b_bd"], s["rb_bd"], s["rk_bd"] = [], [], []
        for h in range(2):
            g0 = s["gram"][2 * h * C:(2 * h + 1) * C]
            g1 = s["gram"][(2 * h + 1) * C:(2 * h + 2) * C]
            A = jnp.where(strict, g0[:, 0:TS], 0.0)
            s["b_bd"].append(expand(jnp.where(strict, g0[:, TS:2 * TS], 0.0).astype(BF16)))
            s["rb_bd"].append(expand(jnp.where(incl, g1[:, 0:TS], 0.0).astype(BF16)))
            s["rk_bd"].append(expand(jnp.where(incl, g1[:, TS:2 * TS], 0.0).astype(BF16)))
            chains.append(dict(Tw=jnp.where(sq == tq, 1.0, 0.0) - A, Ab=A.astype(BF16)))
    for ch in chains:
        ch["Q"] = jnp.dot(ch["Ab"], expand(ch["Ab"]), preferred_element_type=F32)
    for _ in range(int(math.log2(C)) - 2):
        for ch in chains:
            Qb = ch["Q"].astype(BF16)
            out = jnp.dot(jnp.concatenate([ch["Tw"].astype(BF16), Qb], 0), expand(Qb), preferred_element_type=F32)
            ch["Tw"] = ch["Tw"] + out[0:C]
            ch["Q"] = out[C:2 * C]
    for ch in chains:
        Tw = ch["Tw"] + jnp.dot(ch["Tw"].astype(BF16), expand(ch["Q"].astype(BF16)), preferred_element_type=F32)
        ch["t_bd"] = expand(Tw.astype(BF16))

    for n, s in enumerate(st):
        s["Bv"] = jnp.dot(jnp.concatenate(s["b_bd"], 1), s["vm"], preferred_element_type=F32)
    for n, s in enumerate(st):
        wm = both_heads(jnp.concatenate([s["ktb"], s["Bv"].astype(BF16)], 1), head0_2)
        t_bd = [chains[2 * n]["t_bd"], chains[2 * n + 1]["t_bd"]]
        s["x1"] = jnp.dot(jnp.concatenate(t_bd, 1), wm, preferred_element_type=F32)
    for s, prob in zip(st, probs):
        x1 = s["x1"]
        rhs2 = jnp.concatenate([both_heads(-x1.astype(BF16), head0_2),
                                jnp.concatenate([jnp.zeros((2 * TS, P), BF16), s["vm"]], 1)], 0)
        x2 = jnp.dot(jnp.concatenate(s["rb_bd"] + s["rk_bd"], 1), rhs2, preferred_element_type=F32)
        s["K2"] = x1[:, 0:P]
        s["V2"] = x1[:, P:2 * P]
        s["R2"] = prob[0] + x2[:, 0:P]
        s["Y2"] = x2[:, P:2 * P]
        s["H"] = prob[6]
        s["pc"] = jnp.exp(prob[5])
        s["ys"] = [None] * nch

    for ci in range(nch):
        for s, prob in zip(st, probs):
            (rt, kt, bt, kh, v, le, _, reverse) = prob
            c = nch - 1 - ci if reverse else ci
            sl = slice(c * C, (c + 1) * C)
            pc = s["pc"][c:c + 1]
            bh = (bt[sl] * pc).T
            khh = (kh[sl] * pc).T
            Mc = jnp.where(heye, pc, 0.0) - jnp.where(bdiag, _dot(bh, s["K2"][sl]), 0.0)
            Gc = jnp.where(bdiag, _dot(jnp.concatenate([khh, -bh], 1),
                                       jnp.concatenate([v[sl], s["V2"][sl]], 0)), 0.0)
            s["ys"][c] = _dot(s["R2"][sl], s["H"]) + s["Y2"][sl]
            s["H"] = _dot(Mc, s["H"]) + Gc
    return [(jnp.concatenate(s["ys"], 0), s["H"]) for s in st]


WKV_PAIRS = 2


def _wkv_kernel(rtf_ref, ktf_ref, btf_ref, khf_ref, vf_ref, lef_ref,
                rtb_ref, ktb_ref, btb_ref, khb_ref, vb_ref, leb_ref, yf_ref, yb_ref, h_ref):
    i = pl.program_id(2)

    @pl.when(i == 0)
    def _():
        h_ref[...] = jnp.zeros_like(h_ref)

    P = 2 * RW_HEAD_DIM
    dirs = ((rtf_ref, ktf_ref, btf_ref, khf_ref, vf_ref, lef_ref, yf_ref),
            (rtb_ref, ktb_ref, btb_ref, khb_ref, vb_ref, leb_ref, yb_ref))
    probs, outs = [], []
    for d, (rt_ref, kt_ref, bt_ref, kh_ref, v_ref, le_ref, y_ref) in enumerate(dirs):
        for pr in range(WKV_PAIRS):
            ln = slice(pr * P, (pr + 1) * P)
            probs.append((rt_ref[0, 0, :, ln], kt_ref[0, 0, :, ln], bt_ref[0, 0, :, ln], kh_ref[0, 0, :, ln],
                          v_ref[0, :, ln], le_ref[0, 0, 0, :, ln], h_ref[d, pr], d == 1))
            outs.append((y_ref, d, pr, ln))
    for (y, H), (y_ref, d, pr, ln) in zip(_wkv_tiles(probs), outs):
        y_ref[0, :, ln] = y
        h_ref[d, pr] = H


def wkv_scan(rt, kt, bt, kh, v, le):
    _, B, S, W = rt.shape
    TS = WKV_TILE
    nt = S // TS
    nch = TS // WKV_CHUNK
    P = 2 * RW_HEAD_DIM
    PW = WKV_PAIRS * P
    tiles = (lambda i: i, lambda i: nt - 1 - i)

    def dir_specs(d):
        t = tiles[d]
        big = lambda: pl.BlockSpec((1, 1, TS, PW), lambda b, p, i: (d, b, t(i), p))
        return [big(), big(), big(), big(),
                pl.BlockSpec((1, TS, PW), lambda b, p, i: (b, t(i), p)),
                pl.BlockSpec((1, 1, 1, nch, PW), lambda b, p, i: (d, b, t(i), 0, p))]

    return pl.pallas_call(
        _wkv_kernel,
        grid=(B, W // PW, nt),
        in_specs=dir_specs(0) + dir_specs(1),
        out_specs=[pl.BlockSpec((1, TS, PW), lambda b, p, i: (b, tiles[0](i), p)),
                   pl.BlockSpec((1, TS, PW), lambda b, p, i: (b, tiles[1](i), p))],
        out_shape=[jax.ShapeDtypeStruct((B, S, W), F32)] * 2,
        scratch_shapes=[pltpu.VMEM((2, WKV_PAIRS, P, P), F32)],
        compiler_params=_cparams("parallel", "parallel", "arbitrary"),
        name="wkv",
    )(rt, kt, bt, kh, v, le, rt, kt, bt, kh, v, le)


def _rec_out_kernel(x_ref, rg_ref, yf_ref, yb_ref, bonus_ref, g_ref, lnw_ref, lnb_ref, w_ref, o_ref):
    W = RW_WIDTH
    ones = _head_ones(W, RW_HEAD_DIM)
    y = yf_ref[...] + yb_ref[...]
    inv_n = 1.0 / RW_HEAD_DIM
    mu = _dot_exact_rhs(y, ones) * inv_n
    yc = y - mu
    var = _dot_exact_rhs(yc * yc, ones) * inv_n
    yn = yc * lax.rsqrt(var + RW_GN_EPS) * lnw_ref[...] + lnb_ref[...]
    rw_out = (yn + bonus_ref[...]) * g_ref[...]
    mix = (jnp.dot(rg_ref[...], w_ref[0:RG_WIDTH, :], preferred_element_type=F32)
           + jnp.dot(rw_out.astype(BF16), w_ref[RG_WIDTH:RG_WIDTH + W, :], preferred_element_type=F32))
    o_ref[...] = x_ref[...] + mix


def rec_out(x2d, rg_out, y_f, y_b, bonus, g, ln_w, ln_b, w_out):
    T, D = x2d.shape
    W = RW_WIDTH
    row = lambda n: pl.BlockSpec((ROW_TILE, n), lambda i: (i, 0))
    vec = lambda n: pl.BlockSpec((1, n), lambda i: (0, 0))
    return pl.pallas_call(
        _rec_out_kernel,
        grid=(T // ROW_TILE,),
        in_specs=[row(D), row(RG_WIDTH), row(W), row(W), row(W), row(W), vec(W), vec(W),
                  pl.BlockSpec((RG_WIDTH + W, D), lambda i: (0, 0))],
        out_specs=row(D),
        out_shape=jax.ShapeDtypeStruct((T, D), F32),
        compiler_params=_cparams("parallel"),
        name="rec_out",
    )(x2d, rg_out, y_f, y_b, bonus, g, ln_w.reshape(1, W), ln_b.reshape(1, W), w_out.astype(BF16))


def recurrent_layer(x, norm_g, w_in, conv_w, conv_b, rg_wa, rg_ba, rg_wx, rg_bx, rg_lambda,
                    mu_l, mu_r, w0, w_up, a0, a_up, g_up, k_k, k_a, r_k, ln_w, ln_b, w_out):
    B, S, D = x.shape
    T = B * S
    x2d = x.reshape(T, D)
    rg_x, rg_g, rw = norm_proj(x2d, norm_g, w_in, (RG_WIDTH, RG_WIDTH, RW_IN))
    rg_out = rglru_branch(rg_x.reshape(B, S, RG_WIDTH), rg_g.reshape(B, S, RG_WIDTH), conv_w, conv_b,
                          rg_wa, rg_ba, rg_wx, rg_bx, rg_lambda)
    rt, kt, bt, kh, v, le, bonus, g = rwkv_prep(rw.reshape(B, S, RW_IN), mu_l, mu_r, w0, w_up, a0, a_up,
                                                g_up, k_k, k_a, r_k.reshape(-1))
    y_f, y_b = wkv_scan(rt, kt, bt, kh, v, le)
    flat = lambda t: t.reshape(T, -1)
    out = rec_out(x2d, flat(rg_out), flat(y_f), flat(y_b), flat(bonus), flat(g), ln_w, ln_b, w_out)
    return out.reshape(B, S, D)


def _alibi_slope(h):
    return 2.0 ** (-8.0 * (h + 1) / ATT_HEADS)


ATT_PAIR = 2 * ATT_HEAD_DIM
ATT_KVW = ATT_KV_HEADS * ATT_PAIR


def _qkv_proj_kernel(x_ref, g_ref, w_ref, eq_ref, ek_ref, qg_ref, kg_ref, q_ref, k_ref, v_ref):
    QW = ATT_HEADS * ATT_HEAD_DIM
    x = x_ref[...]
    h = (x * lax.rsqrt(jnp.mean(x * x, -1, keepdims=True) + RMS_EPS) * g_ref[...]).astype(BF16)
    q = jnp.dot(h, w_ref[:, 0:QW], preferred_element_type=F32)
    msq = jnp.dot((q * q).astype(BF16), eq_ref[...], preferred_element_type=F32) * (1.0 / ATT_HEAD_DIM)
    q_ref[...] = (q * lax.rsqrt(msq + RMS_EPS) * qg_ref[...]).astype(BF16)
    k = jnp.dot(h, w_ref[:, QW:QW + ATT_KVW], preferred_element_type=F32)
    msk = jnp.dot((k * k).astype(BF16), ek_ref[...], preferred_element_type=F32) * (1.0 / ATT_PAIR)
    k_ref[...] = (k * lax.rsqrt(msk + RMS_EPS) * kg_ref[...]).astype(BF16)
    v_ref[...] = jnp.dot(h, w_ref[:, QW + ATT_KVW:QW + 2 * ATT_KVW], preferred_element_type=F32).astype(BF16)


def qkv_proj(x2d, g, w_in, q_norm, k_norm):
    T, D = x2d.shape
    QW = ATT_HEADS * ATT_HEAD_DIM
    KW = ATT_KV_HEADS * ATT_HEAD_DIM
    dup = lambda w: jnp.concatenate([w.reshape(D, ATT_KV_HEADS, 1, ATT_HEAD_DIM)] * 2, 2).reshape(D, ATT_KVW)
    w = jnp.concatenate([w_in[:, :QW], dup(w_in[:, QW:QW + KW]), dup(w_in[:, QW + KW:])], 1).astype(BF16)
    N = QW + 2 * ATT_KVW
    qg = jnp.tile(q_norm.astype(F32), ATT_HEADS).reshape(1, QW) * (ATT_HEAD_DIM ** -0.5)
    kg = jnp.tile(k_norm.astype(F32), 2 * ATT_KV_HEADS).reshape(1, ATT_KVW)
    row = lambda n: pl.BlockSpec((ROW_TILE, n), lambda i: (i, 0))
    fixed = lambda r, n: pl.BlockSpec((r, n), lambda i: (0, 0))
    return pl.pallas_call(
        _qkv_proj_kernel,
        grid=(T // ROW_TILE,),
        in_specs=[row(D), fixed(1, D), fixed(D, N), fixed(QW, QW), fixed(ATT_KVW, ATT_KVW),
                  fixed(1, QW), fixed(1, ATT_KVW)],
        out_specs=[row(QW), row(ATT_KVW), row(ATT_KVW)],
        out_shape=[jax.ShapeDtypeStruct((T, QW), BF16), jax.ShapeDtypeStruct((T, ATT_KVW), BF16),
                   jax.ShapeDtypeStruct((T, ATT_KVW), BF16)],
        compiler_params=_cparams("parallel"),
        name="qkv_proj",
    )(x2d, g.reshape(1, D), w, _head_ones(QW, ATT_HEAD_DIM), _head_ones(ATT_KVW, ATT_PAIR), qg, kg)


def _attn_kernel(sink_ref, x_ref, q_ref, kp_ref, kc_ref, kn_ref, vp_ref, vc_ref, vn_ref, bias_ref,
                 w_ref, o_ref):
    i = pl.program_id(1)
    nb = pl.num_programs(1)
    BLK = ATT_BLOCK
    P = ATT_PAIR
    span = 3 * BLK
    kc = jnp.concatenate([kp_ref[0], kc_ref[0], kn_ref[0]], 0)
    vc = jnp.concatenate([vp_ref[0], vc_ref[0], vn_ref[0]], 0)
    kpos = lax.broadcasted_iota(jnp.int32, (1, span), 1) + (i - 1) * BLK
    edge = jnp.where((kpos >= 0) & (kpos < nb * BLK), 0.0, NEG_INF)
    lane = lax.broadcasted_iota(jnp.int32, (1, P), 1)
    low = lane < ATT_HEAD_DIM
    zero = jnp.zeros((), BF16)
    ones = jnp.ones((span, P), BF16)
    slabs = []
    for g in range(ATT_KV_HEADS):
        kg = kc[:, g * P:(g + 1) * P]
        k_half = (jnp.where(low, kg, zero), jnp.where(low, zero, kg))
        v_ext = jnp.concatenate([vc[:, g * P:(g + 1) * P], ones], 1)
        for pr in range(ATT_GROUP // 2):
            slab = g * (ATT_GROUP // 2) + pr
            qp = q_ref[0, :, slab * P:(slab + 1) * P]
            halves = []
            for hf in range(2):
                h = 2 * slab + hf
                s = lax.dot_general(qp, k_half[hf], (((1,), (1,)), ((), ())), preferred_element_type=F32)
                s = s + bias_ref[h] + edge
                sk = sink_ref[h]
                m = jnp.maximum(jnp.max(s, -1, keepdims=True), sk)
                p = jnp.exp(s - m).astype(BF16)
                o = jnp.dot(p, v_ext, preferred_element_type=F32)
                halves.append(o[:, 0:P] / (o[:, P:2 * P] + jnp.exp(sk - m)))
            slabs.append(jnp.where(low, halves[0], halves[1]))
    o = jnp.concatenate(slabs, -1).astype(BF16)
    o_ref[0] = x_ref[0] + jnp.dot(o, w_ref[...], preferred_element_type=F32)


def attention_layer(x, norm_g, w_in, q_norm, k_norm, sink, w_out):
    B, S, D = x.shape
    T = B * S
    QW = ATT_HEADS * ATT_HEAD_DIM
    q, k, v = qkv_proj(x.reshape(T, D), norm_g, w_in, q_norm, k_norm)
    q = q.reshape(B, S, QW)
    k = k.reshape(B, S, ATT_KVW)
    v = v.reshape(B, S, ATT_KVW)
    nb = S // ATT_BLOCK
    span = 3 * ATT_BLOCK
    rel = (jnp.arange(span)[None, :] - WINDOW) - jnp.arange(ATT_BLOCK)[:, None]
    slopes = jnp.asarray([_alibi_slope(h) for h in range(ATT_HEADS)], F32)
    bias = jnp.where(jnp.abs(rel) <= WINDOW, -slopes[:, None, None] * jnp.abs(rel).astype(F32), NEG_INF)
    prev = lambda b, i: (b, jnp.maximum(i - 1, 0), 0)
    cur = lambda b, i: (b, i, 0)
    nxt = lambda b, i: (b, jnp.minimum(i + 1, nb - 1), 0)
    kv = lambda f: pl.BlockSpec((1, ATT_BLOCK, ATT_KVW), f)
    return pl.pallas_call(
        _attn_kernel,
        grid=(B, nb),
        in_specs=[pl.BlockSpec(memory_space=pltpu.SMEM),
                  pl.BlockSpec((1, ATT_BLOCK, D), cur),
                  pl.BlockSpec((1, ATT_BLOCK, QW), cur),
                  kv(prev), kv(cur), kv(nxt), kv(prev), kv(cur), kv(nxt),
                  pl.BlockSpec((ATT_HEADS, ATT_BLOCK, span), lambda b, i: (0, 0, 0)),
                  pl.BlockSpec((QW, D), lambda b, i: (0, 0))],
        out_specs=pl.BlockSpec((1, ATT_BLOCK, D), cur),
        out_shape=jax.ShapeDtypeStruct((B, S, D), F32),
        compiler_params=_cparams("parallel", "parallel"),
        name="window_attn",
    )(sink.astype(F32), x, q, k, k, k, v, v, v, bias, w_out.astype(BF16))


ROUTE_OFF = N_GROUPS


def _router_kernel(x_ref, g_ref, w1_ref, w2_ref, b_ref, h_ref, ri_ref, rg_ref, cnt_ref, run_ref):
    i = pl.program_id(0)

    @pl.when(i == 0)
    def _():
        run_ref[...] = jnp.zeros_like(run_ref)

    TM = x_ref.shape[0]
    x = x_ref[...]
    h = x * lax.rsqrt(jnp.mean(x * x, -1, keepdims=True) + RMS_EPS) * g_ref[...]
    h_ref[...] = _pack_bf16_pairs(h)
    h1 = h.astype(BF16)
    h2 = (h - h1.astype(F32)).astype(BF16)
    lg = (jnp.dot(h1, w1_ref[...], preferred_element_type=F32) + jnp.dot(h1, w2_ref[...], preferred_element_type=F32)
          + jnp.dot(h2, w1_ref[...], preferred_element_type=F32)) + b_ref[...]
    lane = lax.broadcasted_iota(jnp.int32, (TM, ROUTE_LANES), 1)
    far = ROUTE_LANES
    gmask = lane < N_GROUPS
    gl = jnp.where(gmask, lg, NEG_INF)
    gm = jnp.max(gl, -1, keepdims=True)
    p_group = 1.0 / jnp.sum(jnp.where(gmask, jnp.exp(gl - gm), 0.0), -1, keepdims=True)
    group = jnp.min(jnp.where(gl == gm, lane, far), -1, keepdims=True)
    fmask = (lane >= ROUTE_OFF) & (lane < ROUTE_OFF + N_EXPERTS) & ((lane - ROUTE_OFF) // EXPERTS_PER_GROUP == group)
    fl = jnp.where(fmask, lg, NEG_INF)
    m1 = jnp.max(fl, -1, keepdims=True)
    ssum = jnp.sum(jnp.where(fmask, jnp.exp(fl - m1), 0.0), -1, keepdims=True)
    i1 = jnp.min(jnp.where(fl == m1, lane, far), -1, keepdims=True)
    fl2 = jnp.where(lane == i1, NEG_INF, fl)
    m2 = jnp.max(fl2, -1, keepdims=True)
    i2 = jnp.min(jnp.where(fl2 == m2, lane, far), -1, keepdims=True)
    p1 = 1.0 / ssum
    p2 = jnp.exp(m2 - m1) / ssum
    norm = p_group / (p1 + p2)
    oh = jnp.where((lane == i1) | (lane == i2), 1.0, 0.0)
    tr = lax.broadcasted_iota(jnp.int32, (TM, TM), 0)
    tc = lax.broadcasted_iota(jnp.int32, (TM, TM), 1)
    before = jnp.where(tc < tr, 1.0, 0.0).astype(BF16)
    pre = jnp.dot(before, oh.astype(BF16), preferred_element_type=F32) + run_ref[...]
    rank1 = jnp.sum(jnp.where(lane == i1, pre, 0.0), -1, keepdims=True)
    rank2 = jnp.sum(jnp.where(lane == i2, pre, 0.0), -1, keepdims=True)
    total = run_ref[...] + jnp.sum(oh, 0, keepdims=True)
    run_ref[...] = total
    cnt_ref[...] = total.astype(jnp.int32)
    ri_ref[...] = jnp.where(lane == 0, i1 - ROUTE_OFF,
                            jnp.where(lane == 1, i2 - ROUTE_OFF,
                                      jnp.where(lane == 2, rank1.astype(jnp.int32),
                                                jnp.where(lane == 3, rank2.astype(jnp.int32), 0))))
    rg_ref[...] = jnp.where(lane == 0, p1 * norm, jnp.where(lane == 1, p2 * norm, 0.0))


def moe_router(x2d, g, wg1, bg1, wg2, bg2):
    T, D = x2d.shape
    wr = jnp.zeros((D, ROUTE_LANES), F32)
    wr = wr.at[:, 0:N_GROUPS].set(wg1)
    wr = wr.at[:, ROUTE_OFF:ROUTE_OFF + N_EXPERTS].set(jnp.moveaxis(wg2, 0, 1).reshape(D, N_EXPERTS))
    w1 = wr.astype(BF16)
    w2 = (wr - w1.astype(F32)).astype(BF16)
    bias = jnp.zeros((1, ROUTE_LANES), F32)
    bias = bias.at[0, 0:N_GROUPS].set(bg1).at[0, ROUTE_OFF:ROUTE_OFF + N_EXPERTS].set(bg2.reshape(-1))
    row = lambda n: pl.BlockSpec((ROW_TILE, n), lambda i: (i, 0))
    fixed = lambda r, n: pl.BlockSpec((r, n), lambda i: (0, 0))
    return pl.pallas_call(
        _router_kernel,
        grid=(T // ROW_TILE,),
        in_specs=[row(D), fixed(1, D), fixed(D, ROUTE_LANES), fixed(D, ROUTE_LANES), fixed(1, ROUTE_LANES)],
        out_specs=[row(D // 2), row(ROUTE_LANES), row(ROUTE_LANES), fixed(1, ROUTE_LANES)],
        out_shape=[jax.ShapeDtypeStruct((T, D // 2), jnp.int32),
                   jax.ShapeDtypeStruct((T, ROUTE_LANES), jnp.int32),
                   jax.ShapeDtypeStruct((T, ROUTE_LANES), F32),
                   jax.ShapeDtypeStruct((1, ROUTE_LANES), jnp.int32)],
        scratch_shapes=[pltpu.VMEM((1, ROUTE_LANES), F32)],
        compiler_params=_cparams("arbitrary"),
        name="moe_router",
    )(x2d, g.reshape(1, D), w1, w2, bias)


def _row_copy(src, src_row, dst, dst_row, sem):
    return pltpu.make_async_copy(src.at[pl.ds(src_row, 1)], dst.at[pl.ds(dst_row, 1)], sem)


DMA_UNROLL = 8


def _rows_wait(ref, nrows, sem):
    pltpu.make_async_copy(ref.at[pl.ds(0, nrows)], ref.at[pl.ds(0, nrows)], sem).wait()


DISPATCH_ROWS = 512


def _dispatch_kernel(pend_ref, dest_ref, h_ref, xs_ref, zero_ref, sem, zsem):
    i = pl.program_id(0)
    n = h_ref.shape[0]
    nblk = xs_ref.shape[0] // MOE_ROWS

    @pl.when(i == 0)
    def _():
        zero_ref[...] = jnp.zeros_like(zero_ref)

        def block_copy(row0):
            return pltpu.make_async_copy(zero_ref, xs_ref.at[pl.ds(pl.multiple_of(row0, MOE_ROWS), MOE_ROWS)], zsem)

        tails = [jnp.maximum(pend_ref[e] - MOE_ROWS, 0) for e in range(N_EXPERTS)]
        for t in tails:
            block_copy(t).start()
        for t in tails:
            block_copy(t).wait()
        first_unused = pend_ref[N_EXPERTS - 1] // MOE_ROWS

        def clear(j, c):
            block_copy(j * MOE_ROWS).start()
            block_copy(j * MOE_ROWS).wait()
            return c

        lax.fori_loop(first_unused, nblk, clear, 0)

    def issue(r, c):
        for k in range(TOP_K):
            _row_copy(h_ref, r, xs_ref, dest_ref[0, 0, TOP_K * r + k], sem).start()
        return c

    lax.fori_loop(0, n, issue, 0, unroll=DMA_UNROLL)
    _rows_wait(xs_ref, TOP_K * n, sem)


def moe_dispatch(h2d, dest, p_end, rows):
    T, D = h2d.shape
    nt = T // DISPATCH_ROWS
    return pl.pallas_call(
        _dispatch_kernel,
        grid_spec=pltpu.PrefetchScalarGridSpec(
            num_scalar_prefetch=1,
            grid=(nt,),
            in_specs=[pl.BlockSpec((1, 1, TOP_K * DISPATCH_ROWS), lambda i, pe: (i, 0, 0),
                                   memory_space=pltpu.SMEM),
                      pl.BlockSpec((DISPATCH_ROWS, D), lambda i, pe: (i, 0))],
            out_specs=pl.BlockSpec(memory_space=pl.ANY),
            scratch_shapes=[pltpu.VMEM((MOE_ROWS, D), h2d.dtype), pltpu.SemaphoreType.DMA(()),
                            pltpu.SemaphoreType.DMA(())]),
        out_shape=jax.ShapeDtypeStruct((rows, D), h2d.dtype),
        compiler_params=_cparams("arbitrary"),
        name="moe_dispatch",
    )(p_end, dest.reshape(nt, 1, TOP_K * DISPATCH_ROWS), h2d)


def _expert_kernel(be_ref, nu_ref, x_ref, wg_ref, wu_ref, wd_ref, o_ref):
    i = pl.program_id(0)

    @pl.when(i < nu_ref[0])
    def _():
        xb = _unpack_bf16_pairs(x_ref[...]).astype(BF16)
        hg = jnp.dot(xb, wg_ref[0], preferred_element_type=F32)
        hu = jnp.dot(xb, wu_ref[0], preferred_element_type=F32)
        hb = (hg * _sigmoid_tanh(hg) * hu).astype(BF16)
        o_ref[...] = _pack_bf16_pairs(jnp.dot(hb, wd_ref[0], preferred_element_type=F32))

    @pl.when(i >= nu_ref[0])
    def _():
        o_ref[...] = jnp.zeros_like(o_ref)


def moe_experts(xs, blk_exp, n_used, w_gate, w_up, w_down):
    rows, DP = xs.shape
    D = 2 * DP
    nblk = rows // MOE_ROWS
    F = EXPERT_FF
    return pl.pallas_call(
        _expert_kernel,
        grid_spec=pltpu.PrefetchScalarGridSpec(
            num_scalar_prefetch=2,
            grid=(nblk,),
            in_specs=[pl.BlockSpec((MOE_ROWS, DP), lambda i, be, nu: (jnp.minimum(i, nu[0] - 1), 0)),
                      pl.BlockSpec((1, D, F), lambda i, be, nu: (be[i], 0, 0)),
                      pl.BlockSpec((1, D, F), lambda i, be, nu: (be[i], 0, 0)),
                      pl.BlockSpec((1, F, D), lambda i, be, nu: (be[i], 0, 0))],
            out_specs=pl.BlockSpec((MOE_ROWS, DP), lambda i, be, nu: (i, 0))),
        out_shape=jax.ShapeDtypeStruct((rows, DP), xs.dtype),
        compiler_params=_cparams("arbitrary"),
        name="moe_experts",
    )(blk_exp, n_used, xs, w_gate, w_up, w_down)


def _combine_kernel(dest_ref, dnext_ref, x_ref, gate_ref, eo_ref, o_ref, buf_ref, sem):
    i = pl.program_id(0)
    nt = pl.num_programs(0)
    n = x_ref.shape[0]
    slot = i % 2

    def gather(d_ref, s):
        def issue(r, c):
            for k in range(TOP_K):
                _row_copy(eo_ref, d_ref[0, 0, TOP_K * r + k], buf_ref.at[s, k], r, sem.at[s]).start()
            return c

        lax.fori_loop(0, n, issue, 0, unroll=DMA_UNROLL)

    @pl.when(i == 0)
    def _():
        gather(dest_ref, 0)

    @pl.when(i + 1 < nt)
    def _():
        gather(dnext_ref, 1 - slot)

    pltpu.make_async_copy(buf_ref.at[slot], buf_ref.at[slot], sem.at[slot]).wait()
    gate = gate_ref[...]
    ffn = gate[:, 0:1] * _unpack_bf16_pairs(buf_ref[slot, 0])
    for k in range(1, TOP_K):
        ffn = ffn + gate[:, k:k + 1] * _unpack_bf16_pairs(buf_ref[slot, k])
    o_ref[...] = x_ref[...] + ffn


def moe_combine(x2d, gates, dest, eo):
    T, D = x2d.shape
    nt = T // DMA_ROWS
    dest3 = dest.reshape(nt, 1, TOP_K * DMA_ROWS)
    dspec = lambda f: pl.BlockSpec((1, 1, TOP_K * DMA_ROWS), f, memory_space=pltpu.SMEM)
    return pl.pallas_call(
        _combine_kernel,
        grid=(nt,),
        in_specs=[dspec(lambda i: (i, 0, 0)),
                  dspec(lambda i: (jnp.minimum(i + 1, nt - 1), 0, 0)),
                  pl.BlockSpec((DMA_ROWS, D), lambda i: (i, 0)),
                  pl.BlockSpec((DMA_ROWS, ROUTE_LANES), lambda i: (i, 0)),
                  pl.BlockSpec(memory_space=pl.ANY)],
        out_specs=pl.BlockSpec((DMA_ROWS, D), lambda i: (i, 0)),
        out_shape=jax.ShapeDtypeStruct((T, D), F32),
        scratch_shapes=[pltpu.VMEM((2, TOP_K, DMA_ROWS, eo.shape[1]), eo.dtype), pltpu.SemaphoreType.DMA((2,))],
        compiler_params=_cparams("arbitrary"),
        name="moe_combine",
    )(dest3, dest3, x2d, gates, eo)


def moe_layer(x, norm_g, wg1, bg1, wg2, bg2, w_gate, w_up, w_down):
    B, S, D = x.shape
    T = B * S
    A = T * TOP_K
    x2d = x.reshape(T, D)
    h, route_i, route_g, counts = moe_router(x2d, norm_g, wg1, bg1, wg2, bg2)
    counts = counts[0, ROUTE_OFF:ROUTE_OFF + N_EXPERTS]
    padded = (counts + MOE_ROWS - 1) // MOE_ROWS * MOE_ROWS
    p_end = jnp.cumsum(padded)
    p_start = p_end - padded
    sel = route_i[:, 0:TOP_K, None] == jnp.arange(N_EXPERTS, dtype=jnp.int32)
    dest = (jnp.sum(jnp.where(sel, p_start, 0), -1) + route_i[:, TOP_K:2 * TOP_K]).astype(jnp.int32)
    nblk = -(-A // MOE_ROWS) + N_EXPERTS
    blk_row = jnp.arange(nblk, dtype=jnp.int32) * MOE_ROWS
    blk_exp = jnp.minimum(jnp.sum(p_end[None, :] <= blk_row[:, None], -1), N_EXPERTS - 1).astype(jnp.int32)
    n_used = (p_end[-1:] // MOE_ROWS).astype(jnp.int32)
    xs = moe_dispatch(h, dest, p_end.astype(jnp.int32), nblk * MOE_ROWS)
    eo = moe_experts(xs, blk_exp, n_used, w_gate.astype(BF16), w_up.astype(BF16), w_down.astype(BF16))
    return moe_combine(x2d, route_g, dest, eo).reshape(B, S, D)


def _trunk(x, p):
    x = recurrent_layer(x, p["norm_mix"][0], p["rec_w_in"][0], p["rg_conv_w"][0], p["rg_conv_b"][0],
                        p["rg_wa"][0], p["rg_ba"][0], p["rg_wx"][0], p["rg_bx"][0], p["rg_lambda"][0],
                        p["rw_mu_l"][0], p["rw_mu_r"][0], p["rw_w0"][0], p["rw_w_up"][0], p["rw_a0"][0],
                        p["rw_a_up"][0], p["rw_g_up"][0], p["rw_k_k"][0], p["rw_k_a"][0], p["rw_r_k"][0],
                        p["rw_ln_w"][0], p["rw_ln_b"][0], p["rec_w_out"][0])
    x = moe_layer(x, p["norm_ffn"][0], p["moe_wg1"][0], p["moe_bg1"][0], p["moe_wg2"][0], p["moe_bg2"][0],
                  p["moe_w_gate"][0], p["moe_w_up"][0], p["moe_w_down"][0])
    x = attention_layer(x, p["norm_mix"][1], p["att_w_in"][0], p["att_q_norm"][0], p["att_k_norm"][0],
                        p["att_sink"][0], p["att_w_out"][0])
    x = moe_layer(x, p["norm_ffn"][1], p["moe_wg1"][1], p["moe_bg1"][1], p["moe_wg2"][1], p["moe_bg2"][1],
                  p["moe_w_gate"][1], p["moe_w_up"][1], p["moe_w_down"][1])
    return x


def kernel(x_prompt, x_sample, norm_mix, norm_ffn, rec_w_in, rg_conv_w, rg_conv_b, rg_wa, rg_ba, rg_wx, rg_bx,
           rg_lambda, rw_mu_l, rw_mu_r, rw_w0, rw_w_up, rw_a0, rw_a_up, rw_g_up, rw_k_k, rw_k_a, rw_r_k,
           rw_ln_w, rw_ln_b, rec_w_out, att_w_in, att_q_norm, att_k_norm, att_sink, att_w_out, moe_wg1, moe_bg1,
           moe_wg2, moe_bg2, moe_w_gate, moe_w_up, moe_w_down):
    p = dict(norm_mix=norm_mix, norm_ffn=norm_ffn, rec_w_in=rec_w_in, rg_conv_w=rg_conv_w, rg_conv_b=rg_conv_b,
             rg_wa=rg_wa, rg_ba=rg_ba, rg_wx=rg_wx, rg_bx=rg_bx, rg_lambda=rg_lambda, rw_mu_l=rw_mu_l,
             rw_mu_r=rw_mu_r, rw_w0=rw_w0, rw_w_up=rw_w_up, rw_a0=rw_a0, rw_a_up=rw_a_up, rw_g_up=rw_g_up,
             rw_k_k=rw_k_k, rw_k_a=rw_k_a, rw_r_k=rw_r_k, rw_ln_w=rw_ln_w, rw_ln_b=rw_ln_b, rec_w_out=rec_w_out,
             att_w_in=att_w_in, att_q_norm=att_q_norm, att_k_norm=att_k_norm, att_sink=att_sink,
             att_w_out=att_w_out, moe_wg1=moe_wg1, moe_bg1=moe_bg1, moe_wg2=moe_wg2, moe_bg2=moe_bg2,
             moe_w_gate=moe_w_gate, moe_w_up=moe_w_up, moe_w_down=moe_w_down)
    return (_trunk(x_prompt, p), _trunk(x_sample, p))
```

```python
import functools
import math

import jax
import jax.numpy as jnp
from jax import lax
from jax.experimental import pallas as pl
from jax.experimental.pallas import tpu as pltpu

F32 = jnp.float32
BF16 = jnp.bfloat16

D_MODEL = 1024
RG_WIDTH = 512
RG_BLOCKS = 8
RG_BLOCK_DIM = 64
CONV_WIDTH = 4
RG_C = 8.0
RW_HEADS = 8
RW_HEAD_DIM = 64
RW_WIDTH = 512
DECAY_LORA = 32
ICL_LORA = 32
GATE_LORA = 64
RW_IN = 3 * RW_WIDTH + DECAY_LORA + ICL_LORA + GATE_LORA
REC_IN = 2 * RG_WIDTH + RW_IN
RW_GN_EPS = 64e-5
ATT_HEADS = 16
ATT_KV_HEADS = 4
ATT_GROUP = 4
ATT_HEAD_DIM = 64
WINDOW = 128
ATT_BLOCK = 128
ATT_IN = (ATT_HEADS + 2 * ATT_KV_HEADS) * ATT_HEAD_DIM
N_GROUPS = 4
EXPERTS_PER_GROUP = 8
N_EXPERTS = 32
TOP_K = 2
EXPERT_FF = 512
RMS_EPS = 1e-6
NEG_INF = -1e30

LANES = 128
SUBLANES = 8
VMEM_LIMIT_BYTES = 56 * 1024 * 1024
ROW_TILE = 512
WKV_TILE = 256
WKV_CHUNK = 64
MOE_ROWS = 512
ROUTE_LANES = 128
DMA_ROWS = 256


def _cparams(*sem):
    return pltpu.CompilerParams(dimension_semantics=sem, vmem_limit_bytes=VMEM_LIMIT_BYTES)


def _dot(a, b):
    return jnp.dot(a.astype(BF16), b.astype(BF16), preferred_element_type=F32)


def _dot_nt(a, b):
    return lax.dot_general(a.astype(BF16), b.astype(BF16), (((1,), (1,)), ((), ())),
                           preferred_element_type=F32)


def _split2(x):
    h1 = x.astype(BF16)
    return h1, (x - h1.astype(F32)).astype(BF16)


def _dot_exact_rhs(x, e):
    h1, h2 = _split2(x)
    return jnp.dot(h1, e, preferred_element_type=F32) + jnp.dot(h2, e, preferred_element_type=F32)


def _dot_exact_lhs(e, x):
    h1, h2 = _split2(x)
    return jnp.dot(e, h1, preferred_element_type=F32) + jnp.dot(e, h2, preferred_element_type=F32)


def _sigmoid_tanh(x):
    return 0.5 * jnp.tanh(0.5 * x) + 0.5


def _pack_bf16_pairs(x):
    n = x.shape[1] // 2
    hi = lax.bitcast_convert_type(x[:, :n].astype(BF16).astype(F32), jnp.int32)
    lo = lax.bitcast_convert_type(x[:, n:].astype(BF16).astype(F32), jnp.int32)
    return hi | lax.shift_right_logical(lo, 16)


def _unpack_bf16_pairs(p):
    hi = lax.bitcast_convert_type(p & jnp.int32(-65536), F32)
    lo = lax.bitcast_convert_type(lax.shift_left(p, 16), F32)
    return jnp.concatenate([hi, lo], 1)


def _gelu_tanh(x):
    return 0.5 * x * (1.0 + jnp.tanh(math.sqrt(2.0 / math.pi) * (x + 0.044715 * (x * x * x))))


def _head_ones(width, head):
    r = lax.broadcasted_iota(jnp.int32, (width, width), 0) // head
    c = lax.broadcasted_iota(jnp.int32, (width, width), 1) // head
    return jnp.where(r == c, 1.0, 0.0).astype(BF16)


def _norm_proj_kernel(x_ref, g_ref, w_ref, *out_refs, splits):
    x = x_ref[...]
    h = x * lax.rsqrt(jnp.mean(x * x, -1, keepdims=True) + RMS_EPS) * g_ref[...]
    hb = h.astype(BF16)
    off = 0
    for o_ref, n in zip(out_refs, splits):
        o_ref[...] = jnp.dot(hb, w_ref[:, off:off + n], preferred_element_type=F32).astype(o_ref.dtype)
        off += n


def norm_proj(x2d, g, w, splits, out_dtype=F32):
    T, D = x2d.shape
    N = w.shape[1]
    assert sum(splits) == N and T % ROW_TILE == 0
    return pl.pallas_call(
        functools.partial(_norm_proj_kernel, splits=splits),
        grid=(T // ROW_TILE,),
        in_specs=[pl.BlockSpec((ROW_TILE, D), lambda i: (i, 0)),
                  pl.BlockSpec((1, D), lambda i: (0, 0)),
                  pl.BlockSpec((D, N), lambda i: (0, 0))],
        out_specs=[pl.BlockSpec((ROW_TILE, n), lambda i: (i, 0)) for n in splits],
        out_shape=[jax.ShapeDtypeStruct((T, n), out_dtype) for n in splits],
        compiler_params=_cparams("parallel"),
        name="norm_proj",
    )(x2d, g.reshape(1, D), w.astype(BF16))


RG_HALF = RG_WIDTH // 2
RG_ROWS = 256
RG_PAD = SUBLANES


def _rglru_kernel(x_ref, g_ref, cw_ref, cb_ref, wg_ref, bg_ref, sp_ref, o_ref,
                  xp_ref, af_ref, bf_ref, ab_ref, bb_ref):
    S = x_ref.shape[1]
    C = RG_HALF
    xp_ref[0:RG_PAD, :] = jnp.zeros((RG_PAD, C), F32)
    xp_ref[RG_PAD + S:RG_PAD + S + RG_PAD, :] = jnp.zeros((RG_PAD, C), F32)
    xp_ref[RG_PAD:RG_PAD + S, :] = x_ref[0]
    cw = cw_ref[...]
    left = CONV_WIDTH // 2
    for c in range(S // RG_ROWS):
        r0 = c * RG_ROWS
        xc = cb_ref[...] + cw[0:1] * xp_ref[RG_PAD + r0 - left:RG_PAD + r0 - left + RG_ROWS, :]
        for k in range(1, CONV_WIDTH):
            s0 = RG_PAD + r0 + k - left
            xc = xc + cw[k:k + 1] * xp_ref[s0:s0 + RG_ROWS, :]
        z = _dot(xc, wg_ref[0]) + bg_ref[0]
        for d, (a_ref, b_ref) in enumerate(((af_ref, bf_ref), (ab_ref, bb_ref))):
            r = _sigmoid_tanh(z[:, (2 * d) * C:(2 * d + 1) * C])
            i = _sigmoid_tanh(z[:, (2 * d + 1) * C:(2 * d + 2) * C])
            a = jnp.exp(-RG_C * r * sp_ref[0][:, d * C:(d + 1) * C])
            a_ref[r0:r0 + RG_ROWS, :] = a
            y = 1.0 - a * a
            b_ref[r0:r0 + RG_ROWS, :] = jnp.where(y > 0.0, y * lax.rsqrt(y), 0.0) * (i * xc)

    row8 = lax.broadcasted_iota(jnp.int32, (SUBLANES, C), 0)

    def tile_scan(a, b, carry, reverse):
        for s in (1, 2, 4):
            keep = (row8 < SUBLANES - s) if reverse else (row8 >= s)
            shift = SUBLANES - s if reverse else s
            b = b + a * jnp.where(keep, pltpu.roll(b, shift, 0), 0.0)
            a = a * jnp.where(keep, pltpu.roll(a, shift, 0), 1.0)
        h = b + a * carry
        last = 0 if reverse else SUBLANES - 1
        return h, h[last:last + 1]

    def body(n, carry):
        hf, hb = carry
        r0 = pl.multiple_of(n * SUBLANES, SUBLANES)
        h8, hf = tile_scan(af_ref[pl.ds(r0, SUBLANES), :], bf_ref[pl.ds(r0, SUBLANES), :], hf, False)
        bf_ref[pl.ds(r0, SUBLANES), :] = h8
        r1 = pl.multiple_of(S - SUBLANES - n * SUBLANES, SUBLANES)
        h8, hb = tile_scan(ab_ref[pl.ds(r1, SUBLANES), :], bb_ref[pl.ds(r1, SUBLANES), :], hb, True)
        bb_ref[pl.ds(r1, SUBLANES), :] = h8
        return hf, hb

    zero = jnp.zeros((1, C), F32)
    lax.fori_loop(0, S // SUBLANES, body, (zero, zero), unroll=2)
    for c in range(S // RG_ROWS):
        sl = slice(c * RG_ROWS, (c + 1) * RG_ROWS)
        o_ref[0, sl, :] = ((bf_ref[sl, :] + bb_ref[sl, :]) * _gelu_tanh(g_ref[0, sl, :])).astype(o_ref.dtype)


def rglru_branch(rg_x, rg_g, conv_w, conv_b, wa, ba, wx, bx, lam):
    B, S, _ = rg_x.shape
    C = RG_HALF
    nb = C // RG_BLOCK_DIM

    def bdiag(w):
        w = w.reshape(2, nb, RG_BLOCK_DIM, RG_BLOCK_DIM)
        eye = jnp.eye(nb, dtype=w.dtype)
        return jnp.einsum('hnij,nm->hnimj', w, eye).reshape(2, C, C)

    wg = jnp.concatenate([bdiag(wa[0]), bdiag(wx[0]), bdiag(wa[1]), bdiag(wx[1])], axis=-1).astype(BF16)

    def halves(v):
        return v.reshape(2, 1, C)

    bg = jnp.concatenate([halves(ba[0]), halves(bx[0]), halves(ba[1]), halves(bx[1])], axis=-1)
    sp = jax.nn.softplus(-lam.astype(F32))
    spg = jnp.concatenate([halves(sp[0]), halves(sp[1])], axis=-1)
    return pl.pallas_call(
        _rglru_kernel,
        grid=(B, 2),
        in_specs=[pl.BlockSpec((1, S, C), lambda b, c: (b, 0, c)),
                  pl.BlockSpec((1, S, C), lambda b, c: (b, 0, c)),
                  pl.BlockSpec((CONV_WIDTH, C), lambda b, c: (0, c)),
                  pl.BlockSpec((1, C), lambda b, c: (0, c)),
                  pl.BlockSpec((1, C, 4 * C), lambda b, c: (c, 0, 0)),
                  pl.BlockSpec((1, 1, 4 * C), lambda b, c: (c, 0, 0)),
                  pl.BlockSpec((1, 1, 2 * C), lambda b, c: (c, 0, 0))],
        out_specs=pl.BlockSpec((1, S, C), lambda b, c: (b, 0, c)),
        out_shape=jax.ShapeDtypeStruct((B, S, RG_WIDTH), BF16),
        scratch_shapes=[pltpu.VMEM((S + 2 * RG_PAD, C), F32)] + [pltpu.VMEM((S, C), F32)] * 4,
        compiler_params=_cparams("parallel", "parallel"),
        name="rglru",
    )(rg_x, rg_g, conv_w, conv_b.reshape(1, RG_WIDTH), wg, bg, spg)


def _rwkv_prep_kernel(u_ref, up_ref, un_ref, mul_ref, mur_ref, wl_ref, w0_ref, a0_ref, kk_ref, ka_ref,
                      rk_ref, rt_ref, kt_ref, bt_ref, kh_ref, v_ref, le_ref, bonus_ref, g_ref):
    i = pl.program_id(1)
    nt = pl.num_programs(1)
    TS = u_ref.shape[1]
    W = RW_WIDTH
    u = u_ref[0]
    prow = jnp.where(i == 0, 0.0, up_ref[0][SUBLANES - 1:SUBLANES, :])
    nrow = jnp.where(i == nt - 1, 0.0, un_ref[0][0:1, :])
    rows = lax.broadcasted_iota(jnp.int32, (TS, 1), 0)
    prev = jnp.where(rows == 0, prow, pltpu.roll(u, 1, 0))
    nxt = jnp.where(rows == TS - 1, nrow, pltpu.roll(u, TS - 1, 0))
    m = u + mul_ref[...] * (prev - u) + mur_ref[...] * (nxt - u)
    r = m[:, 0:W]
    k = m[:, W:2 * W]
    v = m[:, 2 * W:3 * W]
    tail = m[:, 3 * W:3 * W + LANES]
    lane = lax.broadcasted_iota(jnp.int32, (1, LANES), 1)
    z = jnp.where(lane < DECAY_LORA, jnp.tanh(tail),
                  jnp.where(lane < DECAY_LORA + ICL_LORA, tail, _sigmoid_tanh(tail)))
    lo = _dot(z, wl_ref[...])
    ones = _head_ones(W, RW_HEAD_DIM)
    kkr = k * kk_ref[...]
    kk = kkr * lax.rsqrt(jnp.maximum(_dot_exact_rhs(kkr * kkr, ones), 1e-24))
    tr = lax.broadcasted_iota(jnp.int32, (TS, TS), 0)
    tc = lax.broadcasted_iota(jnp.int32, (TS, TS), 1)
    same = (tr // WKV_CHUNK) == (tc // WKV_CHUNK)
    kd_sum = jnp.zeros((TS, W), F32)
    nch = TS // WKV_CHUNK
    for d in range(2):
        ld = -math.exp(-0.5) * _sigmoid_tanh(w0_ref[d:d + 1, :] + lo[:, d * W:(d + 1) * W])
        a = _sigmoid_tanh(a0_ref[d:d + 1, :] + lo[:, (2 + d) * W:(3 + d) * W])
        kd = k * (1.0 + (a - 1.0) * ka_ref[...])
        kd_sum = kd_sum + kd
        tri = jnp.where(same & ((tc <= tr) if d == 0 else (tc >= tr)), 1.0, 0.0).astype(BF16)
        L = _dot_exact_lhs(tri, ld)
        en = jnp.exp(-L)
        rt_ref[d, 0] = (r * jnp.exp(L)).astype(BF16)
        kt_ref[d, 0] = (kk * jnp.exp(L - ld)).astype(BF16)
        bt_ref[d, 0] = (kk * a * en).astype(BF16)
        kh_ref[d, 0] = (kd * en).astype(BF16)
        ends = [L[(c + 1) * WKV_CHUNK - 1:(c + 1) * WKV_CHUNK] if d == 0 else L[c * WKV_CHUNK:c * WKV_CHUNK + 1]
                for c in range(nch)]
        le_ref[d, 0, 0] = jnp.concatenate(ends, 0)
    v_ref[0] = v.astype(BF16)
    bonus_ref[0] =_dot_exact_rhs(r * kd_sum * rk_ref[...], ones) * v
    g_ref[0] = lo[:, 4 * W:5 * W]


def rwkv_prep(rw, mu_l, mu_r, w0, w_up, a0, a_up, g_up, k_k, k_a, r_k):
    B, S, _ = rw.shape
    TS = WKV_TILE
    W = RW_WIDTH
    nt = S // TS
    nch = TS // WKV_CHUNK
    hb = TS // SUBLANES
    wl = jnp.zeros((LANES, 5 * W), F32)
    wl = wl.at[0:DECAY_LORA, 0:W].set(w_up[0]).at[0:DECAY_LORA, W:2 * W].set(w_up[1])
    o = DECAY_LORA
    wl = wl.at[o:o + ICL_LORA, 2 * W:3 * W].set(a_up[0]).at[o:o + ICL_LORA, 3 * W:4 * W].set(a_up[1])
    o += ICL_LORA
    wl = wl.at[o:o + GATE_LORA, 4 * W:5 * W].set(g_up)
    vec = lambda n: pl.BlockSpec((1, n), lambda b, i: (0, 0))
    big = lambda: pl.BlockSpec((2, 1, TS, W), lambda b, i: (0, b, i, 0))
    one = lambda: pl.BlockSpec((1, TS, W), lambda b, i: (b, i, 0))
    return pl.pallas_call(
        _rwkv_prep_kernel,
        grid=(B, nt),
        in_specs=[pl.BlockSpec((1, TS, RW_IN), lambda b, i: (b, i, 0)),
                  pl.BlockSpec((1, SUBLANES, RW_IN), lambda b, i: (b, jnp.maximum(i * hb - 1, 0), 0)),
                  pl.BlockSpec((1, SUBLANES, RW_IN), lambda b, i: (b, jnp.minimum((i + 1) * hb, S // SUBLANES - 1), 0)),
                  vec(RW_IN), vec(RW_IN),
                  pl.BlockSpec((LANES, 5 * W), lambda b, i: (0, 0)),
                  pl.BlockSpec((2, W), lambda b, i: (0, 0)),
                  pl.BlockSpec((2, W), lambda b, i: (0, 0)),
                  vec(W), vec(W), vec(W)],
        out_specs=[big(), big(), big(), big(), one(),
                   pl.BlockSpec((2, 1, 1, nch, W), lambda b, i: (0, b, i, 0, 0)),
                   one(), one()],
        out_shape=[jax.ShapeDtypeStruct((2, B, S, W), BF16)] * 4
        + [jax.ShapeDtypeStruct((B, S, W), BF16),
           jax.ShapeDtypeStruct((2, B, nt, nch, W), F32),
           jax.ShapeDtypeStruct((B, S, W), F32),
           jax.ShapeDtypeStruct((B, S, W), F32)],
        compiler_params=_cparams("parallel", "parallel"),
        name="rwkv_prep",
    )(rw, rw, rw, mu_l.reshape(1, RW_IN), mu_r.reshape(1, RW_IN), wl.astype(BF16), w0, a0,
      k_k.reshape(1, W), k_a.reshape(1, W), r_k.reshape(1, W))


def _wkv_tiles(probs):
    TS = WKV_TILE
    C = WKV_CHUNK
    N = RW_HEAD_DIM
    P = 2 * N
    nch = TS // C
    zero = jnp.zeros((), BF16)
    head0 = lax.broadcasted_iota(jnp.int32, (1, P), 1) < N
    head0_2 = (lax.broadcasted_iota(jnp.int32, (1, 2 * P), 1) % P) < N
    head0_w = (lax.broadcasted_iota(jnp.int32, (1, nch * P), 1) % P) < N
    own = (lax.broadcasted_iota(jnp.int32, (TS, nch * P), 0) // C
           == lax.broadcasted_iota(jnp.int32, (TS, nch * P), 1) // P)
    tq = lax.broadcasted_iota(jnp.int32, (C, TS), 0)
    sq = lax.broadcasted_iota(jnp.int32, (C, TS), 1) % C
    blk = (lax.broadcasted_iota(jnp.int32, (TS, TS), 0) // C
           == lax.broadcasted_iota(jnp.int32, (TS, TS), 1) // C)
    hr = lax.broadcasted_iota(jnp.int32, (P, P), 0)
    hc = lax.broadcasted_iota(jnp.int32, (P, P), 1)
    bdiag = (hr // N) == (hc // N)
    heye = hr == hc

    def both_heads(xb, m):
        return jnp.concatenate([jnp.where(m, xb, zero), jnp.where(m, zero, xb)], 0)

    def wide(xb):
        return jnp.concatenate([xb[c * C:(c + 1) * C] for c in range(nch)], 1)

    def expand(xw):
        return jnp.where(blk, jnp.concatenate([xw] * nch, 0), zero)

    st = []
    for (rt, kt, bt, kh, v, le, H, reverse) in probs:
        ktb = kt.astype(BF16)
        vb = v.astype(BF16)
        ktw = wide(ktb)
        rtw = wide(rt.astype(BF16))
        lhs = jnp.concatenate([jnp.where(head0_w, ktw, zero), jnp.where(head0_w, rtw, zero),
                               jnp.where(head0_w, zero, ktw), jnp.where(head0_w, zero, rtw)], 0)
        rhs = jnp.concatenate([jnp.where(own, jnp.concatenate([bt.astype(BF16)] * nch, 1), zero),
                               jnp.where(own, jnp.concatenate([kh.astype(BF16)] * nch, 1), zero)], 0)
        gram = lax.dot_general(lhs, rhs, (((1,), (1,)), ((), ())), preferred_element_type=F32)
        st.append(dict(ktb=ktb, vb=vb, gram=gram, vm=both_heads(vb, head0)))

    chains = []
    for s, prob in zip(st, probs):
        reverse = prob[7]
        strict = (sq > tq) if reverse else (sq < tq)
        incl = (sq >= tq) if reverse else (sq <= tq)
        s["b_bd"], s["rb_bd"], s["rk_bd"] = [], [], []
        for h in range(2):
            g0 = s["gram"][2 * h * C:(2 * h + 1) * C]
            g1 = s["gram"][(2 * h + 1) * C:(2 * h + 2) * C]
            A = jnp.where(strict, g0[:, 0:TS], 0.0)
            s["b_bd"].append(expand(jnp.where(strict, g0[:, TS:2 * TS], 0.0).astype(BF16)))
            s["rb_bd"].append(expand(jnp.where(incl, g1[:, 0:TS], 0.0).astype(BF16)))
            s["rk_bd"].append(expand(jnp.where(incl, g1[:, TS:2 * TS], 0.0).astype(BF16)))
            chains.append(dict(Tw=jnp.where(sq == tq, 1.0, 0.0) - A, Ab=A.astype(BF16)))
    for ch in chains:
        ch["Q"] = jnp.dot(ch["Ab"], expand(ch["Ab"]), preferred_element_type=F32)
    for _ in range(int(math.log2(C)) - 2):
        for ch in chains:
            Qb = ch["Q"].astype(BF16)
            out = jnp.dot(jnp.concatenate([ch["Tw"].astype(BF16), Qb], 0), expand(Qb), preferred_element_type=F32)
            ch["Tw"] = ch["Tw"] + out[0:C]
            ch["Q"] = out[C:2 * C]
    for ch in chains:
        Tw = ch["Tw"] + jnp.dot(ch["Tw"].astype(BF16), expand(ch["Q"].astype(BF16)), preferred_element_type=F32)
        ch["t_bd"] = expand(Tw.astype(BF16))

    for n, s in enumerate(st):
        s["Bv"] = jnp.dot(jnp.concatenate(s["b_bd"], 1), s["vm"], preferred_element_type=F32)
    for n, s in enumerate(st):
        wm = both_heads(jnp.concatenate([s["ktb"], s["Bv"].astype(BF16)], 1), head0_2)
        t_bd = [chains[2 * n]["t_bd"], chains[2 * n + 1]["t_bd"]]
        s["x1"] = jnp.dot(jnp.concatenate(t_bd, 1), wm, preferred_element_type=F32)
    for s, prob in zip(st, probs):
        x1 = s["x1"]
        rhs2 = jnp.concatenate([both_heads(-x1.astype(BF16), head0_2),
                                jnp.concatenate([jnp.zeros((2 * TS, P), BF16), s["vm"]], 1)], 0)
        x2 = jnp.dot(jnp.concatenate(s["rb_bd"] + s["rk_bd"], 1), rhs2, preferred_element_type=F32)
        s["K2"] = x1[:, 0:P]
        s["V2"] = x1[:, P:2 * P]
        s["R2"] = prob[0] + x2[:, 0:P]
        s["Y2"] = x2[:, P:2 * P]
        s["H"] = prob[6]
        s["pc"] = jnp.exp(prob[5])
        s["ys"] = [None] * nch

    for ci in range(nch):
        for s, prob in zip(st, probs):
            (rt, kt, bt, kh, v, le, _, reverse) = prob
            c = nch - 1 - ci if reverse else ci
            sl = slice(c * C, (c + 1) * C)
            pc = s["pc"][c:c + 1]
            bh = (bt[sl] * pc).T
            khh = (kh[sl] * pc).T
            Mc = jnp.where(heye, pc, 0.0) - jnp.where(bdiag, _dot(bh, s["K2"][sl]), 0.0)
            Gc = jnp.where(bdiag, _dot(jnp.concatenate([khh, -bh], 1),
                                       jnp.concatenate([v[sl], s["V2"][sl]], 0)), 0.0)
            s["ys"][c] = _dot(s["R2"][sl], s["H"]) + s["Y2"][sl]
            s["H"] = _dot(Mc, s["H"]) + Gc
    return [(jnp.concatenate(s["ys"], 0), s["H"]) for s in st]


WKV_PAIRS = 4


def _wkv_kernel(rtf_ref, ktf_ref, btf_ref, khf_ref, vf_ref, lef_ref,
                rtb_ref, ktb_ref, btb_ref, khb_ref, vb_ref, leb_ref, yf_ref, yb_ref, h_ref):
    i = pl.program_id(2)

    @pl.when(i == 0)
    def _():
        h_ref[...] = jnp.zeros_like(h_ref)

    P = 2 * RW_HEAD_DIM
    dirs = ((rtf_ref, ktf_ref, btf_ref, khf_ref, vf_ref, lef_ref, yf_ref),
            (rtb_ref, ktb_ref, btb_ref, khb_ref, vb_ref, leb_ref, yb_ref))
    probs, outs = [], []
    for d, (rt_ref, kt_ref, bt_ref, kh_ref, v_ref, le_ref, y_ref) in enumerate(dirs):
        for pr in range(WKV_PAIRS):
            ln = slice(pr * P, (pr + 1) * P)
            probs.append((rt_ref[0, 0, :, ln], kt_ref[0, 0, :, ln], bt_ref[0, 0, :, ln], kh_ref[0, 0, :, ln],
                          v_ref[0, :, ln], le_ref[0, 0, 0, :, ln], h_ref[d, pr], d == 1))
            outs.append((y_ref, d, pr, ln))
    for (y, H), (y_ref, d, pr, ln) in zip(_wkv_tiles(probs), outs):
        y_ref[0, :, ln] = y
        h_ref[d, pr] = H


def wkv_scan(rt, kt, bt, kh, v, le):
    _, B, S, W = rt.shape
    TS = WKV_TILE
    nt = S // TS
    nch = TS // WKV_CHUNK
    P = 2 * RW_HEAD_DIM
    PW = WKV_PAIRS * P
    tiles = (lambda i: i, lambda i: nt - 1 - i)

    def dir_specs(d):
        t = tiles[d]
        big = lambda: pl.BlockSpec((1, 1, TS, PW), lambda b, p, i: (d, b, t(i), p))
        return [big(), big(), big(), big(),
                pl.BlockSpec((1, TS, PW), lambda b, p, i: (b, t(i), p)),
                pl.BlockSpec((1, 1, 1, nch, PW), lambda b, p, i: (d, b, t(i), 0, p))]

    return pl.pallas_call(
        _wkv_kernel,
        grid=(B, W // PW, nt),
        in_specs=dir_specs(0) + dir_specs(1),
        out_specs=[pl.BlockSpec((1, TS, PW), lambda b, p, i: (b, tiles[0](i), p)),
                   pl.BlockSpec((1, TS, PW), lambda b, p, i: (b, tiles[1](i), p))],
        out_shape=[jax.ShapeDtypeStruct((B, S, W), F32)] * 2,
        scratch_shapes=[pltpu.VMEM((2, WKV_PAIRS, P, P), F32)],
        compiler_params=_cparams("parallel", "parallel", "arbitrary"),
        name="wkv",
    )(rt, kt, bt, kh, v, le, rt, kt, bt, kh, v, le)


def _rec_out_kernel(x_ref, rg_ref, yf_ref, yb_ref, bonus_ref, g_ref, lnw_ref, lnb_ref, w_ref, o_ref):
    W = RW_WIDTH
    ones = _head_ones(W, RW_HEAD_DIM)
    y = yf_ref[...] + yb_ref[...]
    inv_n = 1.0 / RW_HEAD_DIM
    mu = _dot_exact_rhs(y, ones) * inv_n
    yc = y - mu
    var = _dot_exact_rhs(yc * yc, ones) * inv_n
    yn = yc * lax.rsqrt(var + RW_GN_EPS) * lnw_ref[...] + lnb_ref[...]
    rw_out = (yn + bonus_ref[...]) * g_ref[...]
    mix = (jnp.dot(rg_ref[...], w_ref[0:RG_WIDTH, :], preferred_element_type=F32)
           + jnp.dot(rw_out.astype(BF16), w_ref[RG_WIDTH:RG_WIDTH + W, :], preferred_element_type=F32))
    o_ref[...] = x_ref[...] + mix


def rec_out(x2d, rg_out, y_f, y_b, bonus, g, ln_w, ln_b, w_out):
    T, D = x2d.shape
    W = RW_WIDTH
    row = lambda n: pl.BlockSpec((ROW_TILE, n), lambda i: (i, 0))
    vec = lambda n: pl.BlockSpec((1, n), lambda i: (0, 0))
    return pl.pallas_call(
        _rec_out_kernel,
        grid=(T // ROW_TILE,),
        in_specs=[row(D), row(RG_WIDTH), row(W), row(W), row(W), row(W), vec(W), vec(W),
                  pl.BlockSpec((RG_WIDTH + W, D), lambda i: (0, 0))],
        out_specs=row(D),
        out_shape=jax.ShapeDtypeStruct((T, D), F32),
        compiler_params=_cparams("parallel"),
        name="rec_out",
    )(x2d, rg_out, y_f, y_b, bonus, g, ln_w.reshape(1, W), ln_b.reshape(1, W), w_out.astype(BF16))


def recurrent_layer(x, norm_g, w_in, conv_w, conv_b, rg_wa, rg_ba, rg_wx, rg_bx, rg_lambda,
                    mu_l, mu_r, w0, w_up, a0, a_up, g_up, k_k, k_a, r_k, ln_w, ln_b, w_out):
    B, S, D = x.shape
    T = B * S
    x2d = x.reshape(T, D)
    rg_x, rg_g, rw = norm_proj(x2d, norm_g, w_in, (RG_WIDTH, RG_WIDTH, RW_IN))
    rg_out = rglru_branch(rg_x.reshape(B, S, RG_WIDTH), rg_g.reshape(B, S, RG_WIDTH), conv_w, conv_b,
                          rg_wa, rg_ba, rg_wx, rg_bx, rg_lambda)
    rt, kt, bt, kh, v, le, bonus, g = rwkv_prep(rw.reshape(B, S, RW_IN), mu_l, mu_r, w0, w_up, a0, a_up,
                                                g_up, k_k, k_a, r_k.reshape(-1))
    y_f, y_b = wkv_scan(rt, kt, bt, kh, v, le)
    flat = lambda t: t.reshape(T, -1)
    out = rec_out(x2d, flat(rg_out), flat(y_f), flat(y_b), flat(bonus), flat(g), ln_w, ln_b, w_out)
    return out.reshape(B, S, D)


def _alibi_slope(h):
    return 2.0 ** (-8.0 * (h + 1) / ATT_HEADS)


ATT_PAIR = 2 * ATT_HEAD_DIM
ATT_KVW = ATT_KV_HEADS * ATT_PAIR


def _qkv_proj_kernel(x_ref, g_ref, w_ref, eq_ref, ek_ref, qg_ref, kg_ref, q_ref, k_ref, v_ref):
    QW = ATT_HEADS * ATT_HEAD_DIM
    x = x_ref[...]
    h = (x * lax.rsqrt(jnp.mean(x * x, -1, keepdims=True) + RMS_EPS) * g_ref[...]).astype(BF16)
    q = jnp.dot(h, w_ref[:, 0:QW], preferred_element_type=F32)
    msq = jnp.dot((q * q).astype(BF16), eq_ref[...], preferred_element_type=F32) * (1.0 / ATT_HEAD_DIM)
    q_ref[...] = (q * lax.rsqrt(msq + RMS_EPS) * qg_ref[...]).astype(BF16)
    k = jnp.dot(h, w_ref[:, QW:QW + ATT_KVW], preferred_element_type=F32)
    msk = jnp.dot((k * k).astype(BF16), ek_ref[...], preferred_element_type=F32) * (1.0 / ATT_PAIR)
    k_ref[...] = (k * lax.rsqrt(msk + RMS_EPS) * kg_ref[...]).astype(BF16)
    v_ref[...] = jnp.dot(h, w_ref[:, QW + ATT_KVW:QW + 2 * ATT_KVW], preferred_element_type=F32).astype(BF16)


def qkv_proj(x2d, g, w_in, q_norm, k_norm):
    T, D = x2d.shape
    QW = ATT_HEADS * ATT_HEAD_DIM
    KW = ATT_KV_HEADS * ATT_HEAD_DIM
    dup = lambda w: jnp.concatenate([w.reshape(D, ATT_KV_HEADS, 1, ATT_HEAD_DIM)] * 2, 2).reshape(D, ATT_KVW)
    w = jnp.concatenate([w_in[:, :QW], dup(w_in[:, QW:QW + KW]), dup(w_in[:, QW + KW:])], 1).astype(BF16)
    N = QW + 2 * ATT_KVW
    qg = jnp.tile(q_norm.astype(F32), ATT_HEADS).reshape(1, QW) * (ATT_HEAD_DIM ** -0.5)
    kg = jnp.tile(k_norm.astype(F32), 2 * ATT_KV_HEADS).reshape(1, ATT_KVW)
    row = lambda n: pl.BlockSpec((ROW_TILE, n), lambda i: (i, 0))
    fixed = lambda r, n: pl.BlockSpec((r, n), lambda i: (0, 0))
    return pl.pallas_call(
        _qkv_proj_kernel,
        grid=(T // ROW_TILE,),
        in_specs=[row(D), fixed(1, D), fixed(D, N), fixed(QW, QW), fixed(ATT_KVW, ATT_KVW),
                  fixed(1, QW), fixed(1, ATT_KVW)],
        out_specs=[row(QW), row(ATT_KVW), row(ATT_KVW)],
        out_shape=[jax.ShapeDtypeStruct((T, QW), BF16), jax.ShapeDtypeStruct((T, ATT_KVW), BF16),
                   jax.ShapeDtypeStruct((T, ATT_KVW), BF16)],
        compiler_params=_cparams("parallel"),
        name="qkv_proj",
    )(x2d, g.reshape(1, D), w, _head_ones(QW, ATT_HEAD_DIM), _head_ones(ATT_KVW, ATT_PAIR), qg, kg)


def _attn_kernel(sink_ref, x_ref, q_ref, kp_ref, kc_ref, kn_ref, vp_ref, vc_ref, vn_ref, bias_ref,
                 w_ref, o_ref):
    i = pl.program_id(1)
    nb = pl.num_programs(1)
    BLK = ATT_BLOCK
    P = ATT_PAIR
    span = 3 * BLK
    kc = jnp.concatenate([kp_ref[0], kc_ref[0], kn_ref[0]], 0)
    vc = jnp.concatenate([vp_ref[0], vc_ref[0], vn_ref[0]], 0)
    kpos = lax.broadcasted_iota(jnp.int32, (1, span), 1) + (i - 1) * BLK
    edge = jnp.where((kpos >= 0) & (kpos < nb * BLK), 0.0, NEG_INF)
    lane = lax.broadcasted_iota(jnp.int32, (1, P), 1)
    low = lane < ATT_HEAD_DIM
    zero = jnp.zeros((), BF16)
    ones = jnp.ones((span, P), BF16)
    slabs = []
    for g in range(ATT_KV_HEADS):
        kg = kc[:, g * P:(g + 1) * P]
        k_half = (jnp.where(low, kg, zero), jnp.where(low, zero, kg))
        v_ext = jnp.concatenate([vc[:, g * P:(g + 1) * P], ones], 1)
        for pr in range(ATT_GROUP // 2):
            slab = g * (ATT_GROUP // 2) + pr
            qp = q_ref[0, :, slab * P:(slab + 1) * P]
            halves = []
            for hf in range(2):
                h = 2 * slab + hf
                s = lax.dot_general(qp, k_half[hf], (((1,), (1,)), ((), ())), preferred_element_type=F32)
                s = s + bias_ref[h] + edge
                sk = sink_ref[h]
                m = jnp.maximum(jnp.max(s, -1, keepdims=True), sk)
                p = jnp.exp(s - m).astype(BF16)
                o = jnp.dot(p, v_ext, preferred_element_type=F32)
                halves.append(o[:, 0:P] / (o[:, P:2 * P] + jnp.exp(sk - m)))
            slabs.append(jnp.where(low, halves[0], halves[1]))
    o = jnp.concatenate(slabs, -1).astype(BF16)
    o_ref[0] = x_ref[0] + jnp.dot(o, w_ref[...], preferred_element_type=F32)


def attention_layer(x, norm_g, w_in, q_norm, k_norm, sink, w_out):
    B, S, D = x.shape
    T = B * S
    QW = ATT_HEADS * ATT_HEAD_DIM
    q, k, v = qkv_proj(x.reshape(T, D), norm_g, w_in, q_norm, k_norm)
    q = q.reshape(B, S, QW)
    k = k.reshape(B, S, ATT_KVW)
    v = v.reshape(B, S, ATT_KVW)
    nb = S // ATT_BLOCK
    span = 3 * ATT_BLOCK
    rel = (jnp.arange(span)[None, :] - WINDOW) - jnp.arange(ATT_BLOCK)[:, None]
    slopes = jnp.asarray([_alibi_slope(h) for h in range(ATT_HEADS)], F32)
    bias = jnp.where(jnp.abs(rel) <= WINDOW, -slopes[:, None, None] * jnp.abs(rel).astype(F32), NEG_INF)
    prev = lambda b, i: (b, jnp.maximum(i - 1, 0), 0)
    cur = lambda b, i: (b, i, 0)
    nxt = lambda b, i: (b, jnp.minimum(i + 1, nb - 1), 0)
    kv = lambda f: pl.BlockSpec((1, ATT_BLOCK, ATT_KVW), f)
    return pl.pallas_call(
        _attn_kernel,
        grid=(B, nb),
        in_specs=[pl.BlockSpec(memory_space=pltpu.SMEM),
                  pl.BlockSpec((1, ATT_BLOCK, D), cur),
                  pl.BlockSpec((1, ATT_BLOCK, QW), cur),
                  kv(prev), kv(cur), kv(nxt), kv(prev), kv(cur), kv(nxt),
                  pl.BlockSpec((ATT_HEADS, ATT_BLOCK, span), lambda b, i: (0, 0, 0)),
                  pl.BlockSpec((QW, D), lambda b, i: (0, 0))],
        out_specs=pl.BlockSpec((1, ATT_BLOCK, D), cur),
        out_shape=jax.ShapeDtypeStruct((B, S, D), F32),
        compiler_params=_cparams("parallel", "parallel"),
        name="window_attn",
    )(sink.astype(F32), x, q, k, k, k, v, v, v, bias, w_out.astype(BF16))


ROUTE_OFF = N_GROUPS


def _router_kernel(x_ref, g_ref, w1_ref, w2_ref, b_ref, h_ref, ri_ref, rg_ref, cnt_ref, run_ref):
    i = pl.program_id(0)

    @pl.when(i == 0)
    def _():
        run_ref[...] = jnp.zeros_like(run_ref)

    TM = x_ref.shape[0]
    x = x_ref[...]
    h = x * lax.rsqrt(jnp.mean(x * x, -1, keepdims=True) + RMS_EPS) * g_ref[...]
    _store_rows(h_ref, _pack_bf16_pairs(h))
    h1 = h.astype(BF16)
    h2 = (h - h1.astype(F32)).astype(BF16)
    lg = (jnp.dot(h1, w1_ref[...], preferred_element_type=F32) + jnp.dot(h1, w2_ref[...], preferred_element_type=F32)
          + jnp.dot(h2, w1_ref[...], preferred_element_type=F32)) + b_ref[...]
    lane = lax.broadcasted_iota(jnp.int32, (TM, ROUTE_LANES), 1)
    far = ROUTE_LANES
    gmask = lane < N_GROUPS
    gl = jnp.where(gmask, lg, NEG_INF)
    gm = jnp.max(gl, -1, keepdims=True)
    p_group = 1.0 / jnp.sum(jnp.where(gmask, jnp.exp(gl - gm), 0.0), -1, keepdims=True)
    group = jnp.min(jnp.where(gl == gm, lane, far), -1, keepdims=True)
    fmask = (lane >= ROUTE_OFF) & (lane < ROUTE_OFF + N_EXPERTS) & ((lane - ROUTE_OFF) // EXPERTS_PER_GROUP == group)
    fl = jnp.where(fmask, lg, NEG_INF)
    m1 = jnp.max(fl, -1, keepdims=True)
    ssum = jnp.sum(jnp.where(fmask, jnp.exp(fl - m1), 0.0), -1, keepdims=True)
    i1 = jnp.min(jnp.where(fl == m1, lane, far), -1, keepdims=True)
    fl2 = jnp.where(lane == i1, NEG_INF, fl)
    m2 = jnp.max(fl2, -1, keepdims=True)
    i2 = jnp.min(jnp.where(fl2 == m2, lane, far), -1, keepdims=True)
    p1 = 1.0 / ssum
    p2 = jnp.exp(m2 - m1) / ssum
    norm = p_group / (p1 + p2)
    oh = jnp.where((lane == i1) | (lane == i2), 1.0, 0.0)
    tr = lax.broadcasted_iota(jnp.int32, (TM, TM), 0)
    tc = lax.broadcasted_iota(jnp.int32, (TM, TM), 1)
    before = jnp.where(tc < tr, 1.0, 0.0).astype(BF16)
    pre = jnp.dot(before, oh.astype(BF16), preferred_element_type=F32) + run_ref[...]
    rank1 = jnp.sum(jnp.where(lane == i1, pre, 0.0), -1, keepdims=True)
    rank2 = jnp.sum(jnp.where(lane == i2, pre, 0.0), -1, keepdims=True)
    total = run_ref[...] + jnp.sum(oh, 0, keepdims=True)
    run_ref[...] = total
    cnt_ref[...] = total.astype(jnp.int32)
    ri_ref[...] = jnp.where(lane == 0, i1 - ROUTE_OFF,
                            jnp.where(lane == 1, i2 - ROUTE_OFF,
                                      jnp.where(lane == 2, rank1.astype(jnp.int32),
                                                jnp.where(lane == 3, rank2.astype(jnp.int32), 0))))
    rg_ref[...] = jnp.where(lane == 0, p1 * norm, jnp.where(lane == 1, p2 * norm, 0.0))


def moe_router(x2d, g, wg1, bg1, wg2, bg2):
    T, D = x2d.shape
    wr = jnp.zeros((D, ROUTE_LANES), F32)
    wr = wr.at[:, 0:N_GROUPS].set(wg1)
    wr = wr.at[:, ROUTE_OFF:ROUTE_OFF + N_EXPERTS].set(jnp.moveaxis(wg2, 0, 1).reshape(D, N_EXPERTS))
    w1 = wr.astype(BF16)
    w2 = (wr - w1.astype(F32)).astype(BF16)
    bias = jnp.zeros((1, ROUTE_LANES), F32)
    bias = bias.at[0, 0:N_GROUPS].set(bg1).at[0, ROUTE_OFF:ROUTE_OFF + N_EXPERTS].set(bg2.reshape(-1))
    row = lambda n: pl.BlockSpec((ROW_TILE, n), lambda i: (i, 0))
    fixed = lambda r, n: pl.BlockSpec((r, n), lambda i: (0, 0))
    return pl.pallas_call(
        _router_kernel,
        grid=(T // ROW_TILE,),
        in_specs=[row(D), fixed(1, D), fixed(D, ROUTE_LANES), fixed(D, ROUTE_LANES), fixed(1, ROUTE_LANES)],
        out_specs=[pl.BlockSpec((ROW_TILE * ROW_SLABS, LANES), lambda i: (i, 0)),
                   row(ROUTE_LANES), row(ROUTE_LANES), fixed(1, ROUTE_LANES)],
        out_shape=[jax.ShapeDtypeStruct((T * ROW_SLABS, LANES), jnp.int32),
                   jax.ShapeDtypeStruct((T, ROUTE_LANES), jnp.int32),
                   jax.ShapeDtypeStruct((T, ROUTE_LANES), F32),
                   jax.ShapeDtypeStruct((1, ROUTE_LANES), jnp.int32)],
        scratch_shapes=[pltpu.VMEM((1, ROUTE_LANES), F32)],
        compiler_params=_cparams("arbitrary"),
        name="moe_router",
    )(x2d, g.reshape(1, D), w1, w2, bias)


ROW_SLABS = (D_MODEL // 2) // LANES


def _store_rows(ref, packed):
    n = packed.shape[0]
    for j in range(ROW_SLABS):
        ref[pl.ds(j, n, stride=ROW_SLABS), :] = packed[:, j * LANES:(j + 1) * LANES]


def _load_rows(ref, n):
    return jnp.concatenate([ref[pl.ds(j, n, stride=ROW_SLABS), :] for j in range(ROW_SLABS)], 1)


def _row_copy(src, src_row, dst, dst_row, sem):
    return pltpu.make_async_copy(src.at[pl.ds(src_row * ROW_SLABS, ROW_SLABS)],
                                 dst.at[pl.ds(dst_row * ROW_SLABS, ROW_SLABS)], sem)


DMA_UNROLL = 8


def _rows_wait(ref, nrows, sem):
    pltpu.make_async_copy(ref.at[pl.ds(0, nrows * ROW_SLABS)], ref.at[pl.ds(0, nrows * ROW_SLABS)], sem).wait()


DISPATCH_ROWS = 512


def _dispatch_kernel(pend_ref, dest_ref, h_ref, xs_ref, zero_ref, sem, zsem):
    i = pl.program_id(0)
    n = h_ref.shape[0] // ROW_SLABS
    blk = MOE_ROWS * ROW_SLABS
    nblk = xs_ref.shape[0] // blk

    @pl.when(i == 0)
    def _():
        zero_ref[...] = jnp.zeros_like(zero_ref)

        def block_copy(row0):
            return pltpu.make_async_copy(zero_ref, xs_ref.at[pl.ds(pl.multiple_of(row0 * ROW_SLABS, blk), blk)], zsem)

        tails = [jnp.maximum(pend_ref[e] - MOE_ROWS, 0) for e in range(N_EXPERTS)]
        for t in tails:
            block_copy(t).start()
        for t in tails:
            block_copy(t).wait()
        first_unused = pend_ref[N_EXPERTS - 1] // MOE_ROWS

        def clear(j, c):
            block_copy(j * MOE_ROWS).start()
            block_copy(j * MOE_ROWS).wait()
            return c

        lax.fori_loop(first_unused, nblk, clear, 0)

    def issue(r, c):
        for k in range(TOP_K):
            _row_copy(h_ref, r, xs_ref, dest_ref[0, 0, TOP_K * r + k], sem).start()
        return c

    lax.fori_loop(0, n, issue, 0, unroll=DMA_UNROLL)
    _rows_wait(xs_ref, TOP_K * n, sem)


def moe_dispatch(h_rows, dest, p_end, rows):
    T = h_rows.shape[0] // ROW_SLABS
    nt = T // DISPATCH_ROWS
    return pl.pallas_call(
        _dispatch_kernel,
        grid_spec=pltpu.PrefetchScalarGridSpec(
            num_scalar_prefetch=1,
            grid=(nt,),
            in_specs=[pl.BlockSpec((1, 1, TOP_K * DISPATCH_ROWS), lambda i, pe: (i, 0, 0),
                                   memory_space=pltpu.SMEM),
                      pl.BlockSpec((DISPATCH_ROWS * ROW_SLABS, LANES), lambda i, pe: (i, 0))],
            out_specs=pl.BlockSpec(memory_space=pl.ANY),
            scratch_shapes=[pltpu.VMEM((MOE_ROWS * ROW_SLABS, LANES), h_rows.dtype), pltpu.SemaphoreType.DMA(()),
                            pltpu.SemaphoreType.DMA(())]),
        out_shape=jax.ShapeDtypeStruct((rows * ROW_SLABS, LANES), h_rows.dtype),
        compiler_params=_cparams("arbitrary"),
        name="moe_dispatch",
    )(p_end, dest.reshape(nt, 1, TOP_K * DISPATCH_ROWS), h_rows)


def _expert_kernel(be_ref, nu_ref, x_ref, wg_ref, wu_ref, wd_ref, o_ref):
    i = pl.program_id(0)

    @pl.when(i < nu_ref[0])
    def _():
        xb = _unpack_bf16_pairs(_load_rows(x_ref, MOE_ROWS)).astype(BF16)
        hg = jnp.dot(xb, wg_ref[0], preferred_element_type=F32)
        hu = jnp.dot(xb, wu_ref[0], preferred_element_type=F32)
        hb = (hg * _sigmoid_tanh(hg) * hu).astype(BF16)
        _store_rows(o_ref, _pack_bf16_pairs(jnp.dot(hb, wd_ref[0], preferred_element_type=F32)))

    @pl.when(i >= nu_ref[0])
    def _():
        o_ref[...] = jnp.zeros_like(o_ref)


def moe_experts(xs, blk_exp, n_used, w_gate, w_up, w_down):
    D = D_MODEL
    blk = MOE_ROWS * ROW_SLABS
    nblk = xs.shape[0] // blk
    F = EXPERT_FF
    return pl.pallas_call(
        _expert_kernel,
        grid_spec=pltpu.PrefetchScalarGridSpec(
            num_scalar_prefetch=2,
            grid=(nblk,),
            in_specs=[pl.BlockSpec((blk, LANES), lambda i, be, nu: (jnp.minimum(i, nu[0] - 1), 0)),
                      pl.BlockSpec((1, D, F), lambda i, be, nu: (be[i], 0, 0)),
                      pl.BlockSpec((1, D, F), lambda i, be, nu: (be[i], 0, 0)),
                      pl.BlockSpec((1, F, D), lambda i, be, nu: (be[i], 0, 0))],
            out_specs=pl.BlockSpec((blk, LANES), lambda i, be, nu: (i, 0))),
        out_shape=jax.ShapeDtypeStruct(xs.shape, xs.dtype),
        compiler_params=_cparams("arbitrary"),
        name="moe_experts",
    )(blk_exp, n_used, xs, w_gate, w_up, w_down)


def _combine_kernel(dest_ref, dnext_ref, x_ref, gate_ref, eo_ref, o_ref, buf_ref, sem):
    i = pl.program_id(0)
    nt = pl.num_programs(0)
    n = x_ref.shape[0]
    slot = i % 2

    def gather(d_ref, s):
        def issue(r, c):
            for k in range(TOP_K):
                _row_copy(eo_ref, d_ref[0, 0, TOP_K * r + k], buf_ref.at[s, k], r, sem.at[s]).start()
            return c

        lax.fori_loop(0, n, issue, 0, unroll=DMA_UNROLL)

    @pl.when(i == 0)
    def _():
        gather(dest_ref, 0)

    @pl.when(i + 1 < nt)
    def _():
        gather(dnext_ref, 1 - slot)

    pltpu.make_async_copy(buf_ref.at[slot], buf_ref.at[slot], sem.at[slot]).wait()
    gate = gate_ref[...]
    ffn = gate[:, 0:1] * _unpack_bf16_pairs(_load_rows(buf_ref.at[slot, 0], n))
    for k in range(1, TOP_K):
        ffn = ffn + gate[:, k:k + 1] * _unpack_bf16_pairs(_load_rows(buf_ref.at[slot, k], n))
    o_ref[...] = x_ref[...] + ffn


def moe_combine(x2d, gates, dest, eo):
    T, D = x2d.shape
    nt = T // DMA_ROWS
    dest3 = dest.reshape(nt, 1, TOP_K * DMA_ROWS)
    dspec = lambda f: pl.BlockSpec((1, 1, TOP_K * DMA_ROWS), f, memory_space=pltpu.SMEM)
    return pl.pallas_call(
        _combine_kernel,
        grid=(nt,),
        in_specs=[dspec(lambda i: (i, 0, 0)),
                  dspec(lambda i: (jnp.minimum(i + 1, nt - 1), 0, 0)),
                  pl.BlockSpec((DMA_ROWS, D), lambda i: (i, 0)),
                  pl.BlockSpec((DMA_ROWS, ROUTE_LANES), lambda i: (i, 0)),
                  pl.BlockSpec(memory_space=pl.ANY)],
        out_specs=pl.BlockSpec((DMA_ROWS, D), lambda i: (i, 0)),
        out_shape=jax.ShapeDtypeStruct((T, D), F32),
        scratch_shapes=[pltpu.VMEM((2, TOP_K, DMA_ROWS * ROW_SLABS, LANES), eo.dtype),
                        pltpu.SemaphoreType.DMA((2,))],
        compiler_params=_cparams("arbitrary"),
        name="moe_combine",
    )(dest3, dest3, x2d, gates, eo)


def moe_layer(x, norm_g, wg1, bg1, wg2, bg2, w_gate, w_up, w_down):
    B, S, D = x.shape
    T = B * S
    A = T * TOP_K
    x2d = x.reshape(T, D)
    h, route_i, route_g, counts = moe_router(x2d, norm_g, wg1, bg1, wg2, bg2)
    counts = counts[0, ROUTE_OFF:ROUTE_OFF + N_EXPERTS]
    padded = (counts + MOE_ROWS - 1) // MOE_ROWS * MOE_ROWS
    p_end = jnp.cumsum(padded)
    p_start = p_end - padded
    sel = route_i[:, 0:TOP_K, None] == jnp.arange(N_EXPERTS, dtype=jnp.int32)
    dest = (jnp.sum(jnp.where(sel, p_start, 0), -1) + route_i[:, TOP_K:2 * TOP_K]).astype(jnp.int32)
    nblk = -(-A // MOE_ROWS) + N_EXPERTS
    blk_row = jnp.arange(nblk, dtype=jnp.int32) * MOE_ROWS
    blk_exp = jnp.minimum(jnp.sum(p_end[None, :] <= blk_row[:, None], -1), N_EXPERTS - 1).astype(jnp.int32)
    n_used = (p_end[-1:] // MOE_ROWS).astype(jnp.int32)
    xs = moe_dispatch(h, dest, p_end.astype(jnp.int32), nblk * MOE_ROWS)
    eo = moe_experts(xs, blk_exp, n_used, w_gate.astype(BF16), w_up.astype(BF16), w_down.astype(BF16))
    return moe_combine(x2d, route_g, dest, eo).reshape(B, S, D)


def _trunk(x, p):
    x = recurrent_layer(x, p["norm_mix"][0], p["rec_w_in"][0], p["rg_conv_w"][0], p["rg_conv_b"][0],
                        p["rg_wa"][0], p["rg_ba"][0], p["rg_wx"][0], p["rg_bx"][0], p["rg_lambda"][0],
                        p["rw_mu_l"][0], p["rw_mu_r"][0], p["rw_w0"][0], p["rw_w_up"][0], p["rw_a0"][0],
                        p["rw_a_up"][0], p["rw_g_up"][0], p["rw_k_k"][0], p["rw_k_a"][0], p["rw_r_k"][0],
                        p["rw_ln_w"][0], p["rw_ln_b"][0], p["rec_w_out"][0])
    x = moe_layer(x, p["norm_ffn"][0], p["moe_wg1"][0], p["moe_bg1"][0], p["moe_wg2"][0], p["moe_bg2"][0],
                  p["moe_w_gate"][0], p["moe_w_up"][0], p["moe_w_down"][0])
    x = attention_layer(x, p["norm_mix"][1], p["att_w_in"][0], p["att_q_norm"][0], p["att_k_norm"][0],
                        p["att_sink"][0], p["att_w_out"][0])
    x = moe_layer(x, p["norm_ffn"][1], p["moe_wg1"][1], p["moe_bg1"][1], p["moe_wg2"][1], p["moe_bg2"][1],
                  p["moe_w_gate"][1], p["moe_w_up"][1], p["moe_w_down"][1])
    return x


def kernel(x_prompt, x_sample, norm_mix, norm_ffn, rec_w_in, rg_conv_w, rg_conv_b, rg_wa, rg_ba, rg_wx, rg_bx,
           rg_lambda, rw_mu_l, rw_mu_r, rw_w0, rw_w_up, rw_a0, rw_a_up, rw_g_up, rw_k_k, rw_k_a, rw_r_k,
           rw_ln_w, rw_ln_b, rec_w_out, att_w_in, att_q_norm, att_k_norm, att_sink, att_w_out, moe_wg1, moe_bg1,
           moe_wg2, moe_bg2, moe_w_gate, moe_w_up, moe_w_down):
    p = dict(norm_mix=norm_mix, norm_ffn=norm_ffn, rec_w_in=rec_w_in, rg_conv_w=rg_conv_w, rg_conv_b=rg_conv_b,
             rg_wa=rg_wa, rg_ba=rg_ba, rg_wx=rg_wx, rg_bx=rg_bx, rg_lambda=rg_lambda, rw_mu_l=rw_mu_l,
             rw_mu_r=rw_mu_r, rw_w0=rw_w0, rw_w_up=rw_w_up, rw_a0=rw_a0, rw_a_up=rw_a_up, rw_g_up=rw_g_up,
             rw_k_k=rw_k_k, rw_k_a=rw_k_a, rw_r_k=rw_r_k, rw_ln_w=rw_ln_w, rw_ln_b=rw_ln_b, rec_w_out=rec_w_out,
             att_w_in=att_w_in, att_q_norm=att_q_norm, att_k_norm=att_k_norm, att_sink=att_sink,
             att_w_out=att_w_out, moe_wg1=moe_wg1, moe_bg1=moe_bg1, moe_wg2=moe_wg2, moe_bg2=moe_bg2,
             moe_w_gate=moe_w_gate, moe_w_up=moe_w_up, moe_w_down=moe_w_down)
    return (_trunk(x_prompt, p), _trunk(x_sample, p))
```

```python
import functools
import math

import jax
import jax.numpy as jnp
from jax import lax
from jax.experimental import pallas as pl
from jax.experimental.pallas import tpu as pltpu

F32 = jnp.float32
BF16 = jnp.bfloat16

D_MODEL = 1024
RG_WIDTH = 512
RG_BLOCKS = 8
RG_BLOCK_DIM = 64
CONV_WIDTH = 4
RG_C = 8.0
RW_HEADS = 8
RW_HEAD_DIM = 64
RW_WIDTH = 512
DECAY_LORA = 32
ICL_LORA = 32
GATE_LORA = 64
RW_IN = 3 * RW_WIDTH + DECAY_LORA + ICL_LORA + GATE_LORA
REC_IN = 2 * RG_WIDTH + RW_IN
RW_GN_EPS = 64e-5
ATT_HEADS = 16
ATT_KV_HEADS = 4
ATT_GROUP = 4
ATT_HEAD_DIM = 64
WINDOW = 128
ATT_BLOCK = 128
ATT_IN = (ATT_HEADS + 2 * ATT_KV_HEADS) * ATT_HEAD_DIM
N_GROUPS = 4
EXPERTS_PER_GROUP = 8
N_EXPERTS = 32
TOP_K = 2
EXPERT_FF = 512
RMS_EPS = 1e-6
NEG_INF = -1e30

LANES = 128
SUBLANES = 8
VMEM_LIMIT_BYTES = 56 * 1024 * 1024
ROW_TILE = 512
WKV_TILE = 256
WKV_CHUNK = 64
MOE_ROWS = 512
ROUTE_LANES = 128
DMA_ROWS = ROW_TILE


def _cparams(*sem):
    return pltpu.CompilerParams(dimension_semantics=sem, vmem_limit_bytes=VMEM_LIMIT_BYTES)


def _dot(a, b):
    return jnp.dot(a.astype(BF16), b.astype(BF16), preferred_element_type=F32)


def _dot_nt(a, b):
    return lax.dot_general(a.astype(BF16), b.astype(BF16), (((1,), (1,)), ((), ())),
                           preferred_element_type=F32)


def _split2(x):
    h1 = x.astype(BF16)
    return h1, (x - h1.astype(F32)).astype(BF16)


def _dot_exact_lhs(e, x):
    h1, h2 = _split2(x)
    return jnp.dot(e, h1, preferred_element_type=F32) + jnp.dot(e, h2, preferred_element_type=F32)


def _sigmoid_tanh(x):
    return 0.5 * jnp.tanh(0.5 * x) + 0.5


def _pack_bf16_pairs(x):
    n = x.shape[1] // 2
    hi = lax.bitcast_convert_type(x[:, :n].astype(BF16).astype(F32), jnp.int32)
    lo = lax.bitcast_convert_type(x[:, n:].astype(BF16).astype(F32), jnp.int32)
    return hi | lax.shift_right_logical(lo, 16)


def _unpack_bf16_pairs(p):
    hi = lax.bitcast_convert_type(p & jnp.int32(-65536), F32)
    lo = lax.bitcast_convert_type(lax.shift_left(p, 16), F32)
    return jnp.concatenate([hi, lo], 1)


def _gelu_tanh(x):
    return 0.5 * x * (1.0 + jnp.tanh(math.sqrt(2.0 / math.pi) * (x + 0.044715 * (x * x * x))))


def _head_ones(width, head):
    r = lax.broadcasted_iota(jnp.int32, (width, width), 0) // head
    c = lax.broadcasted_iota(jnp.int32, (width, width), 1) // head
    return jnp.where(r == c, 1.0, 0.0).astype(BF16)


def _norm_proj_kernel(x_ref, g_ref, w_ref, *out_refs, splits):
    x = x_ref[...]
    h = x * lax.rsqrt(jnp.mean(x * x, -1, keepdims=True) + RMS_EPS) * g_ref[...]
    hb = h.astype(BF16)
    off = 0
    for o_ref, n in zip(out_refs, splits):
        o_ref[...] = jnp.dot(hb, w_ref[:, off:off + n], preferred_element_type=F32).astype(o_ref.dtype)
        off += n


def norm_proj(x2d, g, w, splits, out_dtype=F32):
    T, D = x2d.shape
    N = w.shape[1]
    assert sum(splits) == N and T % ROW_TILE == 0
    return pl.pallas_call(
        functools.partial(_norm_proj_kernel, splits=splits),
        grid=(T // ROW_TILE,),
        in_specs=[pl.BlockSpec((ROW_TILE, D), lambda i: (i, 0)),
                  pl.BlockSpec((1, D), lambda i: (0, 0)),
                  pl.BlockSpec((D, N), lambda i: (0, 0))],
        out_specs=[pl.BlockSpec((ROW_TILE, n), lambda i: (i, 0)) for n in splits],
        out_shape=[jax.ShapeDtypeStruct((T, n), out_dtype) for n in splits],
        compiler_params=_cparams("parallel"),
        name="norm_proj",
    )(x2d, g.reshape(1, D), w.astype(BF16))


RG_HALF = RG_WIDTH // 2
RG_ROWS = 256
RG_PAD = SUBLANES


def _rglru_kernel(x_ref, g_ref, cw_ref, cb_ref, wg_ref, bg_ref, sp_ref, o_ref,
                  xp_ref, af_ref, bf_ref, ab_ref, bb_ref):
    S = x_ref.shape[1]
    C = RG_HALF
    xp_ref[0:RG_PAD, :] = jnp.zeros((RG_PAD, C), F32)
    xp_ref[RG_PAD + S:RG_PAD + S + RG_PAD, :] = jnp.zeros((RG_PAD, C), F32)
    xp_ref[RG_PAD:RG_PAD + S, :] = x_ref[0]
    cw = cw_ref[...]
    left = CONV_WIDTH // 2
    for c in range(S // RG_ROWS):
        r0 = c * RG_ROWS
        xc = cb_ref[...] + cw[0:1] * xp_ref[RG_PAD + r0 - left:RG_PAD + r0 - left + RG_ROWS, :]
        for k in range(1, CONV_WIDTH):
            s0 = RG_PAD + r0 + k - left
            xc = xc + cw[k:k + 1] * xp_ref[s0:s0 + RG_ROWS, :]
        z = _dot(xc, wg_ref[0]) + bg_ref[0]
        for d, (a_ref, b_ref) in enumerate(((af_ref, bf_ref), (ab_ref, bb_ref))):
            r = _sigmoid_tanh(z[:, (2 * d) * C:(2 * d + 1) * C])
            i = _sigmoid_tanh(z[:, (2 * d + 1) * C:(2 * d + 2) * C])
            a = jnp.exp(-RG_C * r * sp_ref[0][:, d * C:(d + 1) * C])
            a_ref[r0:r0 + RG_ROWS, :] = a
            y = 1.0 - a * a
            b_ref[r0:r0 + RG_ROWS, :] = jnp.where(y > 0.0, y * lax.rsqrt(y), 0.0) * (i * xc)

    row8 = lax.broadcasted_iota(jnp.int32, (SUBLANES, C), 0)

    def tile_scan(a, b, carry, reverse):
        for s in (1, 2, 4):
            keep = (row8 < SUBLANES - s) if reverse else (row8 >= s)
            shift = SUBLANES - s if reverse else s
            b = b + a * jnp.where(keep, pltpu.roll(b, shift, 0), 0.0)
            a = a * jnp.where(keep, pltpu.roll(a, shift, 0), 1.0)
        h = b + a * carry
        last = 0 if reverse else SUBLANES - 1
        return h, h[last:last + 1]

    def body(n, carry):
        hf, hb = carry
        r0 = pl.multiple_of(n * SUBLANES, SUBLANES)
        h8, hf = tile_scan(af_ref[pl.ds(r0, SUBLANES), :], bf_ref[pl.ds(r0, SUBLANES), :], hf, False)
        bf_ref[pl.ds(r0, SUBLANES), :] = h8
        r1 = pl.multiple_of(S - SUBLANES - n * SUBLANES, SUBLANES)
        h8, hb = tile_scan(ab_ref[pl.ds(r1, SUBLANES), :], bb_ref[pl.ds(r1, SUBLANES), :], hb, True)
        bb_ref[pl.ds(r1, SUBLANES), :] = h8
        return hf, hb

    zero = jnp.zeros((1, C), F32)
    lax.fori_loop(0, S // SUBLANES, body, (zero, zero), unroll=2)
    for c in range(S // RG_ROWS):
        sl = slice(c * RG_ROWS, (c + 1) * RG_ROWS)
        o_ref[0, sl, :] = ((bf_ref[sl, :] + bb_ref[sl, :]) * _gelu_tanh(g_ref[0, sl, :])).astype(o_ref.dtype)


def rglru_branch(rg_x, rg_g, conv_w, conv_b, wa, ba, wx, bx, lam):
    B, S, _ = rg_x.shape
    C = RG_HALF
    nb = C // RG_BLOCK_DIM

    def bdiag(w):
        w = w.reshape(2, nb, RG_BLOCK_DIM, RG_BLOCK_DIM)
        eye = jnp.eye(nb, dtype=w.dtype)
        return jnp.einsum('hnij,nm->hnimj', w, eye).reshape(2, C, C)

    wg = jnp.concatenate([bdiag(wa[0]), bdiag(wx[0]), bdiag(wa[1]), bdiag(wx[1])], axis=-1).astype(BF16)

    def halves(v):
        return v.reshape(2, 1, C)

    bg = jnp.concatenate([halves(ba[0]), halves(bx[0]), halves(ba[1]), halves(bx[1])], axis=-1)
    sp = jax.nn.softplus(-lam.astype(F32))
    spg = jnp.concatenate([halves(sp[0]), halves(sp[1])], axis=-1)
    return pl.pallas_call(
        _rglru_kernel,
        grid=(B, 2),
        in_specs=[pl.BlockSpec((1, S, C), lambda b, c: (b, 0, c)),
                  pl.BlockSpec((1, S, C), lambda b, c: (b, 0, c)),
                  pl.BlockSpec((CONV_WIDTH, C), lambda b, c: (0, c)),
                  pl.BlockSpec((1, C), lambda b, c: (0, c)),
                  pl.BlockSpec((1, C, 4 * C), lambda b, c: (c, 0, 0)),
                  pl.BlockSpec((1, 1, 4 * C), lambda b, c: (c, 0, 0)),
                  pl.BlockSpec((1, 1, 2 * C), lambda b, c: (c, 0, 0))],
        out_specs=pl.BlockSpec((1, S, C), lambda b, c: (b, 0, c)),
        out_shape=jax.ShapeDtypeStruct((B, S, RG_WIDTH), BF16),
        scratch_shapes=[pltpu.VMEM((S + 2 * RG_PAD, C), F32)] + [pltpu.VMEM((S, C), F32)] * 4,
        compiler_params=_cparams("parallel", "parallel"),
        name="rglru",
    )(rg_x, rg_g, conv_w, conv_b.reshape(1, RG_WIDTH), wg, bg, spg)


def _rwkv_prep_kernel(u_ref, up_ref, un_ref, mul_ref, mur_ref, wl_ref, w0_ref, a0_ref, kk_ref, ka_ref,
                      rk_ref, rt_ref, kt_ref, bt_ref, kh_ref, v_ref, le_ref, bonus_ref, g_ref):
    i = pl.program_id(1)
    nt = pl.num_programs(1)
    TS = u_ref.shape[1]
    W = RW_WIDTH
    u = u_ref[0]
    prow = jnp.where(i == 0, 0.0, up_ref[0][SUBLANES - 1:SUBLANES, :])
    nrow = jnp.where(i == nt - 1, 0.0, un_ref[0][0:1, :])
    rows = lax.broadcasted_iota(jnp.int32, (TS, 1), 0)
    prev = jnp.where(rows == 0, prow, pltpu.roll(u, 1, 0))
    nxt = jnp.where(rows == TS - 1, nrow, pltpu.roll(u, TS - 1, 0))
    m = u + mul_ref[...] * (prev - u) + mur_ref[...] * (nxt - u)
    r = m[:, 0:W]
    k = m[:, W:2 * W]
    v = m[:, 2 * W:3 * W]
    tail = m[:, 3 * W:3 * W + LANES]
    lane = lax.broadcasted_iota(jnp.int32, (1, LANES), 1)
    z = jnp.where(lane < DECAY_LORA, jnp.tanh(tail),
                  jnp.where(lane < DECAY_LORA + ICL_LORA, tail, _sigmoid_tanh(tail)))
    lo = _dot(z, wl_ref[...])
    ones = _head_ones(W, RW_HEAD_DIM)
    kkr = k * kk_ref[...]
    kk = kkr * lax.rsqrt(jnp.maximum(_dot(kkr * kkr, ones), 1e-24))
    tr = lax.broadcasted_iota(jnp.int32, (TS, TS), 0)
    tc = lax.broadcasted_iota(jnp.int32, (TS, TS), 1)
    same = (tr // WKV_CHUNK) == (tc // WKV_CHUNK)
    kd_sum = jnp.zeros((TS, W), F32)
    nch = TS // WKV_CHUNK
    for d in range(2):
        ld = -math.exp(-0.5) * _sigmoid_tanh(w0_ref[d:d + 1, :] + lo[:, d * W:(d + 1) * W])
        a = _sigmoid_tanh(a0_ref[d:d + 1, :] + lo[:, (2 + d) * W:(3 + d) * W])
        kd = k * (1.0 + (a - 1.0) * ka_ref[...])
        kd_sum = kd_sum + kd
        tri = jnp.where(same & ((tc <= tr) if d == 0 else (tc >= tr)), 1.0, 0.0).astype(BF16)
        L = _dot_exact_lhs(tri, ld)
        en = jnp.exp(-L)
        rt_ref[d, 0] = (r * jnp.exp(L)).astype(BF16)
        kt_ref[d, 0] = (kk * jnp.exp(L - ld)).astype(BF16)
        bt_ref[d, 0] = (kk * a * en).astype(BF16)
        kh_ref[d, 0] = (kd * en).astype(BF16)
        ends = [L[(c + 1) * WKV_CHUNK - 1:(c + 1) * WKV_CHUNK] if d == 0 else L[c * WKV_CHUNK:c * WKV_CHUNK + 1]
                for c in range(nch)]
        le_ref[d, 0, 0] = jnp.concatenate(ends, 0)
    v_ref[0] = v.astype(BF16)
    bonus_ref[0] = _dot(r * kd_sum * rk_ref[...], ones) * v
    g_ref[0] = lo[:, 4 * W:5 * W]


def rwkv_prep(rw, mu_l, mu_r, w0, w_up, a0, a_up, g_up, k_k, k_a, r_k):
    B, S, _ = rw.shape
    TS = WKV_TILE
    W = RW_WIDTH
    nt = S // TS
    nch = TS // WKV_CHUNK
    hb = TS // SUBLANES
    wl = jnp.zeros((LANES, 5 * W), F32)
    wl = wl.at[0:DECAY_LORA, 0:W].set(w_up[0]).at[0:DECAY_LORA, W:2 * W].set(w_up[1])
    o = DECAY_LORA
    wl = wl.at[o:o + ICL_LORA, 2 * W:3 * W].set(a_up[0]).at[o:o + ICL_LORA, 3 * W:4 * W].set(a_up[1])
    o += ICL_LORA
    wl = wl.at[o:o + GATE_LORA, 4 * W:5 * W].set(g_up)
    vec = lambda n: pl.BlockSpec((1, n), lambda b, i: (0, 0))
    big = lambda: pl.BlockSpec((2, 1, TS, W), lambda b, i: (0, b, i, 0))
    one = lambda: pl.BlockSpec((1, TS, W), lambda b, i: (b, i, 0))
    return pl.pallas_call(
        _rwkv_prep_kernel,
        grid=(B, nt),
        in_specs=[pl.BlockSpec((1, TS, RW_IN), lambda b, i: (b, i, 0)),
                  pl.BlockSpec((1, SUBLANES, RW_IN), lambda b, i: (b, jnp.maximum(i * hb - 1, 0), 0)),
                  pl.BlockSpec((1, SUBLANES, RW_IN), lambda b, i: (b, jnp.minimum((i + 1) * hb, S // SUBLANES - 1), 0)),
                  vec(RW_IN), vec(RW_IN),
                  pl.BlockSpec((LANES, 5 * W), lambda b, i: (0, 0)),
                  pl.BlockSpec((2, W), lambda b, i: (0, 0)),
                  pl.BlockSpec((2, W), lambda b, i: (0, 0)),
                  vec(W), vec(W), vec(W)],
        out_specs=[big(), big(), big(), big(), one(),
                   pl.BlockSpec((2, 1, 1, nch, W), lambda b, i: (0, b, i, 0, 0)),
                   one(), one()],
        out_shape=[jax.ShapeDtypeStruct((2, B, S, W), BF16)] * 4
        + [jax.ShapeDtypeStruct((B, S, W), BF16),
           jax.ShapeDtypeStruct((2, B, nt, nch, W), F32),
           jax.ShapeDtypeStruct((B, S, W), F32),
           jax.ShapeDtypeStruct((B, S, W), F32)],
        compiler_params=_cparams("parallel", "parallel"),
        name="rwkv_prep",
    )(rw, rw, rw, mu_l.reshape(1, RW_IN), mu_r.reshape(1, RW_IN), wl.astype(BF16), w0, a0,
      k_k.reshape(1, W), k_a.reshape(1, W), r_k.reshape(1, W))


def _wkv_tiles(probs):
    TS = WKV_TILE
    C = WKV_CHUNK
    N = RW_HEAD_DIM
    P = 2 * N
    nch = TS // C
    zero = jnp.zeros((), BF16)
    head0 = lax.broadcasted_iota(jnp.int32, (1, P), 1) < N
    head0_2 = (lax.broadcasted_iota(jnp.int32, (1, 2 * P), 1) % P) < N
    head0_w = (lax.broadcasted_iota(jnp.int32, (1, nch * P), 1) % P) < N
    own = (lax.broadcasted_iota(jnp.int32, (TS, nch * P), 0) // C
           == lax.broadcasted_iota(jnp.int32, (TS, nch * P), 1) // P)
    tq = lax.broadcasted_iota(jnp.int32, (C, TS), 0)
    sq = lax.broadcasted_iota(jnp.int32, (C, TS), 1) % C
    blk = (lax.broadcasted_iota(jnp.int32, (TS, TS), 0) // C
           == lax.broadcasted_iota(jnp.int32, (TS, TS), 1) // C)
    hr = lax.broadcasted_iota(jnp.int32, (P, P), 0)
    hc = lax.broadcasted_iota(jnp.int32, (P, P), 1)
    bdiag = (hr // N) == (hc // N)
    heye = hr == hc

    def both_heads(xb, m):
        return jnp.concatenate([jnp.where(m, xb, zero), jnp.where(m, zero, xb)], 0)

    def wide(xb):
        return jnp.concatenate([xb[c * C:(c + 1) * C] for c in range(nch)], 1)

    def expand(xw):
        return jnp.where(blk, jnp.concatenate([xw] * nch, 0), zero)

    st = []
    for (rt, kt, bt, kh, v, le, H, reverse) in probs:
        ktb = kt.astype(BF16)
        vb = v.astype(BF16)
        ktw = wide(ktb)
        rtw = wide(rt.astype(BF16))
        lhs = jnp.concatenate([jnp.where(head0_w, ktw, zero), jnp.where(head0_w, rtw, zero),
                               jnp.where(head0_w, zero, ktw), jnp.where(head0_w, zero, rtw)], 0)
        rhs = jnp.concatenate([jnp.where(own, jnp.concatenate([bt.astype(BF16)] * nch, 1), zero),
                               jnp.where(own, jnp.concatenate([kh.astype(BF16)] * nch, 1), zero)], 0)
        gram = lax.dot_general(lhs, rhs, (((1,), (1,)), ((), ())), preferred_element_type=F32)
        st.append(dict(ktb=ktb, vb=vb, gram=gram, vm=both_heads(vb, head0)))

    chains = []
    for s, prob in zip(st, probs):
        reverse = prob[7]
        strict = (sq > tq) if reverse else (sq < tq)
        incl = (sq >= tq) if reverse else (sq <= tq)
        s["b_bd"], s["rb_bd"], s["rk_bd"] = [], [], []
        for h in range(2):
            g0 = s["gram"][2 * h * C:(2 * h + 1) * C]
            g1 = s["gram"][(2 * h + 1) * C:(2 * h + 2) * C]
            A = jnp.where(strict, g0[:, 0:TS], 0.0)
            s["b_bd"].append(expand(jnp.where(strict, g0[:, TS:2 * TS], 0.0).astype(BF16)))
            s["rb_bd"].append(expand(jnp.where(incl, g1[:, 0:TS], 0.0).astype(BF16)))
            s["rk_bd"].append(expand(jnp.where(incl, g1[:, TS:2 * TS], 0.0).astype(BF16)))
            chains.append(dict(Tw=jnp.where(sq == tq, 1.0, 0.0) - A, Ab=A.astype(BF16)))
    for ch in chains:
        ch["Q"] = jnp.dot(ch["Ab"], expand(ch["Ab"]), preferred_element_type=F32)
    for _ in range(int(math.log2(C)) - 2):
        for ch in chains:
            Qb = ch["Q"].astype(BF16)
            out = jnp.dot(jnp.concatenate([ch["Tw"].astype(BF16), Qb], 0), expand(Qb), preferred_element_type=F32)
            ch["Tw"] = ch["Tw"] + out[0:C]
            ch["Q"] = out[C:2 * C]
    for ch in chains:
        Tw = ch["Tw"] + jnp.dot(ch["Tw"].astype(BF16), expand(ch["Q"].astype(BF16)), preferred_element_type=F32)
        ch["t_bd"] = expand(Tw.astype(BF16))

    for n, s in enumerate(st):
        s["Bv"] = jnp.dot(jnp.concatenate(s["b_bd"], 1), s["vm"], preferred_element_type=F32)
    for n, s in enumerate(st):
        wm = both_heads(jnp.concatenate([s["ktb"], s["Bv"].astype(BF16)], 1), head0_2)
        t_bd = [chains[2 * n]["t_bd"], chains[2 * n + 1]["t_bd"]]
        s["x1"] = jnp.dot(jnp.concatenate(t_bd, 1), wm, preferred_element_type=F32)
    for s, prob in zip(st, probs):
        x1 = s["x1"]
        rhs2 = jnp.concatenate([both_heads(-x1.astype(BF16), head0_2),
                                jnp.concatenate([jnp.zeros((2 * TS, P), BF16), s["vm"]], 1)], 0)
        x2 = jnp.dot(jnp.concatenate(s["rb_bd"] + s["rk_bd"], 1), rhs2, preferred_element_type=F32)
        s["K2"] = x1[:, 0:P]
        s["V2"] = x1[:, P:2 * P]
        s["R2"] = prob[0] + x2[:, 0:P]
        s["Y2"] = x2[:, P:2 * P]
        s["H"] = prob[6]
        s["pc"] = jnp.exp(prob[5])
        s["ys"] = [None] * nch

    for ci in range(nch):
        for s, prob in zip(st, probs):
            (rt, kt, bt, kh, v, le, _, reverse) = prob
            c = nch - 1 - ci if reverse else ci
            sl = slice(c * C, (c + 1) * C)
            pc = s["pc"][c:c + 1]
            bh = (bt[sl] * pc).T
            khh = (kh[sl] * pc).T
            Mc = jnp.where(heye, pc, 0.0) - jnp.where(bdiag, _dot(bh, s["K2"][sl]), 0.0)
            Gc = jnp.where(bdiag, _dot(jnp.concatenate([khh, -bh], 1),
                                       jnp.concatenate([v[sl], s["V2"][sl]], 0)), 0.0)
            s["ys"][c] = _dot(s["R2"][sl], s["H"]) + s["Y2"][sl]
            s["H"] = _dot(Mc, s["H"]) + Gc
    return [(jnp.concatenate(s["ys"], 0), s["H"]) for s in st]


WKV_PAIRS = 4


def _wkv_kernel(rtf_ref, ktf_ref, btf_ref, khf_ref, vf_ref, lef_ref,
                rtb_ref, ktb_ref, btb_ref, khb_ref, vb_ref, leb_ref, yf_ref, yb_ref, h_ref):
    i = pl.program_id(2)

    @pl.when(i == 0)
    def _():
        h_ref[...] = jnp.zeros_like(h_ref)

    P = 2 * RW_HEAD_DIM
    dirs = ((rtf_ref, ktf_ref, btf_ref, khf_ref, vf_ref, lef_ref, yf_ref),
            (rtb_ref, ktb_ref, btb_ref, khb_ref, vb_ref, leb_ref, yb_ref))
    probs, outs = [], []
    for d, (rt_ref, kt_ref, bt_ref, kh_ref, v_ref, le_ref, y_ref) in enumerate(dirs):
        for pr in range(WKV_PAIRS):
            ln = slice(pr * P, (pr + 1) * P)
            probs.append((rt_ref[0, 0, :, ln], kt_ref[0, 0, :, ln], bt_ref[0, 0, :, ln], kh_ref[0, 0, :, ln],
                          v_ref[0, :, ln], le_ref[0, 0, 0, :, ln], h_ref[d, pr], d == 1))
            outs.append((y_ref, d, pr, ln))
    for (y, H), (y_ref, d, pr, ln) in zip(_wkv_tiles(probs), outs):
        y_ref[0, :, ln] = y
        h_ref[d, pr] = H


def wkv_scan(rt, kt, bt, kh, v, le):
    _, B, S, W = rt.shape
    TS = WKV_TILE
    nt = S // TS
    nch = TS // WKV_CHUNK
    P = 2 * RW_HEAD_DIM
    PW = WKV_PAIRS * P
    tiles = (lambda i: i, lambda i: nt - 1 - i)

    def dir_specs(d):
        t = tiles[d]
        big = lambda: pl.BlockSpec((1, 1, TS, PW), lambda b, p, i: (d, b, t(i), p))
        return [big(), big(), big(), big(),
                pl.BlockSpec((1, TS, PW), lambda b, p, i: (b, t(i), p)),
                pl.BlockSpec((1, 1, 1, nch, PW), lambda b, p, i: (d, b, t(i), 0, p))]

    return pl.pallas_call(
        _wkv_kernel,
        grid=(B, W // PW, nt),
        in_specs=dir_specs(0) + dir_specs(1),
        out_specs=[pl.BlockSpec((1, TS, PW), lambda b, p, i: (b, tiles[0](i), p)),
                   pl.BlockSpec((1, TS, PW), lambda b, p, i: (b, tiles[1](i), p))],
        out_shape=[jax.ShapeDtypeStruct((B, S, W), F32)] * 2,
        scratch_shapes=[pltpu.VMEM((2, WKV_PAIRS, P, P), F32)],
        compiler_params=_cparams("parallel", "parallel", "arbitrary"),
        name="wkv",
    )(rt, kt, bt, kh, v, le, rt, kt, bt, kh, v, le)


def _rec_out_kernel(x_ref, rg_ref, yf_ref, yb_ref, bonus_ref, g_ref, lnw_ref, lnb_ref, w_ref, o_ref):
    W = RW_WIDTH
    ones = _head_ones(W, RW_HEAD_DIM)
    y = yf_ref[...] + yb_ref[...]
    inv_n = 1.0 / RW_HEAD_DIM
    mu = _dot(y, ones) * inv_n
    yc = y - mu
    var = _dot(yc * yc, ones) * inv_n
    yn = yc * lax.rsqrt(var + RW_GN_EPS) * lnw_ref[...] + lnb_ref[...]
    rw_out = (yn + bonus_ref[...]) * g_ref[...]
    mix = (jnp.dot(rg_ref[...], w_ref[0:RG_WIDTH, :], preferred_element_type=F32)
           + jnp.dot(rw_out.astype(BF16), w_ref[RG_WIDTH:RG_WIDTH + W, :], preferred_element_type=F32))
    o_ref[...] = x_ref[...] + mix


def rec_out(x2d, rg_out, y_f, y_b, bonus, g, ln_w, ln_b, w_out):
    T, D = x2d.shape
    W = RW_WIDTH
    row = lambda n: pl.BlockSpec((ROW_TILE, n), lambda i: (i, 0))
    vec = lambda n: pl.BlockSpec((1, n), lambda i: (0, 0))
    return pl.pallas_call(
        _rec_out_kernel,
        grid=(T // ROW_TILE,),
        in_specs=[row(D), row(RG_WIDTH), row(W), row(W), row(W), row(W), vec(W), vec(W),
                  pl.BlockSpec((RG_WIDTH + W, D), lambda i: (0, 0))],
        out_specs=row(D),
        out_shape=jax.ShapeDtypeStruct((T, D), F32),
        compiler_params=_cparams("parallel"),
        name="rec_out",
    )(x2d, rg_out, y_f, y_b, bonus, g, ln_w.reshape(1, W), ln_b.reshape(1, W), w_out.astype(BF16))


def recurrent_layer(x, norm_g, w_in, conv_w, conv_b, rg_wa, rg_ba, rg_wx, rg_bx, rg_lambda,
                    mu_l, mu_r, w0, w_up, a0, a_up, g_up, k_k, k_a, r_k, ln_w, ln_b, w_out):
    B, S, D = x.shape
    T = B * S
    x2d = x.reshape(T, D)
    rg_x, rg_g, rw = norm_proj(x2d, norm_g, w_in, (RG_WIDTH, RG_WIDTH, RW_IN))
    rg_out = rglru_branch(rg_x.reshape(B, S, RG_WIDTH), rg_g.reshape(B, S, RG_WIDTH), conv_w, conv_b,
                          rg_wa, rg_ba, rg_wx, rg_bx, rg_lambda)
    rt, kt, bt, kh, v, le, bonus, g = rwkv_prep(rw.reshape(B, S, RW_IN), mu_l, mu_r, w0, w_up, a0, a_up,
                                                g_up, k_k, k_a, r_k.reshape(-1))
    y_f, y_b = wkv_scan(rt, kt, bt, kh, v, le)
    flat = lambda t: t.reshape(T, -1)
    out = rec_out(x2d, flat(rg_out), flat(y_f), flat(y_b), flat(bonus), flat(g), ln_w, ln_b, w_out)
    return out.reshape(B, S, D)


def _alibi_slope(h):
    return 2.0 ** (-8.0 * (h + 1) / ATT_HEADS)


ATT_PAIR = 2 * ATT_HEAD_DIM
ATT_KVW = ATT_KV_HEADS * ATT_PAIR


def _qkv_proj_kernel(x_ref, g_ref, w_ref, eq_ref, ek_ref, qg_ref, kg_ref, q_ref, k_ref, v_ref):
    QW = ATT_HEADS * ATT_HEAD_DIM
    x = x_ref[...]
    h = (x * lax.rsqrt(jnp.mean(x * x, -1, keepdims=True) + RMS_EPS) * g_ref[...]).astype(BF16)
    q = jnp.dot(h, w_ref[:, 0:QW], preferred_element_type=F32)
    msq = jnp.dot((q * q).astype(BF16), eq_ref[...], preferred_element_type=F32) * (1.0 / ATT_HEAD_DIM)
    q_ref[...] = (q * lax.rsqrt(msq + RMS_EPS) * qg_ref[...]).astype(BF16)
    k = jnp.dot(h, w_ref[:, QW:QW + ATT_KVW], preferred_element_type=F32)
    msk = jnp.dot((k * k).astype(BF16), ek_ref[...], preferred_element_type=F32) * (1.0 / ATT_PAIR)
    k_ref[...] = (k * lax.rsqrt(msk + RMS_EPS) * kg_ref[...]).astype(BF16)
    v_ref[...] = jnp.dot(h, w_ref[:, QW + ATT_KVW:QW + 2 * ATT_KVW], preferred_element_type=F32).astype(BF16)


def qkv_proj(x2d, g, w_in, q_norm, k_norm):
    T, D = x2d.shape
    QW = ATT_HEADS * ATT_HEAD_DIM
    KW = ATT_KV_HEADS * ATT_HEAD_DIM
    dup = lambda w: jnp.concatenate([w.reshape(D, ATT_KV_HEADS, 1, ATT_HEAD_DIM)] * 2, 2).reshape(D, ATT_KVW)
    w = jnp.concatenate([w_in[:, :QW], dup(w_in[:, QW:QW + KW]), dup(w_in[:, QW + KW:])], 1).astype(BF16)
    N = QW + 2 * ATT_KVW
    qg = jnp.tile(q_norm.astype(F32), ATT_HEADS).reshape(1, QW) * (ATT_HEAD_DIM ** -0.5)
    kg = jnp.tile(k_norm.astype(F32), 2 * ATT_KV_HEADS).reshape(1, ATT_KVW)
    row = lambda n: pl.BlockSpec((ROW_TILE, n), lambda i: (i, 0))
    fixed = lambda r, n: pl.BlockSpec((r, n), lambda i: (0, 0))
    return pl.pallas_call(
        _qkv_proj_kernel,
        grid=(T // ROW_TILE,),
        in_specs=[row(D), fixed(1, D), fixed(D, N), fixed(QW, QW), fixed(ATT_KVW, ATT_KVW),
                  fixed(1, QW), fixed(1, ATT_KVW)],
        out_specs=[row(QW), row(ATT_KVW), row(ATT_KVW)],
        out_shape=[jax.ShapeDtypeStruct((T, QW), BF16), jax.ShapeDtypeStruct((T, ATT_KVW), BF16),
                   jax.ShapeDtypeStruct((T, ATT_KVW), BF16)],
        compiler_params=_cparams("parallel"),
        name="qkv_proj",
    )(x2d, g.reshape(1, D), w, _head_ones(QW, ATT_HEAD_DIM), _head_ones(ATT_KVW, ATT_PAIR), qg, kg)


def _attn_kernel(sink_ref, x_ref, q_ref, kp_ref, kc_ref, kn_ref, vp_ref, vc_ref, vn_ref, bias_ref,
                 w_ref, o_ref):
    i = pl.program_id(1)
    nb = pl.num_programs(1)
    BLK = ATT_BLOCK
    P = ATT_PAIR
    span = 3 * BLK
    kc = jnp.concatenate([kp_ref[0], kc_ref[0], kn_ref[0]], 0)
    vc = jnp.concatenate([vp_ref[0], vc_ref[0], vn_ref[0]], 0)
    kpos = lax.broadcasted_iota(jnp.int32, (1, span), 1) + (i - 1) * BLK
    edge = jnp.where((kpos >= 0) & (kpos < nb * BLK), 0.0, NEG_INF)
    lane = lax.broadcasted_iota(jnp.int32, (1, P), 1)
    low = lane < ATT_HEAD_DIM
    zero = jnp.zeros((), BF16)
    ones = jnp.ones((span, P), BF16)
    slabs = []
    for g in range(ATT_KV_HEADS):
        kg = kc[:, g * P:(g + 1) * P]
        k_half = (jnp.where(low, kg, zero), jnp.where(low, zero, kg))
        v_ext = jnp.concatenate([vc[:, g * P:(g + 1) * P], ones], 1)
        for pr in range(ATT_GROUP // 2):
            slab = g * (ATT_GROUP // 2) + pr
            qp = q_ref[0, :, slab * P:(slab + 1) * P]
            halves = []
            for hf in range(2):
                h = 2 * slab + hf
                s = lax.dot_general(qp, k_half[hf], (((1,), (1,)), ((), ())), preferred_element_type=F32)
                s = s + bias_ref[h] + edge
                sk = sink_ref[h]
                m = jnp.maximum(jnp.max(s, -1, keepdims=True), sk)
                p = jnp.exp(s - m).astype(BF16)
                o = jnp.dot(p, v_ext, preferred_element_type=F32)
                halves.append(o[:, 0:P] / (o[:, P:2 * P] + jnp.exp(sk - m)))
            slabs.append(jnp.where(low, halves[0], halves[1]))
    o = jnp.concatenate(slabs, -1).astype(BF16)
    o_ref[0] = x_ref[0] + jnp.dot(o, w_ref[...], preferred_element_type=F32)


def attention_layer(x, norm_g, w_in, q_norm, k_norm, sink, w_out):
    B, S, D = x.shape
    T = B * S
    QW = ATT_HEADS * ATT_HEAD_DIM
    q, k, v = qkv_proj(x.reshape(T, D), norm_g, w_in, q_norm, k_norm)
    q = q.reshape(B, S, QW)
    k = k.reshape(B, S, ATT_KVW)
    v = v.reshape(B, S, ATT_KVW)
    nb = S // ATT_BLOCK
    span = 3 * ATT_BLOCK
    rel = (jnp.arange(span)[None, :] - WINDOW) - jnp.arange(ATT_BLOCK)[:, None]
    slopes = jnp.asarray([_alibi_slope(h) for h in range(ATT_HEADS)], F32)
    bias = jnp.where(jnp.abs(rel) <= WINDOW, -slopes[:, None, None] * jnp.abs(rel).astype(F32), NEG_INF)
    prev = lambda b, i: (b, jnp.maximum(i - 1, 0), 0)
    cur = lambda b, i: (b, i, 0)
    nxt = lambda b, i: (b, jnp.minimum(i + 1, nb - 1), 0)
    kv = lambda f: pl.BlockSpec((1, ATT_BLOCK, ATT_KVW), f)
    return pl.pallas_call(
        _attn_kernel,
        grid=(B, nb),
        in_specs=[pl.BlockSpec(memory_space=pltpu.SMEM),
                  pl.BlockSpec((1, ATT_BLOCK, D), cur),
                  pl.BlockSpec((1, ATT_BLOCK, QW), cur),
                  kv(prev), kv(cur), kv(nxt), kv(prev), kv(cur), kv(nxt),
                  pl.BlockSpec((ATT_HEADS, ATT_BLOCK, span), lambda b, i: (0, 0, 0)),
                  pl.BlockSpec((QW, D), lambda b, i: (0, 0))],
        out_specs=pl.BlockSpec((1, ATT_BLOCK, D), cur),
        out_shape=jax.ShapeDtypeStruct((B, S, D), F32),
        compiler_params=_cparams("parallel", "parallel"),
        name="window_attn",
    )(sink.astype(F32), x, q, k, k, k, v, v, v, bias, w_out.astype(BF16))


ROUTE_OFF = N_GROUPS


def _router_kernel(x_ref, g_ref, w12_ref, b_ref, h_ref, ri_ref, rg_ref, cnt_ref, run_ref):
    i = pl.program_id(0)

    @pl.when(i == 0)
    def _():
        run_ref[...] = jnp.zeros_like(run_ref)

    TM = x_ref.shape[0]
    x = x_ref[...]
    h = x * lax.rsqrt(jnp.mean(x * x, -1, keepdims=True) + RMS_EPS) * g_ref[...]
    _store_rows(h_ref, _pack_bf16_pairs(h))
    h1 = h.astype(BF16)
    h2 = (h - h1.astype(F32)).astype(BF16)
    hw = jnp.dot(h1, w12_ref[...], preferred_element_type=F32)
    lg = (hw[:, 0:ROUTE_LANES] + hw[:, ROUTE_LANES:2 * ROUTE_LANES]
          + jnp.dot(h2, w12_ref[:, 0:ROUTE_LANES], preferred_element_type=F32)) + b_ref[...]
    lane_i = lax.broadcasted_iota(jnp.int32, (1, ROUTE_LANES), 1)
    lane = lane_i.astype(F32)
    lane_group = ((lane_i - ROUTE_OFF + EXPERTS_PER_GROUP) // EXPERTS_PER_GROUP - 1).astype(F32)
    far = float(ROUTE_LANES)
    gmask = lane_i < N_GROUPS
    gl = jnp.where(gmask, lg, NEG_INF)
    gm = jnp.max(gl, -1, keepdims=True)
    p_group = 1.0 / jnp.sum(jnp.where(gmask, jnp.exp(gl - gm), 0.0), -1, keepdims=True)
    group = jnp.min(jnp.where(gl == gm, lane, far), -1, keepdims=True)
    fmask = lane_group == group
    fl = jnp.where(fmask, lg, NEG_INF)
    m1 = jnp.max(fl, -1, keepdims=True)
    ssum = jnp.sum(jnp.where(fmask, jnp.exp(fl - m1), 0.0), -1, keepdims=True)
    i1 = jnp.min(jnp.where(fl == m1, lane, far), -1, keepdims=True)
    fl2 = jnp.where(lane == i1, NEG_INF, fl)
    m2 = jnp.max(fl2, -1, keepdims=True)
    i2 = jnp.min(jnp.where(fl2 == m2, lane, far), -1, keepdims=True)
    p1 = 1.0 / ssum
    p2 = jnp.exp(m2 - m1) / ssum
    norm = p_group / (p1 + p2)
    oh = jnp.where((lane == i1) | (lane == i2), 1.0, 0.0)
    tr = lax.broadcasted_iota(jnp.int32, (TM, TM), 0)
    tc = lax.broadcasted_iota(jnp.int32, (TM, TM), 1)
    before = jnp.where(tc < tr, 1.0, 0.0).astype(BF16)
    pre = jnp.dot(before, oh.astype(BF16), preferred_element_type=F32) + run_ref[...]
    rank1 = jnp.sum(jnp.where(lane == i1, pre, 0.0), -1, keepdims=True)
    rank2 = jnp.sum(jnp.where(lane == i2, pre, 0.0), -1, keepdims=True)
    total = run_ref[...] + jnp.sum(oh, 0, keepdims=True)
    run_ref[...] = total
    cnt_ref[...] = total.astype(jnp.int32)
    cols = jnp.where(lane_i == 0, i1 - ROUTE_OFF,
                     jnp.where(lane_i == 1, i2 - ROUTE_OFF,
                               jnp.where(lane_i == 2, rank1, jnp.where(lane_i == 3, rank2, 0.0))))
    ri_ref[...] = jnp.transpose(cols)[0:SUBLANES, :].astype(jnp.int32)
    rg_ref[...] = jnp.where(lane_i == 0, p1 * norm, jnp.where(lane_i == 1, p2 * norm, 0.0))


def moe_router(x2d, g, wg1, bg1, wg2, bg2):
    T, D = x2d.shape
    wr = jnp.zeros((D, ROUTE_LANES), F32)
    wr = wr.at[:, 0:N_GROUPS].set(wg1)
    wr = wr.at[:, ROUTE_OFF:ROUTE_OFF + N_EXPERTS].set(jnp.moveaxis(wg2, 0, 1).reshape(D, N_EXPERTS))
    w1 = wr.astype(BF16)
    w2 = (wr - w1.astype(F32)).astype(BF16)
    bias = jnp.zeros((1, ROUTE_LANES), F32)
    bias = bias.at[0, 0:N_GROUPS].set(bg1).at[0, ROUTE_OFF:ROUTE_OFF + N_EXPERTS].set(bg2.reshape(-1))
    row = lambda n: pl.BlockSpec((ROW_TILE, n), lambda i: (i, 0))
    fixed = lambda r, n: pl.BlockSpec((r, n), lambda i: (0, 0))
    return pl.pallas_call(
        _router_kernel,
        grid=(T // ROW_TILE,),
        in_specs=[row(D), fixed(1, D), fixed(D, 2 * ROUTE_LANES), fixed(1, ROUTE_LANES)],
        out_specs=[pl.BlockSpec((ROW_TILE * ROW_SLABS, LANES), lambda i: (i, 0)),
                   pl.BlockSpec((SUBLANES, ROW_TILE), lambda i: (i, 0)), row(ROUTE_LANES), fixed(1, ROUTE_LANES)],
        out_shape=[jax.ShapeDtypeStruct((T * ROW_SLABS, LANES), jnp.int32),
                   jax.ShapeDtypeStruct((T // ROW_TILE * SUBLANES, ROW_TILE), jnp.int32),
                   jax.ShapeDtypeStruct((T, ROUTE_LANES), F32),
                   jax.ShapeDtypeStruct((1, ROUTE_LANES), jnp.int32)],
        scratch_shapes=[pltpu.VMEM((1, ROUTE_LANES), F32)],
        compiler_params=_cparams("arbitrary"),
        name="moe_router",
    )(x2d, g.reshape(1, D), jnp.concatenate([w1, w2], 1), bias)


ROW_SLABS = (D_MODEL // 2) // LANES


def _store_rows(ref, packed):
    n = packed.shape[0]
    for j in range(ROW_SLABS):
        ref[pl.ds(j, n, stride=ROW_SLABS), :] = packed[:, j * LANES:(j + 1) * LANES]


def _load_rows(ref, n):
    return jnp.concatenate([ref[pl.ds(j, n, stride=ROW_SLABS), :] for j in range(ROW_SLABS)], 1)


def _row_copy(src, src_row, dst, dst_row, sem):
    return pltpu.make_async_copy(src.at[pl.ds(src_row * ROW_SLABS, ROW_SLABS)],
                                 dst.at[pl.ds(dst_row * ROW_SLABS, ROW_SLABS)], sem)


DMA_UNROLL = 8


def _rows_wait(ref, nrows, sem):
    pltpu.make_async_copy(ref.at[pl.ds(0, nrows * ROW_SLABS)], ref.at[pl.ds(0, nrows * ROW_SLABS)], sem).wait()


DISPATCH_ROWS = DMA_ROWS


def _dispatch_kernel(pend_ref, dest_ref, h_ref, xs_ref, zero_ref, sem, zsem):
    i = pl.program_id(0)
    n = h_ref.shape[0] // ROW_SLABS
    blk = MOE_ROWS * ROW_SLABS
    nblk = xs_ref.shape[0] // blk

    @pl.when(i == 0)
    def _():
        zero_ref[...] = jnp.zeros_like(zero_ref)

        def block_copy(row0):
            return pltpu.make_async_copy(zero_ref, xs_ref.at[pl.ds(pl.multiple_of(row0 * ROW_SLABS, blk), blk)], zsem)

        tails = [jnp.maximum(pend_ref[e] - MOE_ROWS, 0) for e in range(N_EXPERTS)]
        for t in tails:
            block_copy(t).start()
        for t in tails:
            block_copy(t).wait()
        first_unused = pend_ref[N_EXPERTS - 1] // MOE_ROWS

        def clear(j, c):
            block_copy(j * MOE_ROWS).start()
            block_copy(j * MOE_ROWS).wait()
            return c

        lax.fori_loop(first_unused, nblk, clear, 0)

    def issue(r, c):
        for k in range(TOP_K):
            _row_copy(h_ref, r, xs_ref, dest_ref[0, 0, k * n + r], sem).start()
        return c

    lax.fori_loop(0, n, issue, 0, unroll=DMA_UNROLL)
    _rows_wait(xs_ref, TOP_K * n, sem)


def moe_dispatch(h_rows, dest, p_end, rows):
    T = h_rows.shape[0] // ROW_SLABS
    nt = T // DISPATCH_ROWS
    return pl.pallas_call(
        _dispatch_kernel,
        grid_spec=pltpu.PrefetchScalarGridSpec(
            num_scalar_prefetch=1,
            grid=(nt,),
            in_specs=[pl.BlockSpec((1, 1, TOP_K * DISPATCH_ROWS), lambda i, pe: (i, 0, 0),
                                   memory_space=pltpu.SMEM),
                      pl.BlockSpec((DISPATCH_ROWS * ROW_SLABS, LANES), lambda i, pe: (i, 0))],
            out_specs=pl.BlockSpec(memory_space=pl.ANY),
            scratch_shapes=[pltpu.VMEM((MOE_ROWS * ROW_SLABS, LANES), h_rows.dtype), pltpu.SemaphoreType.DMA(()),
                            pltpu.SemaphoreType.DMA(())]),
        out_shape=jax.ShapeDtypeStruct((rows * ROW_SLABS, LANES), h_rows.dtype),
        compiler_params=_cparams("arbitrary"),
        name="moe_dispatch",
    )(p_end, dest, h_rows)


def _expert_kernel(be_ref, nu_ref, x_ref, wg_ref, wu_ref, wd_ref, o_ref):
    i = pl.program_id(0)

    @pl.when(i < nu_ref[0])
    def _():
        xb = _unpack_bf16_pairs(_load_rows(x_ref, MOE_ROWS)).astype(BF16)
        hg = jnp.dot(xb, wg_ref[0], preferred_element_type=F32)
        hu = jnp.dot(xb, wu_ref[0], preferred_element_type=F32)
        hb = (hg * _sigmoid_tanh(hg) * hu).astype(BF16)
        _store_rows(o_ref, _pack_bf16_pairs(jnp.dot(hb, wd_ref[0], preferred_element_type=F32)))

    @pl.when(i >= nu_ref[0])
    def _():
        o_ref[...] = jnp.zeros_like(o_ref)


def moe_experts(xs, blk_exp, n_used, w_gate, w_up, w_down):
    D = D_MODEL
    blk = MOE_ROWS * ROW_SLABS
    nblk = xs.shape[0] // blk
    F = EXPERT_FF
    return pl.pallas_call(
        _expert_kernel,
        grid_spec=pltpu.PrefetchScalarGridSpec(
            num_scalar_prefetch=2,
            grid=(nblk,),
            in_specs=[pl.BlockSpec((blk, LANES), lambda i, be, nu: (jnp.minimum(i, nu[0] - 1), 0)),
                      pl.BlockSpec((1, D, F), lambda i, be, nu: (be[i], 0, 0)),
                      pl.BlockSpec((1, D, F), lambda i, be, nu: (be[i], 0, 0)),
                      pl.BlockSpec((1, F, D), lambda i, be, nu: (be[i], 0, 0))],
            out_specs=pl.BlockSpec((blk, LANES), lambda i, be, nu: (i, 0))),
        out_shape=jax.ShapeDtypeStruct(xs.shape, xs.dtype),
        compiler_params=_cparams("arbitrary"),
        name="moe_experts",
    )(blk_exp, n_used, xs, w_gate, w_up, w_down)


def _combine_kernel(dest_ref, dnext_ref, x_ref, gate_ref, eo_ref, o_ref, buf_ref, sem):
    i = pl.program_id(0)
    nt = pl.num_programs(0)
    n = x_ref.shape[0]
    slot = i % 2

    def gather(d_ref, s):
        def issue(r, c):
            for k in range(TOP_K):
                _row_copy(eo_ref, d_ref[0, 0, k * n + r], buf_ref.at[s, k], r, sem.at[s]).start()
            return c

        lax.fori_loop(0, n, issue, 0, unroll=DMA_UNROLL)

    @pl.when(i == 0)
    def _():
        gather(dest_ref, 0)

    @pl.when(i + 1 < nt)
    def _():
        gather(dnext_ref, 1 - slot)

    pltpu.make_async_copy(buf_ref.at[slot], buf_ref.at[slot], sem.at[slot]).wait()
    gate = gate_ref[...]
    ffn = gate[:, 0:1] * _unpack_bf16_pairs(_load_rows(buf_ref.at[slot, 0], n))
    for k in range(1, TOP_K):
        ffn = ffn + gate[:, k:k + 1] * _unpack_bf16_pairs(_load_rows(buf_ref.at[slot, k], n))
    o_ref[...] = x_ref[...] + ffn


def moe_combine(x2d, gates, dest, eo):
    T, D = x2d.shape
    nt = T // DMA_ROWS
    dest3 = dest
    dspec = lambda f: pl.BlockSpec((1, 1, TOP_K * DMA_ROWS), f, memory_space=pltpu.SMEM)
    return pl.pallas_call(
        _combine_kernel,
        grid=(nt,),
        in_specs=[dspec(lambda i: (i, 0, 0)),
                  dspec(lambda i: (jnp.minimum(i + 1, nt - 1), 0, 0)),
                  pl.BlockSpec((DMA_ROWS, D), lambda i: (i, 0)),
                  pl.BlockSpec((DMA_ROWS, ROUTE_LANES), lambda i: (i, 0)),
                  pl.BlockSpec(memory_space=pl.ANY)],
        out_specs=pl.BlockSpec((DMA_ROWS, D), lambda i: (i, 0)),
        out_shape=jax.ShapeDtypeStruct((T, D), F32),
        scratch_shapes=[pltpu.VMEM((2, TOP_K, DMA_ROWS * ROW_SLABS, LANES), eo.dtype),
                        pltpu.SemaphoreType.DMA((2,))],
        compiler_params=_cparams("arbitrary"),
        name="moe_combine",
    )(dest3, dest3, x2d, gates, eo)


def moe_layer(x, norm_g, wg1, bg1, wg2, bg2, w_gate, w_up, w_down):
    B, S, D = x.shape
    T = B * S
    A = T * TOP_K
    x2d = x.reshape(T, D)
    h, route_i, route_g, counts = moe_router(x2d, norm_g, wg1, bg1, wg2, bg2)
    counts = counts[0, ROUTE_OFF:ROUTE_OFF + N_EXPERTS]
    padded = (counts + MOE_ROWS - 1) // MOE_ROWS * MOE_ROWS
    p_end = jnp.cumsum(padded)
    p_start = p_end - padded
    nt = T // ROW_TILE
    ri = route_i.reshape(nt, SUBLANES, ROW_TILE)
    sel = ri[:, 0:TOP_K, :, None] == jnp.arange(N_EXPERTS, dtype=jnp.int32)
    dest = jnp.sum(jnp.where(sel, p_start, 0), -1) + ri[:, TOP_K:2 * TOP_K, :]
    dest = dest.reshape(nt, 1, TOP_K * ROW_TILE).astype(jnp.int32)
    nblk = -(-A // MOE_ROWS) + N_EXPERTS
    blk_row = jnp.arange(nblk, dtype=jnp.int32) * MOE_ROWS
    blk_exp = jnp.minimum(jnp.sum(p_end[None, :] <= blk_row[:, None], -1), N_EXPERTS - 1).astype(jnp.int32)
    n_used = (p_end[-1:] // MOE_ROWS).astype(jnp.int32)
    xs = moe_dispatch(h, dest, p_end.astype(jnp.int32), nblk * MOE_ROWS)
    eo = moe_experts(xs, blk_exp, n_used, w_gate.astype(BF16), w_up.astype(BF16), w_down.astype(BF16))
    return moe_combine(x2d, route_g, dest, eo).reshape(B, S, D)


def _trunk(x, p):
    x = recurrent_layer(x, p["norm_mix"][0], p["rec_w_in"][0], p["rg_conv_w"][0], p["rg_conv_b"][0],
                        p["rg_wa"][0], p["rg_ba"][0], p["rg_wx"][0], p["rg_bx"][0], p["rg_lambda"][0],
                        p["rw_mu_l"][0], p["rw_mu_r"][0], p["rw_w0"][0], p["rw_w_up"][0], p["rw_a0"][0],
                        p["rw_a_up"][0], p["rw_g_up"][0], p["rw_k_k"][0], p["rw_k_a"][0], p["rw_r_k"][0],
                        p["rw_ln_w"][0], p["rw_ln_b"][0], p["rec_w_out"][0])
    x = moe_layer(x, p["norm_ffn"][0], p["moe_wg1"][0], p["moe_bg1"][0], p["moe_wg2"][0], p["moe_bg2"][0],
                  p["moe_w_gate"][0], p["moe_w_up"][0], p["moe_w_down"][0])
    x = attention_layer(x, p["norm_mix"][1], p["att_w_in"][0], p["att_q_norm"][0], p["att_k_norm"][0],
                        p["att_sink"][0], p["att_w_out"][0])
    x = moe_layer(x, p["norm_ffn"][1], p["moe_wg1"][1], p["moe_bg1"][1], p["moe_wg2"][1], p["moe_bg2"][1],
                  p["moe_w_gate"][1], p["moe_w_up"][1], p["moe_w_down"][1])
    return x


def kernel(x_prompt, x_sample, norm_mix, norm_ffn, rec_w_in, rg_conv_w, rg_conv_b, rg_wa, rg_ba, rg_wx, rg_bx,
           rg_lambda, rw_mu_l, rw_mu_r, rw_w0, rw_w_up, rw_a0, rw_a_up, rw_g_up, rw_k_k, rw_k_a, rw_r_k,
           rw_ln_w, rw_ln_b, rec_w_out, att_w_in, att_q_norm, att_k_norm, att_sink, att_w_out, moe_wg1, moe_bg1,
           moe_wg2, moe_bg2, moe_w_gate, moe_w_up, moe_w_down):
    p = dict(norm_mix=norm_mix, norm_ffn=norm_ffn, rec_w_in=rec_w_in, rg_conv_w=rg_conv_w, rg_conv_b=rg_conv_b,
             rg_wa=rg_wa, rg_ba=rg_ba, rg_wx=rg_wx, rg_bx=rg_bx, rg_lambda=rg_lambda, rw_mu_l=rw_mu_l,
             rw_mu_r=rw_mu_r, rw_w0=rw_w0, rw_w_up=rw_w_up, rw_a0=rw_a0, rw_a_up=rw_a_up, rw_g_up=rw_g_up,
             rw_k_k=rw_k_k, rw_k_a=rw_k_a, rw_r_k=rw_r_k, rw_ln_w=rw_ln_w, rw_ln_b=rw_ln_b, rec_w_out=rec_w_out,
             att_w_in=att_w_in, att_q_norm=att_q_norm, att_k_norm=att_k_norm, att_sink=att_sink,
             att_w_out=att_w_out, moe_wg1=moe_wg1, moe_bg1=moe_bg1, moe_wg2=moe_wg2, moe_bg2=moe_bg2,
             moe_w_gate=moe_w_gate, moe_w_up=moe_w_up, moe_w_down=moe_w_down)
    return (_trunk(x_prompt, p), _trunk(x_sample, p))
```

```python
import functools
import math

import jax
import jax.numpy as jnp
from jax import lax
from jax.experimental import pallas as pl
from jax.experimental.pallas import tpu as pltpu

F32 = jnp.float32
BF16 = jnp.bfloat16

D_MODEL = 1024
RG_WIDTH = 512
RG_BLOCKS = 8
RG_BLOCK_DIM = 64
CONV_WIDTH = 4
RG_C = 8.0
RW_HEADS = 8
RW_HEAD_DIM = 64
RW_WIDTH = 512
DECAY_LORA = 32
ICL_LORA = 32
GATE_LORA = 64
RW_IN = 3 * RW_WIDTH + DECAY_LORA + ICL_LORA + GATE_LORA
REC_IN = 2 * RG_WIDTH + RW_IN
RW_GN_EPS = 64e-5
ATT_HEADS = 16
ATT_KV_HEADS = 4
ATT_GROUP = 4
ATT_HEAD_DIM = 64
WINDOW = 128
ATT_BLOCK = 128
ATT_IN = (ATT_HEADS + 2 * ATT_KV_HEADS) * ATT_HEAD_DIM
N_GROUPS = 4
EXPERTS_PER_GROUP = 8
N_EXPERTS = 32
TOP_K = 2
EXPERT_FF = 512
RMS_EPS = 1e-6
NEG_INF = -1e30

LANES = 128
SUBLANES = 8
VMEM_LIMIT_BYTES = 56 * 1024 * 1024
ROW_TILE = 512
WKV_TILE = 256
WKV_CHUNK = 64
MOE_ROWS = 512
ROUTE_LANES = 128
DMA_ROWS = ROW_TILE


def _cparams(*sem):
    return pltpu.CompilerParams(dimension_semantics=sem, vmem_limit_bytes=VMEM_LIMIT_BYTES)


def _dot(a, b):
    return jnp.dot(a.astype(BF16), b.astype(BF16), preferred_element_type=F32)


def _dot_nt(a, b):
    return lax.dot_general(a.astype(BF16), b.astype(BF16), (((1,), (1,)), ((), ())),
                           preferred_element_type=F32)


def _split2(x):
    h1 = x.astype(BF16)
    return h1, (x - h1.astype(F32)).astype(BF16)


def _dot_exact_lhs(e, x):
    h1, h2 = _split2(x)
    return jnp.dot(e, h1, preferred_element_type=F32) + jnp.dot(e, h2, preferred_element_type=F32)


def _sigmoid_tanh(x):
    return 0.5 * jnp.tanh(0.5 * x) + 0.5


def _pack_bf16_pairs(x):
    n = x.shape[1] // 2
    hi = lax.bitcast_convert_type(x[:, :n].astype(BF16).astype(F32), jnp.int32)
    lo = lax.bitcast_convert_type(x[:, n:].astype(BF16).astype(F32), jnp.int32)
    return hi | lax.shift_right_logical(lo, 16)


def _unpack_bf16_pairs(p):
    hi = lax.bitcast_convert_type(p & jnp.int32(-65536), F32)
    lo = lax.bitcast_convert_type(lax.shift_left(p, 16), F32)
    return jnp.concatenate([hi, lo], 1)


def _gelu_tanh(x):
    return 0.5 * x * (1.0 + jnp.tanh(math.sqrt(2.0 / math.pi) * (x + 0.044715 * (x * x * x))))


def _head_ones(width, head):
    r = lax.broadcasted_iota(jnp.int32, (width, width), 0) // head
    c = lax.broadcasted_iota(jnp.int32, (width, width), 1) // head
    return jnp.where(r == c, 1.0, 0.0).astype(BF16)


def _norm_proj_kernel(x_ref, g_ref, w_ref, *out_refs, splits):
    x = x_ref[...]
    h = x * lax.rsqrt(jnp.mean(x * x, -1, keepdims=True) + RMS_EPS) * g_ref[...]
    hb = h.astype(BF16)
    off = 0
    for o_ref, n in zip(out_refs, splits):
        o_ref[...] = jnp.dot(hb, w_ref[:, off:off + n], preferred_element_type=F32).astype(o_ref.dtype)
        off += n


def norm_proj(x2d, g, w, splits, out_dtype=F32):
    T, D = x2d.shape
    N = w.shape[1]
    assert sum(splits) == N and T % ROW_TILE == 0
    return pl.pallas_call(
        functools.partial(_norm_proj_kernel, splits=splits),
        grid=(T // ROW_TILE,),
        in_specs=[pl.BlockSpec((ROW_TILE, D), lambda i: (i, 0)),
                  pl.BlockSpec((1, D), lambda i: (0, 0)),
                  pl.BlockSpec((D, N), lambda i: (0, 0))],
        out_specs=[pl.BlockSpec((ROW_TILE, n), lambda i: (i, 0)) for n in splits],
        out_shape=[jax.ShapeDtypeStruct((T, n), out_dtype) for n in splits],
        compiler_params=_cparams("parallel"),
        name="norm_proj",
    )(x2d, g.reshape(1, D), w.astype(BF16))


RG_HALF = RG_WIDTH // 2
RG_ROWS = 256
RG_PAD = SUBLANES


def _rglru_kernel(x_ref, g_ref, cw_ref, cb_ref, wg_ref, bg_ref, sp_ref, o_ref,
                  xp_ref, af_ref, bf_ref, ab_ref, bb_ref):
    S = x_ref.shape[1]
    C = RG_HALF
    xp_ref[0:RG_PAD, :] = jnp.zeros((RG_PAD, C), F32)
    xp_ref[RG_PAD + S:RG_PAD + S + RG_PAD, :] = jnp.zeros((RG_PAD, C), F32)
    xp_ref[RG_PAD:RG_PAD + S, :] = x_ref[0]
    cw = cw_ref[...]
    left = CONV_WIDTH // 2
    for c in range(S // RG_ROWS):
        r0 = c * RG_ROWS
        xc = cb_ref[...] + cw[0:1] * xp_ref[RG_PAD + r0 - left:RG_PAD + r0 - left + RG_ROWS, :]
        for k in range(1, CONV_WIDTH):
            s0 = RG_PAD + r0 + k - left
            xc = xc + cw[k:k + 1] * xp_ref[s0:s0 + RG_ROWS, :]
        z = _dot(xc, wg_ref[0]) + bg_ref[0]
        for d, (a_ref, b_ref) in enumerate(((af_ref, bf_ref), (ab_ref, bb_ref))):
            r = _sigmoid_tanh(z[:, (2 * d) * C:(2 * d + 1) * C])
            i = _sigmoid_tanh(z[:, (2 * d + 1) * C:(2 * d + 2) * C])
            a = jnp.exp(-RG_C * r * sp_ref[0][:, d * C:(d + 1) * C])
            a_ref[r0:r0 + RG_ROWS, :] = a
            y = 1.0 - a * a
            b_ref[r0:r0 + RG_ROWS, :] = jnp.where(y > 0.0, y * lax.rsqrt(y), 0.0) * (i * xc)

    row8 = lax.broadcasted_iota(jnp.int32, (SUBLANES, C), 0)

    def tile_scan(a, b, carry, reverse):
        for s in (1, 2, 4):
            keep = (row8 < SUBLANES - s) if reverse else (row8 >= s)
            shift = SUBLANES - s if reverse else s
            b = b + a * jnp.where(keep, pltpu.roll(b, shift, 0), 0.0)
            a = a * jnp.where(keep, pltpu.roll(a, shift, 0), 1.0)
        h = b + a * carry
        last = 0 if reverse else SUBLANES - 1
        return h, h[last:last + 1]

    def body(n, carry):
        hf, hb = carry
        r0 = pl.multiple_of(n * SUBLANES, SUBLANES)
        h8, hf = tile_scan(af_ref[pl.ds(r0, SUBLANES), :], bf_ref[pl.ds(r0, SUBLANES), :], hf, False)
        bf_ref[pl.ds(r0, SUBLANES), :] = h8
        r1 = pl.multiple_of(S - SUBLANES - n * SUBLANES, SUBLANES)
        h8, hb = tile_scan(ab_ref[pl.ds(r1, SUBLANES), :], bb_ref[pl.ds(r1, SUBLANES), :], hb, True)
        bb_ref[pl.ds(r1, SUBLANES), :] = h8
        return hf, hb

    zero = jnp.zeros((1, C), F32)
    lax.fori_loop(0, S // SUBLANES, body, (zero, zero), unroll=2)
    for c in range(S // RG_ROWS):
        sl = slice(c * RG_ROWS, (c + 1) * RG_ROWS)
        o_ref[0, sl, :] = ((bf_ref[sl, :] + bb_ref[sl, :]) * _gelu_tanh(g_ref[0, sl, :])).astype(o_ref.dtype)


def rglru_branch(rg_x, rg_g, conv_w, conv_b, wa, ba, wx, bx, lam):
    B, S, _ = rg_x.shape
    C = RG_HALF
    nb = C // RG_BLOCK_DIM

    def bdiag(w):
        w = w.reshape(2, nb, RG_BLOCK_DIM, RG_BLOCK_DIM)
        eye = jnp.eye(nb, dtype=w.dtype)
        return jnp.einsum('hnij,nm->hnimj', w, eye).reshape(2, C, C)

    wg = jnp.concatenate([bdiag(wa[0]), bdiag(wx[0]), bdiag(wa[1]), bdiag(wx[1])], axis=-1).astype(BF16)

    def halves(v):
        return v.reshape(2, 1, C)

    bg = jnp.concatenate([halves(ba[0]), halves(bx[0]), halves(ba[1]), halves(bx[1])], axis=-1)
    sp = jax.nn.softplus(-lam.astype(F32))
    spg = jnp.concatenate([halves(sp[0]), halves(sp[1])], axis=-1)
    return pl.pallas_call(
        _rglru_kernel,
        grid=(B, 2),
        in_specs=[pl.BlockSpec((1, S, C), lambda b, c: (b, 0, c)),
                  pl.BlockSpec((1, S, C), lambda b, c: (b, 0, c)),
                  pl.BlockSpec((CONV_WIDTH, C), lambda b, c: (0, c)),
                  pl.BlockSpec((1, C), lambda b, c: (0, c)),
                  pl.BlockSpec((1, C, 4 * C), lambda b, c: (c, 0, 0)),
                  pl.BlockSpec((1, 1, 4 * C), lambda b, c: (c, 0, 0)),
                  pl.BlockSpec((1, 1, 2 * C), lambda b, c: (c, 0, 0))],
        out_specs=pl.BlockSpec((1, S, C), lambda b, c: (b, 0, c)),
        out_shape=jax.ShapeDtypeStruct((B, S, RG_WIDTH), BF16),
        scratch_shapes=[pltpu.VMEM((S + 2 * RG_PAD, C), F32)] + [pltpu.VMEM((S, C), F32)] * 4,
        compiler_params=_cparams("parallel", "parallel"),
        name="rglru",
    )(rg_x, rg_g, conv_w, conv_b.reshape(1, RG_WIDTH), wg, bg, spg)


def _rwkv_prep_kernel(u_ref, up_ref, un_ref, mul_ref, mur_ref, wl_ref, w0_ref, a0_ref, kk_ref, ka_ref,
                      rk_ref, rt_ref, kt_ref, bt_ref, kh_ref, v_ref, le_ref, bonus_ref, g_ref):
    i = pl.program_id(1)
    nt = pl.num_programs(1)
    TS = u_ref.shape[1]
    W = RW_WIDTH
    u = u_ref[0]
    prow = jnp.where(i == 0, 0.0, up_ref[0][SUBLANES - 1:SUBLANES, :])
    nrow = jnp.where(i == nt - 1, 0.0, un_ref[0][0:1, :])
    rows = lax.broadcasted_iota(jnp.int32, (TS, 1), 0)
    prev = jnp.where(rows == 0, prow, pltpu.roll(u, 1, 0))
    nxt = jnp.where(rows == TS - 1, nrow, pltpu.roll(u, TS - 1, 0))
    m = u + mul_ref[...] * (prev - u) + mur_ref[...] * (nxt - u)
    r = m[:, 0:W]
    k = m[:, W:2 * W]
    v = m[:, 2 * W:3 * W]
    tail = m[:, 3 * W:3 * W + LANES]
    lane = lax.broadcasted_iota(jnp.int32, (1, LANES), 1)
    z = jnp.where(lane < DECAY_LORA, jnp.tanh(tail),
                  jnp.where(lane < DECAY_LORA + ICL_LORA, tail, _sigmoid_tanh(tail)))
    lo = _dot(z, wl_ref[...])
    ones = _head_ones(W, RW_HEAD_DIM)
    kkr = k * kk_ref[...]
    kk = kkr * lax.rsqrt(jnp.maximum(_dot(kkr * kkr, ones), 1e-24))
    tr = lax.broadcasted_iota(jnp.int32, (TS, TS), 0)
    tc = lax.broadcasted_iota(jnp.int32, (TS, TS), 1)
    same = (tr // WKV_CHUNK) == (tc // WKV_CHUNK)
    kd_sum = jnp.zeros((TS, W), F32)
    nch = TS // WKV_CHUNK
    for d in range(2):
        ld = -math.exp(-0.5) * _sigmoid_tanh(w0_ref[d:d + 1, :] + lo[:, d * W:(d + 1) * W])
        a = _sigmoid_tanh(a0_ref[d:d + 1, :] + lo[:, (2 + d) * W:(3 + d) * W])
        kd = k * (1.0 + (a - 1.0) * ka_ref[...])
        kd_sum = kd_sum + kd
        tri = jnp.where(same & ((tc <= tr) if d == 0 else (tc >= tr)), 1.0, 0.0).astype(BF16)
        L = _dot_exact_lhs(tri, ld)
        en = jnp.exp(-L)
        rt_ref[d, 0] = (r * jnp.exp(L)).astype(BF16)
        kt_ref[d, 0] = (kk * jnp.exp(L - ld)).astype(BF16)
        bt_ref[d, 0] = (kk * a * en).astype(BF16)
        kh_ref[d, 0] = (kd * en).astype(BF16)
        ends = [L[(c + 1) * WKV_CHUNK - 1:(c + 1) * WKV_CHUNK] if d == 0 else L[c * WKV_CHUNK:c * WKV_CHUNK + 1]
                for c in range(nch)]
        le_ref[d, 0, 0] = jnp.concatenate(ends, 0)
    v_ref[0] = v.astype(BF16)
    bonus_ref[0] = _dot(r * kd_sum * rk_ref[...], ones) * v
    g_ref[0] = lo[:, 4 * W:5 * W]


def rwkv_prep(rw, mu_l, mu_r, w0, w_up, a0, a_up, g_up, k_k, k_a, r_k):
    B, S, _ = rw.shape
    TS = WKV_TILE
    W = RW_WIDTH
    nt = S // TS
    nch = TS // WKV_CHUNK
    hb = TS // SUBLANES
    wl = jnp.zeros((LANES, 5 * W), F32)
    wl = wl.at[0:DECAY_LORA, 0:W].set(w_up[0]).at[0:DECAY_LORA, W:2 * W].set(w_up[1])
    o = DECAY_LORA
    wl = wl.at[o:o + ICL_LORA, 2 * W:3 * W].set(a_up[0]).at[o:o + ICL_LORA, 3 * W:4 * W].set(a_up[1])
    o += ICL_LORA
    wl = wl.at[o:o + GATE_LORA, 4 * W:5 * W].set(g_up)
    vec = lambda n: pl.BlockSpec((1, n), lambda b, i: (0, 0))
    big = lambda: pl.BlockSpec((2, 1, TS, W), lambda b, i: (0, b, i, 0))
    one = lambda: pl.BlockSpec((1, TS, W), lambda b, i: (b, i, 0))
    return pl.pallas_call(
        _rwkv_prep_kernel,
        grid=(B, nt),
        in_specs=[pl.BlockSpec((1, TS, RW_IN), lambda b, i: (b, i, 0)),
                  pl.BlockSpec((1, SUBLANES, RW_IN), lambda b, i: (b, jnp.maximum(i * hb - 1, 0), 0)),
                  pl.BlockSpec((1, SUBLANES, RW_IN), lambda b, i: (b, jnp.minimum((i + 1) * hb, S // SUBLANES - 1), 0)),
                  vec(RW_IN), vec(RW_IN),
                  pl.BlockSpec((LANES, 5 * W), lambda b, i: (0, 0)),
                  pl.BlockSpec((2, W), lambda b, i: (0, 0)),
                  pl.BlockSpec((2, W), lambda b, i: (0, 0)),
                  vec(W), vec(W), vec(W)],
        out_specs=[big(), big(), big(), big(), one(),
                   pl.BlockSpec((2, 1, 1, nch, W), lambda b, i: (0, b, i, 0, 0)),
                   one(), one()],
        out_shape=[jax.ShapeDtypeStruct((2, B, S, W), BF16)] * 4
        + [jax.ShapeDtypeStruct((B, S, W), BF16),
           jax.ShapeDtypeStruct((2, B, nt, nch, W), F32),
           jax.ShapeDtypeStruct((B, S, W), F32),
           jax.ShapeDtypeStruct((B, S, W), F32)],
        compiler_params=_cparams("parallel", "parallel"),
        name="rwkv_prep",
    )(rw, rw, rw, mu_l.reshape(1, RW_IN), mu_r.reshape(1, RW_IN), wl.astype(BF16), w0, a0,
      k_k.reshape(1, W), k_a.reshape(1, W), r_k.reshape(1, W))


def _wkv_tiles(probs):
    TS = WKV_TILE
    C = WKV_CHUNK
    N = RW_HEAD_DIM
    P = 2 * N
    nch = TS // C
    zero = jnp.zeros((), BF16)
    head0 = lax.broadcasted_iota(jnp.int32, (1, P), 1) < N
    head0_2 = (lax.broadcasted_iota(jnp.int32, (1, 2 * P), 1) % P) < N
    head0_w = (lax.broadcasted_iota(jnp.int32, (1, nch * P), 1) % P) < N
    own = (lax.broadcasted_iota(jnp.int32, (TS, nch * P), 0) // C
           == lax.broadcasted_iota(jnp.int32, (TS, nch * P), 1) // P)
    tq = lax.broadcasted_iota(jnp.int32, (C, TS), 0)
    sq = lax.broadcasted_iota(jnp.int32, (C, TS), 1) % C
    blk = (lax.broadcasted_iota(jnp.int32, (TS, TS), 0) // C
           == lax.broadcasted_iota(jnp.int32, (TS, TS), 1) // C)
    hr = lax.broadcasted_iota(jnp.int32, (P, P), 0)
    hc = lax.broadcasted_iota(jnp.int32, (P, P), 1)
    bdiag = (hr // N) == (hc // N)
    heye = hr == hc

    def both_heads(xb, m):
        return jnp.concatenate([jnp.where(m, xb, zero), jnp.where(m, zero, xb)], 0)

    def wide(xb):
        return jnp.concatenate([xb[c * C:(c + 1) * C] for c in range(nch)], 1)

    def expand(xw):
        return jnp.where(blk, jnp.concatenate([xw] * nch, 0), zero)

    st = []
    for (rt, kt, bt, kh, v, le, H, reverse) in probs:
        ktb = kt.astype(BF16)
        vb = v.astype(BF16)
        ktw = wide(ktb)
        rtw = wide(rt.astype(BF16))
        lhs = jnp.concatenate([jnp.where(head0_w, ktw, zero), jnp.where(head0_w, rtw, zero),
                               jnp.where(head0_w, zero, ktw), jnp.where(head0_w, zero, rtw)], 0)
        rhs = jnp.concatenate([jnp.where(own, jnp.concatenate([bt.astype(BF16)] * nch, 1), zero),
                               jnp.where(own, jnp.concatenate([kh.astype(BF16)] * nch, 1), zero)], 0)
        gram = lax.dot_general(lhs, rhs, (((1,), (1,)), ((), ())), preferred_element_type=F32)
        st.append(dict(ktb=ktb, vb=vb, gram=gram, vm=both_heads(vb, head0)))

    chains = []
    for s, prob in zip(st, probs):
        reverse = prob[7]
        strict = (sq > tq) if reverse else (sq < tq)
        incl = (sq >= tq) if reverse else (sq <= tq)
        s["b_bd"], s["rb_bd"], s["rk_bd"] = [], [], []
        for h in range(2):
            g0 = s["gram"][2 * h * C:(2 * h + 1) * C]
            g1 = s["gram"][(2 * h + 1) * C:(2 * h + 2) * C]
            A = jnp.where(strict, g0[:, 0:TS], 0.0)
            s["b_bd"].append(expand(jnp.where(strict, g0[:, TS:2 * TS], 0.0).astype(BF16)))
            s["rb_bd"].append(expand(jnp.where(incl, g1[:, 0:TS], 0.0).astype(BF16)))
            s["rk_bd"].append(expand(jnp.where(incl, g1[:, TS:2 * TS], 0.0).astype(BF16)))
            chains.append(dict(Tw=jnp.where(sq == tq, 1.0, 0.0) - A, Ab=A.astype(BF16)))
    for ch in chains:
        ch["Q"] = jnp.dot(ch["Ab"], expand(ch["Ab"]), preferred_element_type=F32)
    for _ in range(int(math.log2(C)) - 2):
        for ch in chains:
            Qb = ch["Q"].astype(BF16)
            out = jnp.dot(jnp.concatenate([ch["Tw"].astype(BF16), Qb], 0), expand(Qb), preferred_element_type=F32)
            ch["Tw"] = ch["Tw"] + out[0:C]
            ch["Q"] = out[C:2 * C]
    for ch in chains:
        Tw = ch["Tw"] + jnp.dot(ch["Tw"].astype(BF16), expand(ch["Q"].astype(BF16)), preferred_element_type=F32)
        ch["t_bd"] = expand(Tw.astype(BF16))

    for n, s in enumerate(st):
        s["Bv"] = jnp.dot(jnp.concatenate(s["b_bd"], 1), s["vm"], preferred_element_type=F32)
    for n, s in enumerate(st):
        wm = both_heads(jnp.concatenate([s["ktb"], s["Bv"].astype(BF16)], 1), head0_2)
        t_bd = [chains[2 * n]["t_bd"], chains[2 * n + 1]["t_bd"]]
        s["x1"] = jnp.dot(jnp.concatenate(t_bd, 1), wm, preferred_element_type=F32)
    for s, prob in zip(st, probs):
        x1 = s["x1"]
        rhs2 = jnp.concatenate([both_heads(-x1.astype(BF16), head0_2),
                                jnp.concatenate([jnp.zeros((2 * TS, P), BF16), s["vm"]], 1)], 0)
        x2 = jnp.dot(jnp.concatenate(s["rb_bd"] + s["rk_bd"], 1), rhs2, preferred_element_type=F32)
        s["K2"] = x1[:, 0:P]
        s["V2"] = x1[:, P:2 * P]
        s["R2"] = prob[0] + x2[:, 0:P]
        s["Y2"] = x2[:, P:2 * P]
        s["H"] = prob[6]
        s["pc"] = jnp.exp(prob[5])
        s["ys"] = [None] * nch

    for ci in range(nch):
        for s, prob in zip(st, probs):
            (rt, kt, bt, kh, v, le, _, reverse) = prob
            c = nch - 1 - ci if reverse else ci
            sl = slice(c * C, (c + 1) * C)
            pc = s["pc"][c:c + 1]
            bh = (bt[sl] * pc).T
            khh = (kh[sl] * pc).T
            Mc = jnp.where(heye, pc, 0.0) - jnp.where(bdiag, _dot(bh, s["K2"][sl]), 0.0)
            Gc = jnp.where(bdiag, _dot(jnp.concatenate([khh, -bh], 1),
                                       jnp.concatenate([v[sl], s["V2"][sl]], 0)), 0.0)
            s["ys"][c] = _dot(s["R2"][sl], s["H"]) + s["Y2"][sl]
            s["H"] = _dot(Mc, s["H"]) + Gc
    return [(jnp.concatenate(s["ys"], 0), s["H"]) for s in st]


WKV_PAIRS = 4


def _wkv_kernel(rtf_ref, ktf_ref, btf_ref, khf_ref, vf_ref, lef_ref,
                rtb_ref, ktb_ref, btb_ref, khb_ref, vb_ref, leb_ref, yf_ref, yb_ref, h_ref):
    i = pl.program_id(2)

    @pl.when(i == 0)
    def _():
        h_ref[...] = jnp.zeros_like(h_ref)

    P = 2 * RW_HEAD_DIM
    dirs = ((rtf_ref, ktf_ref, btf_ref, khf_ref, vf_ref, lef_ref, yf_ref),
            (rtb_ref, ktb_ref, btb_ref, khb_ref, vb_ref, leb_ref, yb_ref))
    probs, outs = [], []
    for d, (rt_ref, kt_ref, bt_ref, kh_ref, v_ref, le_ref, y_ref) in enumerate(dirs):
        for pr in range(WKV_PAIRS):
            ln = slice(pr * P, (pr + 1) * P)
            probs.append((rt_ref[0, 0, :, ln], kt_ref[0, 0, :, ln], bt_ref[0, 0, :, ln], kh_ref[0, 0, :, ln],
                          v_ref[0, :, ln], le_ref[0, 0, 0, :, ln], h_ref[d, pr], d == 1))
            outs.append((y_ref, d, pr, ln))
    for (y, H), (y_ref, d, pr, ln) in zip(_wkv_tiles(probs), outs):
        y_ref[0, :, ln] = y
        h_ref[d, pr] = H


def wkv_scan(rt, kt, bt, kh, v, le):
    _, B, S, W = rt.shape
    TS = WKV_TILE
    nt = S // TS
    nch = TS // WKV_CHUNK
    P = 2 * RW_HEAD_DIM
    PW = WKV_PAIRS * P
    tiles = (lambda i: i, lambda i: nt - 1 - i)

    def dir_specs(d):
        t = tiles[d]
        big = lambda: pl.BlockSpec((1, 1, TS, PW), lambda b, p, i: (d, b, t(i), p))
        return [big(), big(), big(), big(),
                pl.BlockSpec((1, TS, PW), lambda b, p, i: (b, t(i), p)),
                pl.BlockSpec((1, 1, 1, nch, PW), lambda b, p, i: (d, b, t(i), 0, p))]

    return pl.pallas_call(
        _wkv_kernel,
        grid=(B, W // PW, nt),
        in_specs=dir_specs(0) + dir_specs(1),
        out_specs=[pl.BlockSpec((1, TS, PW), lambda b, p, i: (b, tiles[0](i), p)),
                   pl.BlockSpec((1, TS, PW), lambda b, p, i: (b, tiles[1](i), p))],
        out_shape=[jax.ShapeDtypeStruct((B, S, W), F32)] * 2,
        scratch_shapes=[pltpu.VMEM((2, WKV_PAIRS, P, P), F32)],
        compiler_params=_cparams("parallel", "parallel", "arbitrary"),
        name="wkv",
    )(rt, kt, bt, kh, v, le, rt, kt, bt, kh, v, le)


def _rec_out_kernel(x_ref, rg_ref, yf_ref, yb_ref, bonus_ref, g_ref, lnw_ref, lnb_ref, w_ref, o_ref):
    W = RW_WIDTH
    ones = _head_ones(W, RW_HEAD_DIM)
    y = yf_ref[...] + yb_ref[...]
    inv_n = 1.0 / RW_HEAD_DIM
    mu = _dot(y, ones) * inv_n
    yc = y - mu
    var = _dot(yc * yc, ones) * inv_n
    yn = yc * lax.rsqrt(var + RW_GN_EPS) * lnw_ref[...] + lnb_ref[...]
    rw_out = (yn + bonus_ref[...]) * g_ref[...]
    mix = (jnp.dot(rg_ref[...], w_ref[0:RG_WIDTH, :], preferred_element_type=F32)
           + jnp.dot(rw_out.astype(BF16), w_ref[RG_WIDTH:RG_WIDTH + W, :], preferred_element_type=F32))
    o_ref[...] = x_ref[...] + mix


def rec_out(x2d, rg_out, y_f, y_b, bonus, g, ln_w, ln_b, w_out):
    T, D = x2d.shape
    W = RW_WIDTH
    row = lambda n: pl.BlockSpec((ROW_TILE, n), lambda i: (i, 0))
    vec = lambda n: pl.BlockSpec((1, n), lambda i: (0, 0))
    return pl.pallas_call(
        _rec_out_kernel,
        grid=(T // ROW_TILE,),
        in_specs=[row(D), row(RG_WIDTH), row(W), row(W), row(W), row(W), vec(W), vec(W),
                  pl.BlockSpec((RG_WIDTH + W, D), lambda i: (0, 0))],
        out_specs=row(D),
        out_shape=jax.ShapeDtypeStruct((T, D), F32),
        compiler_params=_cparams("parallel"),
        name="rec_out",
    )(x2d, rg_out, y_f, y_b, bonus, g, ln_w.reshape(1, W), ln_b.reshape(1, W), w_out.astype(BF16))


def recurrent_layer(x, norm_g, w_in, conv_w, conv_b, rg_wa, rg_ba, rg_wx, rg_bx, rg_lambda,
                    mu_l, mu_r, w0, w_up, a0, a_up, g_up, k_k, k_a, r_k, ln_w, ln_b, w_out):
    B, S, D = x.shape
    T = B * S
    x2d = x.reshape(T, D)
    rg_x, rg_g, rw = norm_proj(x2d, norm_g, w_in, (RG_WIDTH, RG_WIDTH, RW_IN))
    rg_out = rglru_branch(rg_x.reshape(B, S, RG_WIDTH), rg_g.reshape(B, S, RG_WIDTH), conv_w, conv_b,
                          rg_wa, rg_ba, rg_wx, rg_bx, rg_lambda)
    rt, kt, bt, kh, v, le, bonus, g = rwkv_prep(rw.reshape(B, S, RW_IN), mu_l, mu_r, w0, w_up, a0, a_up,
                                                g_up, k_k, k_a, r_k.reshape(-1))
    y_f, y_b = wkv_scan(rt, kt, bt, kh, v, le)
    flat = lambda t: t.reshape(T, -1)
    out = rec_out(x2d, flat(rg_out), flat(y_f), flat(y_b), flat(bonus), flat(g), ln_w, ln_b, w_out)
    return out.reshape(B, S, D)


def _alibi_slope(h):
    return 2.0 ** (-8.0 * (h + 1) / ATT_HEADS)


ATT_PAIR = 2 * ATT_HEAD_DIM
ATT_KVW = ATT_KV_HEADS * ATT_PAIR


def _qkv_proj_kernel(x_ref, g_ref, w_ref, eq_ref, ek_ref, qg_ref, kg_ref, q_ref, k_ref, v_ref):
    QW = ATT_HEADS * ATT_HEAD_DIM
    x = x_ref[...]
    h = (x * lax.rsqrt(jnp.mean(x * x, -1, keepdims=True) + RMS_EPS) * g_ref[...]).astype(BF16)
    q = jnp.dot(h, w_ref[:, 0:QW], preferred_element_type=F32)
    msq = jnp.dot((q * q).astype(BF16), eq_ref[...], preferred_element_type=F32) * (1.0 / ATT_HEAD_DIM)
    q_ref[...] = (q * lax.rsqrt(msq + RMS_EPS) * qg_ref[...]).astype(BF16)
    k = jnp.dot(h, w_ref[:, QW:QW + ATT_KVW], preferred_element_type=F32)
    msk = jnp.dot((k * k).astype(BF16), ek_ref[...], preferred_element_type=F32) * (1.0 / ATT_PAIR)
    k_ref[...] = (k * lax.rsqrt(msk + RMS_EPS) * kg_ref[...]).astype(BF16)
    v_ref[...] = jnp.dot(h, w_ref[:, QW + ATT_KVW:QW + 2 * ATT_KVW], preferred_element_type=F32).astype(BF16)


def qkv_proj(x2d, g, w_in, q_norm, k_norm):
    T, D = x2d.shape
    QW = ATT_HEADS * ATT_HEAD_DIM
    KW = ATT_KV_HEADS * ATT_HEAD_DIM
    dup = lambda w: jnp.concatenate([w.reshape(D, ATT_KV_HEADS, 1, ATT_HEAD_DIM)] * 2, 2).reshape(D, ATT_KVW)
    w = jnp.concatenate([w_in[:, :QW], dup(w_in[:, QW:QW + KW]), dup(w_in[:, QW + KW:])], 1).astype(BF16)
    N = QW + 2 * ATT_KVW
    qg = jnp.tile(q_norm.astype(F32), ATT_HEADS).reshape(1, QW) * (ATT_HEAD_DIM ** -0.5)
    kg = jnp.tile(k_norm.astype(F32), 2 * ATT_KV_HEADS).reshape(1, ATT_KVW)
    row = lambda n: pl.BlockSpec((ROW_TILE, n), lambda i: (i, 0))
    fixed = lambda r, n: pl.BlockSpec((r, n), lambda i: (0, 0))
    return pl.pallas_call(
        _qkv_proj_kernel,
        grid=(T // ROW_TILE,),
        in_specs=[row(D), fixed(1, D), fixed(D, N), fixed(QW, QW), fixed(ATT_KVW, ATT_KVW),
                  fixed(1, QW), fixed(1, ATT_KVW)],
        out_specs=[row(QW), row(ATT_KVW), row(ATT_KVW)],
        out_shape=[jax.ShapeDtypeStruct((T, QW), BF16), jax.ShapeDtypeStruct((T, ATT_KVW), BF16),
                   jax.ShapeDtypeStruct((T, ATT_KVW), BF16)],
        compiler_params=_cparams("parallel"),
        name="qkv_proj",
    )(x2d, g.reshape(1, D), w, _head_ones(QW, ATT_HEAD_DIM), _head_ones(ATT_KVW, ATT_PAIR), qg, kg)


def _attn_kernel(sink_ref, x_ref, q_ref, kp_ref, kc_ref, kn_ref, vp_ref, vc_ref, vn_ref, bias_ref,
                 w_ref, o_ref):
    i = pl.program_id(1)
    nb = pl.num_programs(1)
    BLK = ATT_BLOCK
    P = ATT_PAIR
    span = 3 * BLK
    kc = jnp.concatenate([kp_ref[0], kc_ref[0], kn_ref[0]], 0)
    vc = jnp.concatenate([vp_ref[0], vc_ref[0], vn_ref[0]], 0)
    kpos = lax.broadcasted_iota(jnp.int32, (1, span), 1) + (i - 1) * BLK
    edge = jnp.where((kpos >= 0) & (kpos < nb * BLK), 0.0, NEG_INF)
    lane = lax.broadcasted_iota(jnp.int32, (1, P), 1)
    low = lane < ATT_HEAD_DIM
    zero = jnp.zeros((), BF16)
    ones = jnp.ones((span, P), BF16)
    slabs = []
    for g in range(ATT_KV_HEADS):
        kg = kc[:, g * P:(g + 1) * P]
        k_half = (jnp.where(low, kg, zero), jnp.where(low, zero, kg))
        v_ext = jnp.concatenate([vc[:, g * P:(g + 1) * P], ones], 1)
        for pr in range(ATT_GROUP // 2):
            slab = g * (ATT_GROUP // 2) + pr
            qp = q_ref[0, :, slab * P:(slab + 1) * P]
            halves = []
            for hf in range(2):
                h = 2 * slab + hf
                s = lax.dot_general(qp, k_half[hf], (((1,), (1,)), ((), ())), preferred_element_type=F32)
                s = s + bias_ref[h] + edge
                sk = sink_ref[h]
                m = jnp.maximum(jnp.max(s, -1, keepdims=True), sk)
                p = jnp.exp(s - m).astype(BF16)
                o = jnp.dot(p, v_ext, preferred_element_type=F32)
                halves.append(o[:, 0:P] / (o[:, P:2 * P] + jnp.exp(sk - m)))
            slabs.append(jnp.where(low, halves[0], halves[1]))
    o = jnp.concatenate(slabs, -1).astype(BF16)
    o_ref[0] = x_ref[0] + jnp.dot(o, w_ref[...], preferred_element_type=F32)


def attention_layer(x, norm_g, w_in, q_norm, k_norm, sink, w_out):
    B, S, D = x.shape
    T = B * S
    QW = ATT_HEADS * ATT_HEAD_DIM
    q, k, v = qkv_proj(x.reshape(T, D), norm_g, w_in, q_norm, k_norm)
    q = q.reshape(B, S, QW)
    k = k.reshape(B, S, ATT_KVW)
    v = v.reshape(B, S, ATT_KVW)
    nb = S // ATT_BLOCK
    span = 3 * ATT_BLOCK
    rel = (jnp.arange(span)[None, :] - WINDOW) - jnp.arange(ATT_BLOCK)[:, None]
    slopes = jnp.asarray([_alibi_slope(h) for h in range(ATT_HEADS)], F32)
    bias = jnp.where(jnp.abs(rel) <= WINDOW, -slopes[:, None, None] * jnp.abs(rel).astype(F32), NEG_INF)
    prev = lambda b, i: (b, jnp.maximum(i - 1, 0), 0)
    cur = lambda b, i: (b, i, 0)
    nxt = lambda b, i: (b, jnp.minimum(i + 1, nb - 1), 0)
    kv = lambda f: pl.BlockSpec((1, ATT_BLOCK, ATT_KVW), f)
    return pl.pallas_call(
        _attn_kernel,
        grid=(B, nb),
        in_specs=[pl.BlockSpec(memory_space=pltpu.SMEM),
                  pl.BlockSpec((1, ATT_BLOCK, D), cur),
                  pl.BlockSpec((1, ATT_BLOCK, QW), cur),
                  kv(prev), kv(cur), kv(nxt), kv(prev), kv(cur), kv(nxt),
                  pl.BlockSpec((ATT_HEADS, ATT_BLOCK, span), lambda b, i: (0, 0, 0)),
                  pl.BlockSpec((QW, D), lambda b, i: (0, 0))],
        out_specs=pl.BlockSpec((1, ATT_BLOCK, D), cur),
        out_shape=jax.ShapeDtypeStruct((B, S, D), F32),
        compiler_params=_cparams("parallel", "parallel"),
        name="window_attn",
    )(sink.astype(F32), x, q, k, k, k, v, v, v, bias, w_out.astype(BF16))


ROUTE_OFF = N_GROUPS


def _router_kernel(x_ref, g_ref, w12_ref, b_ref, h_ref, ri_ref, rg_ref, cnt_ref, run_ref):
    i = pl.program_id(0)

    @pl.when(i == 0)
    def _():
        run_ref[...] = jnp.zeros_like(run_ref)

    TM = x_ref.shape[0]
    x = x_ref[...]
    h = x * lax.rsqrt(jnp.mean(x * x, -1, keepdims=True) + RMS_EPS) * g_ref[...]
    _store_rows(h_ref, _pack_bf16_pairs(h))
    h1 = h.astype(BF16)
    h2 = (h - h1.astype(F32)).astype(BF16)
    hw = jnp.dot(h1, w12_ref[...], preferred_element_type=F32)
    lg = (hw[:, 0:ROUTE_LANES] + hw[:, ROUTE_LANES:2 * ROUTE_LANES]
          + jnp.dot(h2, w12_ref[:, 0:ROUTE_LANES], preferred_element_type=F32)) + b_ref[...]
    lane_i = lax.broadcasted_iota(jnp.int32, (1, ROUTE_LANES), 1)
    lane = lane_i.astype(F32)
    lane_group = ((lane_i - ROUTE_OFF + EXPERTS_PER_GROUP) // EXPERTS_PER_GROUP - 1).astype(F32)
    far = float(ROUTE_LANES)
    gmask = lane_i < N_GROUPS
    gl = jnp.where(gmask, lg, NEG_INF)
    gm = jnp.max(gl, -1, keepdims=True)
    p_group = 1.0 / jnp.sum(jnp.where(gmask, jnp.exp(gl - gm), 0.0), -1, keepdims=True)
    group = jnp.min(jnp.where(gl == gm, lane, far), -1, keepdims=True)
    fmask = lane_group == group
    fl = jnp.where(fmask, lg, NEG_INF)
    m1 = jnp.max(fl, -1, keepdims=True)
    ssum = jnp.sum(jnp.where(fmask, jnp.exp(fl - m1), 0.0), -1, keepdims=True)
    i1 = jnp.min(jnp.where(fl == m1, lane, far), -1, keepdims=True)
    fl2 = jnp.where(lane == i1, NEG_INF, fl)
    m2 = jnp.max(fl2, -1, keepdims=True)
    i2 = jnp.min(jnp.where(fl2 == m2, lane, far), -1, keepdims=True)
    p1 = 1.0 / ssum
    p2 = jnp.exp(m2 - m1) / ssum
    norm = p_group / (p1 + p2)
    oh = jnp.where((lane == i1) | (lane == i2), 1.0, 0.0)
    tr = lax.broadcasted_iota(jnp.int32, (TM, TM), 0)
    tc = lax.broadcasted_iota(jnp.int32, (TM, TM), 1)
    before = jnp.where(tc < tr, 1.0, 0.0).astype(BF16)
    pre = jnp.dot(before, oh.astype(BF16), preferred_element_type=F32) + run_ref[...]
    rank1 = jnp.sum(jnp.where(lane == i1, pre, 0.0), -1, keepdims=True)
    rank2 = jnp.sum(jnp.where(lane == i2, pre, 0.0), -1, keepdims=True)
    total = run_ref[...] + jnp.sum(oh, 0, keepdims=True)
    run_ref[...] = total
    cnt_ref[...] = total.astype(jnp.int32)
    cols = jnp.where(lane_i == 0, i1 - ROUTE_OFF,
                     jnp.where(lane_i == 1, i2 - ROUTE_OFF,
                               jnp.where(lane_i == 2, rank1, jnp.where(lane_i == 3, rank2, 0.0))))
    ri_ref[...] = jnp.transpose(cols)[0:SUBLANES, :].astype(jnp.int32)
    rg_ref[...] = jnp.where(lane_i == 0, p1 * norm, jnp.where(lane_i == 1, p2 * norm, 0.0))


def moe_router(x2d, g, wg1, bg1, wg2, bg2):
    T, D = x2d.shape
    wr = jnp.zeros((D, ROUTE_LANES), F32)
    wr = wr.at[:, 0:N_GROUPS].set(wg1)
    wr = wr.at[:, ROUTE_OFF:ROUTE_OFF + N_EXPERTS].set(jnp.moveaxis(wg2, 0, 1).reshape(D, N_EXPERTS))
    w1 = wr.astype(BF16)
    w2 = (wr - w1.astype(F32)).astype(BF16)
    bias = jnp.zeros((1, ROUTE_LANES), F32)
    bias = bias.at[0, 0:N_GROUPS].set(bg1).at[0, ROUTE_OFF:ROUTE_OFF + N_EXPERTS].set(bg2.reshape(-1))
    row = lambda n: pl.BlockSpec((ROW_TILE, n), lambda i: (i, 0))
    fixed = lambda r, n: pl.BlockSpec((r, n), lambda i: (0, 0))
    return pl.pallas_call(
        _router_kernel,
        grid=(T // ROW_TILE,),
        in_specs=[row(D), fixed(1, D), fixed(D, 2 * ROUTE_LANES), fixed(1, ROUTE_LANES)],
        out_specs=[pl.BlockSpec((ROW_TILE * ROW_SLABS, LANES), lambda i: (i, 0)),
                   pl.BlockSpec((SUBLANES, ROW_TILE), lambda i: (i, 0)), row(ROUTE_LANES), fixed(1, ROUTE_LANES)],
        out_shape=[jax.ShapeDtypeStruct((T * ROW_SLABS, LANES), jnp.int32),
                   jax.ShapeDtypeStruct((T // ROW_TILE * SUBLANES, ROW_TILE), jnp.int32),
                   jax.ShapeDtypeStruct((T, ROUTE_LANES), F32),
                   jax.ShapeDtypeStruct((1, ROUTE_LANES), jnp.int32)],
        scratch_shapes=[pltpu.VMEM((1, ROUTE_LANES), F32)],
        compiler_params=_cparams("arbitrary"),
        name="moe_router",
    )(x2d, g.reshape(1, D), jnp.concatenate([w1, w2], 1), bias)


ROW_SLABS = (D_MODEL // 2) // LANES


def _store_rows(ref, packed):
    n = packed.shape[0]
    for j in range(ROW_SLABS):
        ref[pl.ds(j, n, stride=ROW_SLABS), :] = packed[:, j * LANES:(j + 1) * LANES]


def _load_rows(ref, n):
    return jnp.concatenate([ref[pl.ds(j, n, stride=ROW_SLABS), :] for j in range(ROW_SLABS)], 1)


def _row_copy(src, src_row, dst, dst_row, sem):
    return pltpu.make_async_copy(src.at[pl.ds(src_row * ROW_SLABS, ROW_SLABS)],
                                 dst.at[pl.ds(dst_row * ROW_SLABS, ROW_SLABS)], sem)


DMA_UNROLL = 8


def _rows_wait(ref, nrows, sem):
    pltpu.make_async_copy(ref.at[pl.ds(0, nrows * ROW_SLABS)], ref.at[pl.ds(0, nrows * ROW_SLABS)], sem).wait()


DISPATCH_ROWS = DMA_ROWS


def _dispatch_kernel(pend_ref, dest_ref, h_ref, xs_ref, zero_ref, sem, zsem):
    i = pl.program_id(0)
    n = h_ref.shape[0] // ROW_SLABS
    blk = MOE_ROWS * ROW_SLABS
    nblk = xs_ref.shape[0] // blk

    @pl.when(i == 0)
    def _():
        zero_ref[...] = jnp.zeros_like(zero_ref)

        def block_copy(row0):
            return pltpu.make_async_copy(zero_ref, xs_ref.at[pl.ds(pl.multiple_of(row0 * ROW_SLABS, blk), blk)], zsem)

        tails = [jnp.maximum(pend_ref[e] - MOE_ROWS, 0) for e in range(N_EXPERTS)]
        for t in tails:
            block_copy(t).start()
        for t in tails:
            block_copy(t).wait()
        first_unused = pend_ref[N_EXPERTS - 1] // MOE_ROWS

        def clear(j, c):
            block_copy(j * MOE_ROWS).start()
            block_copy(j * MOE_ROWS).wait()
            return c

        lax.fori_loop(first_unused, nblk, clear, 0)

    def issue(r, c):
        for k in range(TOP_K):
            _row_copy(h_ref, r, xs_ref, dest_ref[0, 0, k * n + r], sem).start()
        return c

    lax.fori_loop(0, n, issue, 0, unroll=DMA_UNROLL)
    _rows_wait(xs_ref, TOP_K * n, sem)


def moe_dispatch(h_rows, dest, p_end, rows):
    T = h_rows.shape[0] // ROW_SLABS
    nt = T // DISPATCH_ROWS
    return pl.pallas_call(
        _dispatch_kernel,
        grid_spec=pltpu.PrefetchScalarGridSpec(
            num_scalar_prefetch=1,
            grid=(nt,),
            in_specs=[pl.BlockSpec((1, 1, TOP_K * DISPATCH_ROWS), lambda i, pe: (i, 0, 0),
                                   memory_space=pltpu.SMEM),
                      pl.BlockSpec((DISPATCH_ROWS * ROW_SLABS, LANES), lambda i, pe: (i, 0))],
            out_specs=pl.BlockSpec(memory_space=pl.ANY),
            scratch_shapes=[pltpu.VMEM((MOE_ROWS * ROW_SLABS, LANES), h_rows.dtype), pltpu.SemaphoreType.DMA(()),
                            pltpu.SemaphoreType.DMA(())]),
        out_shape=jax.ShapeDtypeStruct((rows * ROW_SLABS, LANES), h_rows.dtype),
        compiler_params=_cparams("arbitrary"),
        name="moe_dispatch",
    )(p_end, dest, h_rows)


def _expert_kernel(be_ref, nu_ref, x_ref, wg_ref, wu_ref, wd_ref, o_ref):
    i = pl.program_id(0)

    @pl.when(i < nu_ref[0])
    def _():
        xb = _unpack_bf16_pairs(_load_rows(x_ref, MOE_ROWS)).astype(BF16)
        hg = jnp.dot(xb, wg_ref[0], preferred_element_type=F32)
        hu = jnp.dot(xb, wu_ref[0], preferred_element_type=F32)
        hb = (hg * _sigmoid_tanh(hg) * hu).astype(BF16)
        _store_rows(o_ref, _pack_bf16_pairs(jnp.dot(hb, wd_ref[0], preferred_element_type=F32)))

    @pl.when(i >= nu_ref[0])
    def _():
        o_ref[...] = jnp.zeros_like(o_ref)


def moe_experts(xs, blk_exp, n_used, w_gate, w_up, w_down):
    D = D_MODEL
    blk = MOE_ROWS * ROW_SLABS
    nblk = xs.shape[0] // blk
    F = EXPERT_FF
    return pl.pallas_call(
        _expert_kernel,
        grid_spec=pltpu.PrefetchScalarGridSpec(
            num_scalar_prefetch=2,
            grid=(nblk,),
            in_specs=[pl.BlockSpec((blk, LANES), lambda i, be, nu: (jnp.minimum(i, nu[0] - 1), 0)),
                      pl.BlockSpec((1, D, F), lambda i, be, nu: (be[i], 0, 0)),
                      pl.BlockSpec((1, D, F), lambda i, be, nu: (be[i], 0, 0)),
                      pl.BlockSpec((1, F, D), lambda i, be, nu: (be[i], 0, 0))],
            out_specs=pl.BlockSpec((blk, LANES), lambda i, be, nu: (i, 0))),
        out_shape=jax.ShapeDtypeStruct(xs.shape, xs.dtype),
        compiler_params=_cparams("arbitrary"),
        name="moe_experts",
    )(blk_exp, n_used, xs, w_gate, w_up, w_down)


def _combine_kernel(dest_ref, dnext_ref, x_ref, gate_ref, eo_ref, o_ref, buf_ref, sem):
    i = pl.program_id(0)
    nt = pl.num_programs(0)
    n = x_ref.shape[0]
    slot = i % 2

    def gather(d_ref, s):
        def issue(r, c):
            for k in range(TOP_K):
                _row_copy(eo_ref, d_ref[0, 0, k * n + r], buf_ref.at[s, k], r, sem.at[s]).start()
            return c

        lax.fori_loop(0, n, issue, 0, unroll=DMA_UNROLL)

    @pl.when(i == 0)
    def _():
        gather(dest_ref, 0)

    @pl.when(i + 1 < nt)
    def _():
        gather(dnext_ref, 1 - slot)

    pltpu.make_async_copy(buf_ref.at[slot], buf_ref.at[slot], sem.at[slot]).wait()
    gate = gate_ref[...]
    ffn = gate[:, 0:1] * _unpack_bf16_pairs(_load_rows(buf_ref.at[slot, 0], n))
    for k in range(1, TOP_K):
        ffn = ffn + gate[:, k:k + 1] * _unpack_bf16_pairs(_load_rows(buf_ref.at[slot, k], n))
    o_ref[...] = x_ref[...] + ffn


def moe_combine(x2d, gates, dest, eo):
    T, D = x2d.shape
    nt = T // DMA_ROWS
    dest3 = dest
    dspec = lambda f: pl.BlockSpec((1, 1, TOP_K * DMA_ROWS), f, memory_space=pltpu.SMEM)
    return pl.pallas_call(
        _combine_kernel,
        grid=(nt,),
        in_specs=[dspec(lambda i: (i, 0, 0)),
                  dspec(lambda i: (jnp.minimum(i + 1, nt - 1), 0, 0)),
                  pl.BlockSpec((DMA_ROWS, D), lambda i: (i, 0)),
                  pl.BlockSpec((DMA_ROWS, ROUTE_LANES), lambda i: (i, 0)),
                  pl.BlockSpec(memory_space=pl.ANY)],
        out_specs=pl.BlockSpec((DMA_ROWS, D), lambda i: (i, 0)),
        out_shape=jax.ShapeDtypeStruct((T, D), F32),
        scratch_shapes=[pltpu.VMEM((2, TOP_K, DMA_ROWS * ROW_SLABS, LANES), eo.dtype),
                        pltpu.SemaphoreType.DMA((2,))],
        compiler_params=_cparams("arbitrary"),
        name="moe_combine",
    )(dest3, dest3, x2d, gates, eo)


def moe_layer(x, layer, norm_g, wg1, bg1, wg2, bg2, w_gate, w_up, w_down):
    B, S, D = x.shape
    T = B * S
    A = T * TOP_K
    x2d = x.reshape(T, D)
    h, route_i, route_g, counts = moe_router(x2d, norm_g, wg1, bg1, wg2, bg2)
    counts = counts[0, ROUTE_OFF:ROUTE_OFF + N_EXPERTS]
    padded = (counts + MOE_ROWS - 1) // MOE_ROWS * MOE_ROWS
    p_end = jnp.cumsum(padded)
    p_start = p_end - padded
    nt = T // ROW_TILE
    ri = route_i.reshape(nt, SUBLANES, ROW_TILE)
    sel = ri[:, 0:TOP_K, :, None] == jnp.arange(N_EXPERTS, dtype=jnp.int32)
    dest = jnp.sum(jnp.where(sel, p_start, 0), -1) + ri[:, TOP_K:2 * TOP_K, :]
    dest = dest.reshape(nt, 1, TOP_K * ROW_TILE).astype(jnp.int32)
    nblk = -(-A // MOE_ROWS) + N_EXPERTS
    blk_row = jnp.arange(nblk, dtype=jnp.int32) * MOE_ROWS
    blk_exp = jnp.minimum(jnp.sum(p_end[None, :] <= blk_row[:, None], -1), N_EXPERTS - 1).astype(jnp.int32)
    n_used = (p_end[-1:] // MOE_ROWS).astype(jnp.int32)
    xs = moe_dispatch(h, dest, p_end.astype(jnp.int32), nblk * MOE_ROWS)
    eo = moe_experts(xs, blk_exp + layer * N_EXPERTS, n_used, w_gate, w_up, w_down)
    return moe_combine(x2d, route_g, dest, eo).reshape(B, S, D)


def _trunk(x, p):
    x = recurrent_layer(x, p["norm_mix"][0], p["rec_w_in"][0], p["rg_conv_w"][0], p["rg_conv_b"][0],
                        p["rg_wa"][0], p["rg_ba"][0], p["rg_wx"][0], p["rg_bx"][0], p["rg_lambda"][0],
                        p["rw_mu_l"][0], p["rw_mu_r"][0], p["rw_w0"][0], p["rw_w_up"][0], p["rw_a0"][0],
                        p["rw_a_up"][0], p["rw_g_up"][0], p["rw_k_k"][0], p["rw_k_a"][0], p["rw_r_k"][0],
                        p["rw_ln_w"][0], p["rw_ln_b"][0], p["rec_w_out"][0])
    x = moe_layer(x, 0, p["norm_ffn"][0], p["moe_wg1"][0], p["moe_bg1"][0], p["moe_wg2"][0], p["moe_bg2"][0],
                  p["moe_w_gate"], p["moe_w_up"], p["moe_w_down"])
    x = attention_layer(x, p["norm_mix"][1], p["att_w_in"][0], p["att_q_norm"][0], p["att_k_norm"][0],
                        p["att_sink"][0], p["att_w_out"][0])
    x = moe_layer(x, 1, p["norm_ffn"][1], p["moe_wg1"][1], p["moe_bg1"][1], p["moe_wg2"][1], p["moe_bg2"][1],
                  p["moe_w_gate"], p["moe_w_up"], p["moe_w_down"])
    return x


def kernel(x_prompt, x_sample, norm_mix, norm_ffn, rec_w_in, rg_conv_w, rg_conv_b, rg_wa, rg_ba, rg_wx, rg_bx,
           rg_lambda, rw_mu_l, rw_mu_r, rw_w0, rw_w_up, rw_a0, rw_a_up, rw_g_up, rw_k_k, rw_k_a, rw_r_k,
           rw_ln_w, rw_ln_b, rec_w_out, att_w_in, att_q_norm, att_k_norm, att_sink, att_w_out, moe_wg1, moe_bg1,
           moe_wg2, moe_bg2, moe_w_gate, moe_w_up, moe_w_down):
    p = dict(norm_mix=norm_mix, norm_ffn=norm_ffn, rec_w_in=rec_w_in, rg_conv_w=rg_conv_w, rg_conv_b=rg_conv_b,
             rg_wa=rg_wa, rg_ba=rg_ba, rg_wx=rg_wx, rg_bx=rg_bx, rg_lambda=rg_lambda, rw_mu_l=rw_mu_l,
             rw_mu_r=rw_mu_r, rw_w0=rw_w0, rw_w_up=rw_w_up, rw_a0=rw_a0, rw_a_up=rw_a_up, rw_g_up=rw_g_up,
             rw_k_k=rw_k_k, rw_k_a=rw_k_a, rw_r_k=rw_r_k, rw_ln_w=rw_ln_w, rw_ln_b=rw_ln_b, rec_w_out=rec_w_out,
             att_w_in=att_w_in, att_q_norm=att_q_norm, att_k_norm=att_k_norm, att_sink=att_sink,
             att_w_out=att_w_out, moe_wg1=moe_wg1, moe_bg1=moe_bg1, moe_wg2=moe_wg2, moe_bg2=moe_bg2,
             moe_w_gate=moe_w_gate.astype(BF16).reshape(-1, D_MODEL, EXPERT_FF),
             moe_w_up=moe_w_up.astype(BF16).reshape(-1, D_MODEL, EXPERT_FF),
             moe_w_down=moe_w_down.astype(BF16).reshape(-1, EXPERT_FF, D_MODEL))
    return (_trunk(x_prompt, p), _trunk(x_sample, p))
```

```python
import functools
import math

import jax
import jax.numpy as jnp
from jax import lax
from jax.experimental import pallas as pl
from jax.experimental.pallas import tpu as pltpu

F32 = jnp.float32
BF16 = jnp.bfloat16

D_MODEL = 1024
RG_WIDTH = 512
RG_BLOCKS = 8
RG_BLOCK_DIM = 64
CONV_WIDTH = 4
RG_C = 8.0
RW_HEADS = 8
RW_HEAD_DIM = 64
RW_WIDTH = 512
DECAY_LORA = 32
ICL_LORA = 32
GATE_LORA = 64
RW_IN = 3 * RW_WIDTH + DECAY_LORA + ICL_LORA + GATE_LORA
REC_IN = 2 * RG_WIDTH + RW_IN
RW_GN_EPS = 64e-5
ATT_HEADS = 16
ATT_KV_HEADS = 4
ATT_GROUP = 4
ATT_HEAD_DIM = 64
WINDOW = 128
ATT_BLOCK = 128
ATT_IN = (ATT_HEADS + 2 * ATT_KV_HEADS) * ATT_HEAD_DIM
N_GROUPS = 4
EXPERTS_PER_GROUP = 8
N_EXPERTS = 32
TOP_K = 2
EXPERT_FF = 512
RMS_EPS = 1e-6
NEG_INF = -1e30

LANES = 128
SUBLANES = 8
HALO_ROWS = 16
VMEM_LIMIT_BYTES = 56 * 1024 * 1024
ROW_TILE = 512
WKV_TILE = 256
WKV_CHUNK = 64
MOE_ROWS = 512
ROUTE_LANES = 128
DMA_ROWS = ROW_TILE


def _cparams(*sem):
    return pltpu.CompilerParams(dimension_semantics=sem, vmem_limit_bytes=VMEM_LIMIT_BYTES)


def _dot(a, b):
    return jnp.dot(a.astype(BF16), b.astype(BF16), preferred_element_type=F32)


def _dot_nt(a, b):
    return lax.dot_general(a.astype(BF16), b.astype(BF16), (((1,), (1,)), ((), ())),
                           preferred_element_type=F32)


def _split2(x):
    h1 = x.astype(BF16)
    return h1, (x - h1.astype(F32)).astype(BF16)


def _dot_exact_lhs(e, x):
    h1, h2 = _split2(x)
    return jnp.dot(e, h1, preferred_element_type=F32) + jnp.dot(e, h2, preferred_element_type=F32)


def _sigmoid_tanh(x):
    return 0.5 * jnp.tanh(0.5 * x) + 0.5


def _pack_bf16_pairs(x):
    n = x.shape[1] // 2
    hi = lax.bitcast_convert_type(x[:, :n].astype(BF16).astype(F32), jnp.int32)
    lo = lax.bitcast_convert_type(x[:, n:].astype(BF16).astype(F32), jnp.int32)
    return hi | lax.shift_right_logical(lo, 16)


def _unpack_bf16_pairs(p):
    hi = lax.bitcast_convert_type(p & jnp.int32(-65536), F32)
    lo = lax.bitcast_convert_type(lax.shift_left(p, 16), F32)
    return jnp.concatenate([hi, lo], 1)


def _gelu_tanh(x):
    return 0.5 * x * (1.0 + jnp.tanh(math.sqrt(2.0 / math.pi) * (x + 0.044715 * (x * x * x))))


def _head_ones(width, head):
    r = lax.broadcasted_iota(jnp.int32, (width, width), 0) // head
    c = lax.broadcasted_iota(jnp.int32, (width, width), 1) // head
    return jnp.where(r == c, 1.0, 0.0).astype(BF16)


def _norm_proj_kernel(x_ref, g_ref, w_ref, *out_refs, splits):
    x = x_ref[...]
    h = x * lax.rsqrt(jnp.mean(x * x, -1, keepdims=True) + RMS_EPS) * g_ref[...]
    hb = h.astype(BF16)
    off = 0
    for o_ref, n in zip(out_refs, splits):
        o_ref[...] = jnp.dot(hb, w_ref[:, off:off + n], preferred_element_type=F32).astype(o_ref.dtype)
        off += n


def norm_proj(x2d, g, w, splits, out_dtype=F32):
    T, D = x2d.shape
    N = w.shape[1]
    assert sum(splits) == N and T % ROW_TILE == 0
    return pl.pallas_call(
        functools.partial(_norm_proj_kernel, splits=splits),
        grid=(T // ROW_TILE,),
        in_specs=[pl.BlockSpec((ROW_TILE, D), lambda i: (i, 0)),
                  pl.BlockSpec((1, D), lambda i: (0, 0)),
                  pl.BlockSpec((D, N), lambda i: (0, 0))],
        out_specs=[pl.BlockSpec((ROW_TILE, n), lambda i: (i, 0)) for n in splits],
        out_shape=[jax.ShapeDtypeStruct((T, n), out_dtype) for n in splits],
        compiler_params=_cparams("parallel"),
        name="norm_proj",
    )(x2d, g.reshape(1, D), w.astype(BF16))


RG_HALF = RG_WIDTH // 2
RG_ROWS = 256
RG_PAD = SUBLANES


def _rglru_kernel(x_ref, g_ref, cw_ref, cb_ref, wg_ref, bg_ref, sp_ref, o_ref,
                  xp_ref, af_ref, bf_ref, ab_ref, bb_ref):
    S = x_ref.shape[1]
    C = RG_HALF
    xp_ref[0:RG_PAD, :] = jnp.zeros((RG_PAD, C), F32)
    xp_ref[RG_PAD + S:RG_PAD + S + RG_PAD, :] = jnp.zeros((RG_PAD, C), F32)
    xp_ref[RG_PAD:RG_PAD + S, :] = x_ref[0].astype(F32)
    cw = cw_ref[...]
    left = CONV_WIDTH // 2
    for c in range(S // RG_ROWS):
        r0 = c * RG_ROWS
        xc = cb_ref[...] + cw[0:1] * xp_ref[RG_PAD + r0 - left:RG_PAD + r0 - left + RG_ROWS, :]
        for k in range(1, CONV_WIDTH):
            s0 = RG_PAD + r0 + k - left
            xc = xc + cw[k:k + 1] * xp_ref[s0:s0 + RG_ROWS, :]
        z = _dot(xc, wg_ref[0]) + bg_ref[0]
        for d, (a_ref, b_ref) in enumerate(((af_ref, bf_ref), (ab_ref, bb_ref))):
            r = _sigmoid_tanh(z[:, (2 * d) * C:(2 * d + 1) * C])
            i = _sigmoid_tanh(z[:, (2 * d + 1) * C:(2 * d + 2) * C])
            a = jnp.exp(-RG_C * r * sp_ref[0][:, d * C:(d + 1) * C])
            a_ref[r0:r0 + RG_ROWS, :] = a
            y = 1.0 - a * a
            b_ref[r0:r0 + RG_ROWS, :] = jnp.where(y > 0.0, y * lax.rsqrt(y), 0.0) * (i * xc)

    row8 = lax.broadcasted_iota(jnp.int32, (SUBLANES, C), 0)

    def tile_scan(a, b, carry, reverse):
        for s in (1, 2, 4):
            keep = (row8 < SUBLANES - s) if reverse else (row8 >= s)
            shift = SUBLANES - s if reverse else s
            b = b + a * jnp.where(keep, pltpu.roll(b, shift, 0), 0.0)
            a = a * jnp.where(keep, pltpu.roll(a, shift, 0), 1.0)
        h = b + a * carry
        last = 0 if reverse else SUBLANES - 1
        return h, h[last:last + 1]

    def body(n, carry):
        hf, hb = carry
        r0 = pl.multiple_of(n * SUBLANES, SUBLANES)
        h8, hf = tile_scan(af_ref[pl.ds(r0, SUBLANES), :], bf_ref[pl.ds(r0, SUBLANES), :], hf, False)
        bf_ref[pl.ds(r0, SUBLANES), :] = h8
        r1 = pl.multiple_of(S - SUBLANES - n * SUBLANES, SUBLANES)
        h8, hb = tile_scan(ab_ref[pl.ds(r1, SUBLANES), :], bb_ref[pl.ds(r1, SUBLANES), :], hb, True)
        bb_ref[pl.ds(r1, SUBLANES), :] = h8
        return hf, hb

    zero = jnp.zeros((1, C), F32)
    lax.fori_loop(0, S // SUBLANES, body, (zero, zero), unroll=2)
    for c in range(S // RG_ROWS):
        sl = slice(c * RG_ROWS, (c + 1) * RG_ROWS)
        o_ref[0, sl, :] = ((bf_ref[sl, :] + bb_ref[sl, :]) * _gelu_tanh(g_ref[0, sl, :].astype(F32))).astype(o_ref.dtype)


def rglru_branch(rg_x, rg_g, conv_w, conv_b, wa, ba, wx, bx, lam):
    B, S, _ = rg_x.shape
    C = RG_HALF
    nb = C // RG_BLOCK_DIM

    def bdiag(w):
        w = w.reshape(2, nb, RG_BLOCK_DIM, RG_BLOCK_DIM)
        eye = jnp.eye(nb, dtype=w.dtype)
        return jnp.einsum('hnij,nm->hnimj', w, eye).reshape(2, C, C)

    wg = jnp.concatenate([bdiag(wa[0]), bdiag(wx[0]), bdiag(wa[1]), bdiag(wx[1])], axis=-1).astype(BF16)

    def halves(v):
        return v.reshape(2, 1, C)

    bg = jnp.concatenate([halves(ba[0]), halves(bx[0]), halves(ba[1]), halves(bx[1])], axis=-1)
    sp = jax.nn.softplus(-lam.astype(F32))
    spg = jnp.concatenate([halves(sp[0]), halves(sp[1])], axis=-1)
    return pl.pallas_call(
        _rglru_kernel,
        grid=(B, 2),
        in_specs=[pl.BlockSpec((1, S, C), lambda b, c: (b, 0, c)),
                  pl.BlockSpec((1, S, C), lambda b, c: (b, 0, c)),
                  pl.BlockSpec((CONV_WIDTH, C), lambda b, c: (0, c)),
                  pl.BlockSpec((1, C), lambda b, c: (0, c)),
                  pl.BlockSpec((1, C, 4 * C), lambda b, c: (c, 0, 0)),
                  pl.BlockSpec((1, 1, 4 * C), lambda b, c: (c, 0, 0)),
                  pl.BlockSpec((1, 1, 2 * C), lambda b, c: (c, 0, 0))],
        out_specs=pl.BlockSpec((1, S, C), lambda b, c: (b, 0, c)),
        out_shape=jax.ShapeDtypeStruct((B, S, RG_WIDTH), BF16),
        scratch_shapes=[pltpu.VMEM((S + 2 * RG_PAD, C), F32)] + [pltpu.VMEM((S, C), F32)] * 4,
        compiler_params=_cparams("parallel", "parallel"),
        name="rglru",
    )(rg_x, rg_g, conv_w, conv_b.reshape(1, RG_WIDTH), wg, bg, spg)


def _rwkv_prep_kernel(u_ref, up_ref, un_ref, mul_ref, mur_ref, wl_ref, w0_ref, a0_ref, kk_ref, ka_ref,
                      rk_ref, rt_ref, kt_ref, bt_ref, kh_ref, v_ref, le_ref, bonus_ref, g_ref):
    i = pl.program_id(1)
    nt = pl.num_programs(1)
    TS = u_ref.shape[1]
    W = RW_WIDTH
    u = u_ref[0].astype(F32)
    prow = jnp.where(i == 0, 0.0, up_ref[0][HALO_ROWS - 1:HALO_ROWS, :].astype(F32))
    nrow = jnp.where(i == nt - 1, 0.0, un_ref[0][0:1, :].astype(F32))
    rows = lax.broadcasted_iota(jnp.int32, (TS, 1), 0)
    prev = jnp.where(rows == 0, prow, pltpu.roll(u, 1, 0))
    nxt = jnp.where(rows == TS - 1, nrow, pltpu.roll(u, TS - 1, 0))
    m = u + mul_ref[...] * (prev - u) + mur_ref[...] * (nxt - u)
    r = m[:, 0:W]
    k = m[:, W:2 * W]
    v = m[:, 2 * W:3 * W]
    tail = m[:, 3 * W:3 * W + LANES]
    lane = lax.broadcasted_iota(jnp.int32, (1, LANES), 1)
    z = jnp.where(lane < DECAY_LORA, jnp.tanh(tail),
                  jnp.where(lane < DECAY_LORA + ICL_LORA, tail, _sigmoid_tanh(tail)))
    lo = _dot(z, wl_ref[...])
    ones = _head_ones(W, RW_HEAD_DIM)
    kkr = k * kk_ref[...]
    kk = kkr * lax.rsqrt(jnp.maximum(_dot(kkr * kkr, ones), 1e-24))
    tr = lax.broadcasted_iota(jnp.int32, (TS, TS), 0)
    tc = lax.broadcasted_iota(jnp.int32, (TS, TS), 1)
    same = (tr // WKV_CHUNK) == (tc // WKV_CHUNK)
    kd_sum = jnp.zeros((TS, W), F32)
    nch = TS // WKV_CHUNK
    for d in range(2):
        ld = -math.exp(-0.5) * _sigmoid_tanh(w0_ref[d:d + 1, :] + lo[:, d * W:(d + 1) * W])
        a = _sigmoid_tanh(a0_ref[d:d + 1, :] + lo[:, (2 + d) * W:(3 + d) * W])
        kd = k * (1.0 + (a - 1.0) * ka_ref[...])
        kd_sum = kd_sum + kd
        tri = jnp.where(same & ((tc <= tr) if d == 0 else (tc >= tr)), 1.0, 0.0).astype(BF16)
        L = _dot_exact_lhs(tri, ld)
        en = jnp.exp(-L)
        rt_ref[d, 0] = (r * jnp.exp(L)).astype(BF16)
        kt_ref[d, 0] = (kk * jnp.exp(L - ld)).astype(BF16)
        bt_ref[d, 0] = (kk * a * en).astype(BF16)
        kh_ref[d, 0] = (kd * en).astype(BF16)
        ends = [L[(c + 1) * WKV_CHUNK - 1:(c + 1) * WKV_CHUNK] if d == 0 else L[c * WKV_CHUNK:c * WKV_CHUNK + 1]
                for c in range(nch)]
        le_ref[d, 0, 0] = jnp.concatenate(ends, 0)
    v_ref[0] = v.astype(BF16)
    bonus_ref[0] = (_dot(r * kd_sum * rk_ref[...], ones) * v).astype(BF16)
    g_ref[0] = lo[:, 4 * W:5 * W].astype(BF16)


def rwkv_prep(rw, mu_l, mu_r, w0, w_up, a0, a_up, g_up, k_k, k_a, r_k):
    B, S, _ = rw.shape
    TS = WKV_TILE
    W = RW_WIDTH
    nt = S // TS
    nch = TS // WKV_CHUNK
    hb = TS // HALO_ROWS
    wl = jnp.zeros((LANES, 5 * W), F32)
    wl = wl.at[0:DECAY_LORA, 0:W].set(w_up[0]).at[0:DECAY_LORA, W:2 * W].set(w_up[1])
    o = DECAY_LORA
    wl = wl.at[o:o + ICL_LORA, 2 * W:3 * W].set(a_up[0]).at[o:o + ICL_LORA, 3 * W:4 * W].set(a_up[1])
    o += ICL_LORA
    wl = wl.at[o:o + GATE_LORA, 4 * W:5 * W].set(g_up)
    vec = lambda n: pl.BlockSpec((1, n), lambda b, i: (0, 0))
    big = lambda: pl.BlockSpec((2, 1, TS, W), lambda b, i: (0, b, i, 0))
    one = lambda: pl.BlockSpec((1, TS, W), lambda b, i: (b, i, 0))
    return pl.pallas_call(
        _rwkv_prep_kernel,
        grid=(B, nt),
        in_specs=[pl.BlockSpec((1, TS, RW_IN), lambda b, i: (b, i, 0)),
                  pl.BlockSpec((1, HALO_ROWS, RW_IN), lambda b, i: (b, jnp.maximum(i * hb - 1, 0), 0)),
                  pl.BlockSpec((1, HALO_ROWS, RW_IN), lambda b, i: (b, jnp.minimum((i + 1) * hb, S // HALO_ROWS - 1), 0)),
                  vec(RW_IN), vec(RW_IN),
                  pl.BlockSpec((LANES, 5 * W), lambda b, i: (0, 0)),
                  pl.BlockSpec((2, W), lambda b, i: (0, 0)),
                  pl.BlockSpec((2, W), lambda b, i: (0, 0)),
                  vec(W), vec(W), vec(W)],
        out_specs=[big(), big(), big(), big(), one(),
                   pl.BlockSpec((2, 1, 1, nch, W), lambda b, i: (0, b, i, 0, 0)),
                   one(), one()],
        out_shape=[jax.ShapeDtypeStruct((2, B, S, W), BF16)] * 4
        + [jax.ShapeDtypeStruct((B, S, W), BF16),
           jax.ShapeDtypeStruct((2, B, nt, nch, W), F32),
           jax.ShapeDtypeStruct((B, S, W), BF16),
           jax.ShapeDtypeStruct((B, S, W), BF16)],
        compiler_params=_cparams("parallel", "parallel"),
        name="rwkv_prep",
    )(rw, rw, rw, mu_l.reshape(1, RW_IN), mu_r.reshape(1, RW_IN), wl.astype(BF16), w0, a0,
      k_k.reshape(1, W), k_a.reshape(1, W), r_k.reshape(1, W))


def _wkv_tiles(probs):
    TS = WKV_TILE
    C = WKV_CHUNK
    N = RW_HEAD_DIM
    P = 2 * N
    nch = TS // C
    zero = jnp.zeros((), BF16)
    head0 = lax.broadcasted_iota(jnp.int32, (1, P), 1) < N
    head0_2 = (lax.broadcasted_iota(jnp.int32, (1, 2 * P), 1) % P) < N
    head0_w = (lax.broadcasted_iota(jnp.int32, (1, nch * P), 1) % P) < N
    own = (lax.broadcasted_iota(jnp.int32, (TS, nch * P), 0) // C
           == lax.broadcasted_iota(jnp.int32, (TS, nch * P), 1) // P)
    tq = lax.broadcasted_iota(jnp.int32, (C, TS), 0)
    sq = lax.broadcasted_iota(jnp.int32, (C, TS), 1) % C
    blk = (lax.broadcasted_iota(jnp.int32, (TS, TS), 0) // C
           == lax.broadcasted_iota(jnp.int32, (TS, TS), 1) // C)
    hr = lax.broadcasted_iota(jnp.int32, (P, P), 0)
    hc = lax.broadcasted_iota(jnp.int32, (P, P), 1)
    bdiag = (hr // N) == (hc // N)
    heye = hr == hc

    def both_heads(xb, m):
        return jnp.concatenate([jnp.where(m, xb, zero), jnp.where(m, zero, xb)], 0)

    def wide(xb):
        return jnp.concatenate([xb[c * C:(c + 1) * C] for c in range(nch)], 1)

    def expand(xw):
        return jnp.where(blk, jnp.concatenate([xw] * nch, 0), zero)

    st = []
    for (rt, kt, bt, kh, v, le, H, reverse) in probs:
        ktb = kt.astype(BF16)
        vb = v.astype(BF16)
        ktw = wide(ktb)
        rtw = wide(rt.astype(BF16))
        lhs = jnp.concatenate([jnp.where(head0_w, ktw, zero), jnp.where(head0_w, rtw, zero),
                               jnp.where(head0_w, zero, ktw), jnp.where(head0_w, zero, rtw)], 0)
        rhs = jnp.concatenate([jnp.where(own, jnp.concatenate([bt.astype(BF16)] * nch, 1), zero),
                               jnp.where(own, jnp.concatenate([kh.astype(BF16)] * nch, 1), zero)], 0)
        gram = lax.dot_general(lhs, rhs, (((1,), (1,)), ((), ())), preferred_element_type=F32)
        st.append(dict(ktb=ktb, vb=vb, gram=gram, vm=both_heads(vb, head0)))

    chains = []
    for s, prob in zip(st, probs):
        reverse = prob[7]
        strict = (sq > tq) if reverse else (sq < tq)
        incl = (sq >= tq) if reverse else (sq <= tq)
        s["b_bd"], s["rb_bd"], s["rk_bd"] = [], [], []
        for h in range(2):
            g0 = s["gram"][2 * h * C:(2 * h + 1) * C]
            g1 = s["gram"][(2 * h + 1) * C:(2 * h + 2) * C]
            A = jnp.where(strict, g0[:, 0:TS], 0.0)
            s["b_bd"].append(expand(jnp.where(strict, g0[:, TS:2 * TS], 0.0).astype(BF16)))
            s["rb_bd"].append(expand(jnp.where(incl, g1[:, 0:TS], 0.0).astype(BF16)))
            s["rk_bd"].append(expand(jnp.where(incl, g1[:, TS:2 * TS], 0.0).astype(BF16)))
            chains.append(dict(Tw=jnp.where(sq == tq, 1.0, 0.0) - A, Ab=A.astype(BF16)))
    for ch in chains:
        ch["Q"] = jnp.dot(ch["Ab"], expand(ch["Ab"]), preferred_element_type=F32)
    for _ in range(int(math.log2(C)) - 2):
        for ch in chains:
            Qb = ch["Q"].astype(BF16)
            out = jnp.dot(jnp.concatenate([ch["Tw"].astype(BF16), Qb], 0), expand(Qb), preferred_element_type=F32)
            ch["Tw"] = ch["Tw"] + out[0:C]
            ch["Q"] = out[C:2 * C]
    for ch in chains:
        Tw = ch["Tw"] + jnp.dot(ch["Tw"].astype(BF16), expand(ch["Q"].astype(BF16)), preferred_element_type=F32)
        ch["t_bd"] = expand(Tw.astype(BF16))

    for n, s in enumerate(st):
        s["Bv"] = jnp.dot(jnp.concatenate(s["b_bd"], 1), s["vm"], preferred_element_type=F32)
    for n, s in enumerate(st):
        wm = both_heads(jnp.concatenate([s["ktb"], s["Bv"].astype(BF16)], 1), head0_2)
        t_bd = [chains[2 * n]["t_bd"], chains[2 * n + 1]["t_bd"]]
        s["x1"] = jnp.dot(jnp.concatenate(t_bd, 1), wm, preferred_element_type=F32)
    for s, prob in zip(st, probs):
        x1 = s["x1"]
        rhs2 = jnp.concatenate([both_heads(-x1.astype(BF16), head0_2),
                                jnp.concatenate([jnp.zeros((2 * TS, P), BF16), s["vm"]], 1)], 0)
        x2 = jnp.dot(jnp.concatenate(s["rb_bd"] + s["rk_bd"], 1), rhs2, preferred_element_type=F32)
        s["K2"] = x1[:, 0:P]
        s["V2"] = x1[:, P:2 * P]
        s["R2"] = prob[0] + x2[:, 0:P]
        s["Y2"] = x2[:, P:2 * P]
        s["H"] = prob[6]
        s["pc"] = jnp.exp(prob[5])
        s["ys"] = [None] * nch

    for ci in range(nch):
        for s, prob in zip(st, probs):
            (rt, kt, bt, kh, v, le, _, reverse) = prob
            c = nch - 1 - ci if reverse else ci
            sl = slice(c * C, (c + 1) * C)
            pc = s["pc"][c:c + 1]
            bh = (bt[sl] * pc).T
            khh = (kh[sl] * pc).T
            Mc = jnp.where(heye, pc, 0.0) - jnp.where(bdiag, _dot(bh, s["K2"][sl]), 0.0)
            Gc = jnp.where(bdiag, _dot(jnp.concatenate([khh, -bh], 1),
                                       jnp.concatenate([v[sl], s["V2"][sl]], 0)), 0.0)
            s["ys"][c] = _dot(s["R2"][sl], s["H"]) + s["Y2"][sl]
            s["H"] = _dot(Mc, s["H"]) + Gc
    return [(jnp.concatenate(s["ys"], 0), s["H"]) for s in st]


WKV_PAIRS = 4


def _wkv_kernel(rtf_ref, ktf_ref, btf_ref, khf_ref, vf_ref, lef_ref,
                rtb_ref, ktb_ref, btb_ref, khb_ref, vb_ref, leb_ref, yf_ref, yb_ref, h_ref):
    i = pl.program_id(2)

    @pl.when(i == 0)
    def _():
        h_ref[...] = jnp.zeros_like(h_ref)

    P = 2 * RW_HEAD_DIM
    dirs = ((rtf_ref, ktf_ref, btf_ref, khf_ref, vf_ref, lef_ref, yf_ref),
            (rtb_ref, ktb_ref, btb_ref, khb_ref, vb_ref, leb_ref, yb_ref))
    probs, outs = [], []
    for d, (rt_ref, kt_ref, bt_ref, kh_ref, v_ref, le_ref, y_ref) in enumerate(dirs):
        for pr in range(WKV_PAIRS):
            ln = slice(pr * P, (pr + 1) * P)
            probs.append((rt_ref[0, 0, :, ln], kt_ref[0, 0, :, ln], bt_ref[0, 0, :, ln], kh_ref[0, 0, :, ln],
                          v_ref[0, :, ln], le_ref[0, 0, 0, :, ln], h_ref[d, pr], d == 1))
            outs.append((y_ref, d, pr, ln))
    for (y, H), (y_ref, d, pr, ln) in zip(_wkv_tiles(probs), outs):
        y_ref[0, :, ln] = y.astype(y_ref.dtype)
        h_ref[d, pr] = H


def wkv_scan(rt, kt, bt, kh, v, le):
    _, B, S, W = rt.shape
    TS = WKV_TILE
    nt = S // TS
    nch = TS // WKV_CHUNK
    P = 2 * RW_HEAD_DIM
    PW = WKV_PAIRS * P
    tiles = (lambda i: i, lambda i: nt - 1 - i)

    def dir_specs(d):
        t = tiles[d]
        big = lambda: pl.BlockSpec((1, 1, TS, PW), lambda b, p, i: (d, b, t(i), p))
        return [big(), big(), big(), big(),
                pl.BlockSpec((1, TS, PW), lambda b, p, i: (b, t(i), p)),
                pl.BlockSpec((1, 1, 1, nch, PW), lambda b, p, i: (d, b, t(i), 0, p))]

    return pl.pallas_call(
        _wkv_kernel,
        grid=(B, W // PW, nt),
        in_specs=dir_specs(0) + dir_specs(1),
        out_specs=[pl.BlockSpec((1, TS, PW), lambda b, p, i: (b, tiles[0](i), p)),
                   pl.BlockSpec((1, TS, PW), lambda b, p, i: (b, tiles[1](i), p))],
        out_shape=[jax.ShapeDtypeStruct((B, S, W), BF16)] * 2,
        scratch_shapes=[pltpu.VMEM((2, WKV_PAIRS, P, P), F32)],
        compiler_params=_cparams("parallel", "parallel", "arbitrary"),
        name="wkv",
    )(rt, kt, bt, kh, v, le, rt, kt, bt, kh, v, le)


def _rec_out_kernel(x_ref, rg_ref, yf_ref, yb_ref, bonus_ref, g_ref, lnw_ref, lnb_ref, w_ref, o_ref):
    W = RW_WIDTH
    ones = _head_ones(W, RW_HEAD_DIM)
    y = yf_ref[...].astype(F32) + yb_ref[...].astype(F32)
    inv_n = 1.0 / RW_HEAD_DIM
    mu = _dot(y, ones) * inv_n
    yc = y - mu
    var = _dot(yc * yc, ones) * inv_n
    yn = yc * lax.rsqrt(var + RW_GN_EPS) * lnw_ref[...] + lnb_ref[...]
    rw_out = (yn + bonus_ref[...]) * g_ref[...]
    mix = (jnp.dot(rg_ref[...], w_ref[0:RG_WIDTH, :], preferred_element_type=F32)
           + jnp.dot(rw_out.astype(BF16), w_ref[RG_WIDTH:RG_WIDTH + W, :], preferred_element_type=F32))
    o_ref[...] = x_ref[...] + mix


def rec_out(x2d, rg_out, y_f, y_b, bonus, g, ln_w, ln_b, w_out):
    T, D = x2d.shape
    W = RW_WIDTH
    row = lambda n: pl.BlockSpec((ROW_TILE, n), lambda i: (i, 0))
    vec = lambda n: pl.BlockSpec((1, n), lambda i: (0, 0))
    return pl.pallas_call(
        _rec_out_kernel,
        grid=(T // ROW_TILE,),
        in_specs=[row(D), row(RG_WIDTH), row(W), row(W), row(W), row(W), vec(W), vec(W),
                  pl.BlockSpec((RG_WIDTH + W, D), lambda i: (0, 0))],
        out_specs=row(D),
        out_shape=jax.ShapeDtypeStruct((T, D), F32),
        compiler_params=_cparams("parallel"),
        name="rec_out",
    )(x2d, rg_out, y_f, y_b, bonus, g, ln_w.reshape(1, W), ln_b.reshape(1, W), w_out.astype(BF16))


def recurrent_layer(x, norm_g, w_in, conv_w, conv_b, rg_wa, rg_ba, rg_wx, rg_bx, rg_lambda,
                    mu_l, mu_r, w0, w_up, a0, a_up, g_up, k_k, k_a, r_k, ln_w, ln_b, w_out):
    B, S, D = x.shape
    T = B * S
    x2d = x.reshape(T, D)
    rg_x, rg_g, rw = norm_proj(x2d, norm_g, w_in, (RG_WIDTH, RG_WIDTH, RW_IN), out_dtype=BF16)
    rg_out = rglru_branch(rg_x.reshape(B, S, RG_WIDTH), rg_g.reshape(B, S, RG_WIDTH), conv_w, conv_b,
                          rg_wa, rg_ba, rg_wx, rg_bx, rg_lambda)
    rt, kt, bt, kh, v, le, bonus, g = rwkv_prep(rw.reshape(B, S, RW_IN), mu_l, mu_r, w0, w_up, a0, a_up,
                                                g_up, k_k, k_a, r_k.reshape(-1))
    y_f, y_b = wkv_scan(rt, kt, bt, kh, v, le)
    flat = lambda t: t.reshape(T, -1)
    out = rec_out(x2d, flat(rg_out), flat(y_f), flat(y_b), flat(bonus), flat(g), ln_w, ln_b, w_out)
    return out.reshape(B, S, D)


def _alibi_slope(h):
    return 2.0 ** (-8.0 * (h + 1) / ATT_HEADS)


ATT_PAIR = 2 * ATT_HEAD_DIM
ATT_KVW = ATT_KV_HEADS * ATT_PAIR


def _qkv_proj_kernel(x_ref, g_ref, w_ref, eq_ref, ek_ref, qg_ref, kg_ref, q_ref, k_ref, v_ref):
    QW = ATT_HEADS * ATT_HEAD_DIM
    x = x_ref[...]
    h = (x * lax.rsqrt(jnp.mean(x * x, -1, keepdims=True) + RMS_EPS) * g_ref[...]).astype(BF16)
    q = jnp.dot(h, w_ref[:, 0:QW], preferred_element_type=F32)
    msq = jnp.dot((q * q).astype(BF16), eq_ref[...], preferred_element_type=F32) * (1.0 / ATT_HEAD_DIM)
    q_ref[...] = (q * lax.rsqrt(msq + RMS_EPS) * qg_ref[...]).astype(BF16)
    k = jnp.dot(h, w_ref[:, QW:QW + ATT_KVW], preferred_element_type=F32)
    msk = jnp.dot((k * k).astype(BF16), ek_ref[...], preferred_element_type=F32) * (1.0 / ATT_PAIR)
    k_ref[...] = (k * lax.rsqrt(msk + RMS_EPS) * kg_ref[...]).astype(BF16)
    v_ref[...] = jnp.dot(h, w_ref[:, QW + ATT_KVW:QW + 2 * ATT_KVW], preferred_element_type=F32).astype(BF16)


def qkv_proj(x2d, g, w_in, q_norm, k_norm):
    T, D = x2d.shape
    QW = ATT_HEADS * ATT_HEAD_DIM
    KW = ATT_KV_HEADS * ATT_HEAD_DIM
    dup = lambda w: jnp.concatenate([w.reshape(D, ATT_KV_HEADS, 1, ATT_HEAD_DIM)] * 2, 2).reshape(D, ATT_KVW)
    w = jnp.concatenate([w_in[:, :QW], dup(w_in[:, QW:QW + KW]), dup(w_in[:, QW + KW:])], 1).astype(BF16)
    N = QW + 2 * ATT_KVW
    qg = jnp.tile(q_norm.astype(F32), ATT_HEADS).reshape(1, QW) * (ATT_HEAD_DIM ** -0.5)
    kg = jnp.tile(k_norm.astype(F32), 2 * ATT_KV_HEADS).reshape(1, ATT_KVW)
    row = lambda n: pl.BlockSpec((ROW_TILE, n), lambda i: (i, 0))
    fixed = lambda r, n: pl.BlockSpec((r, n), lambda i: (0, 0))
    return pl.pallas_call(
        _qkv_proj_kernel,
        grid=(T // ROW_TILE,),
        in_specs=[row(D), fixed(1, D), fixed(D, N), fixed(QW, QW), fixed(ATT_KVW, ATT_KVW),
                  fixed(1, QW), fixed(1, ATT_KVW)],
        out_specs=[row(QW), row(ATT_KVW), row(ATT_KVW)],
        out_shape=[jax.ShapeDtypeStruct((T, QW), BF16), jax.ShapeDtypeStruct((T, ATT_KVW), BF16),
                   jax.ShapeDtypeStruct((T, ATT_KVW), BF16)],
        compiler_params=_cparams("parallel"),
        name="qkv_proj",
    )(x2d, g.reshape(1, D), w, _head_ones(QW, ATT_HEAD_DIM), _head_ones(ATT_KVW, ATT_PAIR), qg, kg)


def _attn_kernel(sink_ref, x_ref, q_ref, kp_ref, kc_ref, kn_ref, vp_ref, vc_ref, vn_ref, bias_ref,
                 w_ref, o_ref):
    i = pl.program_id(1)
    nb = pl.num_programs(1)
    BLK = ATT_BLOCK
    P = ATT_PAIR
    span = 3 * BLK
    kc = jnp.concatenate([kp_ref[0], kc_ref[0], kn_ref[0]], 0)
    vc = jnp.concatenate([vp_ref[0], vc_ref[0], vn_ref[0]], 0)
    kpos = lax.broadcasted_iota(jnp.int32, (1, span), 1) + (i - 1) * BLK
    edge = jnp.where((kpos >= 0) & (kpos < nb * BLK), 0.0, NEG_INF)
    lane = lax.broadcasted_iota(jnp.int32, (1, P), 1)
    low = lane < ATT_HEAD_DIM
    zero = jnp.zeros((), BF16)
    ones = jnp.ones((span, P), BF16)
    slabs = []
    for g in range(ATT_KV_HEADS):
        kg = kc[:, g * P:(g + 1) * P]
        k_half = (jnp.where(low, kg, zero), jnp.where(low, zero, kg))
        v_ext = jnp.concatenate([vc[:, g * P:(g + 1) * P], ones], 1)
        for pr in range(ATT_GROUP // 2):
            slab = g * (ATT_GROUP // 2) + pr
            qp = q_ref[0, :, slab * P:(slab + 1) * P]
            halves = []
            for hf in range(2):
                h = 2 * slab + hf
                s = lax.dot_general(qp, k_half[hf], (((1,), (1,)), ((), ())), preferred_element_type=F32)
                s = s + bias_ref[h] + edge
                sk = sink_ref[h]
                m = jnp.maximum(jnp.max(s, -1, keepdims=True), sk)
                p = jnp.exp(s - m).astype(BF16)
                o = jnp.dot(p, v_ext, preferred_element_type=F32)
                halves.append(o[:, 0:P] / (o[:, P:2 * P] + jnp.exp(sk - m)))
            slabs.append(jnp.where(low, halves[0], halves[1]))
    o = jnp.concatenate(slabs, -1).astype(BF16)
    o_ref[0] = x_ref[0] + jnp.dot(o, w_ref[...], preferred_element_type=F32)


def attention_layer(x, norm_g, w_in, q_norm, k_norm, sink, w_out):
    B, S, D = x.shape
    T = B * S
    QW = ATT_HEADS * ATT_HEAD_DIM
    q, k, v = qkv_proj(x.reshape(T, D), norm_g, w_in, q_norm, k_norm)
    q = q.reshape(B, S, QW)
    k = k.reshape(B, S, ATT_KVW)
    v = v.reshape(B, S, ATT_KVW)
    nb = S // ATT_BLOCK
    span = 3 * ATT_BLOCK
    rel = (jnp.arange(span)[None, :] - WINDOW) - jnp.arange(ATT_BLOCK)[:, None]
    slopes = jnp.asarray([_alibi_slope(h) for h in range(ATT_HEADS)], F32)
    bias = jnp.where(jnp.abs(rel) <= WINDOW, -slopes[:, None, None] * jnp.abs(rel).astype(F32), NEG_INF)
    prev = lambda b, i: (b, jnp.maximum(i - 1, 0), 0)
    cur = lambda b, i: (b, i, 0)
    nxt = lambda b, i: (b, jnp.minimum(i + 1, nb - 1), 0)
    kv = lambda f: pl.BlockSpec((1, ATT_BLOCK, ATT_KVW), f)
    return pl.pallas_call(
        _attn_kernel,
        grid=(B, nb),
        in_specs=[pl.BlockSpec(memory_space=pltpu.SMEM),
                  pl.BlockSpec((1, ATT_BLOCK, D), cur),
                  pl.BlockSpec((1, ATT_BLOCK, QW), cur),
                  kv(prev), kv(cur), kv(nxt), kv(prev), kv(cur), kv(nxt),
                  pl.BlockSpec((ATT_HEADS, ATT_BLOCK, span), lambda b, i: (0, 0, 0)),
                  pl.BlockSpec((QW, D), lambda b, i: (0, 0))],
        out_specs=pl.BlockSpec((1, ATT_BLOCK, D), cur),
        out_shape=jax.ShapeDtypeStruct((B, S, D), F32),
        compiler_params=_cparams("parallel", "parallel"),
        name="window_attn",
    )(sink.astype(F32), x, q, k, k, k, v, v, v, bias, w_out.astype(BF16))


ROUTE_OFF = N_GROUPS


def _router_kernel(x_ref, g_ref, w12_ref, b_ref, h_ref, ri_ref, rg_ref, cnt_ref, run_ref):
    i = pl.program_id(0)

    @pl.when(i == 0)
    def _():
        run_ref[...] = jnp.zeros_like(run_ref)

    TM = x_ref.shape[0]
    x = x_ref[...]
    h = x * lax.rsqrt(jnp.mean(x * x, -1, keepdims=True) + RMS_EPS) * g_ref[...]
    _store_rows(h_ref, _pack_bf16_pairs(h))
    h1 = h.astype(BF16)
    h2 = (h - h1.astype(F32)).astype(BF16)
    hw = jnp.dot(h1, w12_ref[...], preferred_element_type=F32)
    lg = (hw[:, 0:ROUTE_LANES] + hw[:, ROUTE_LANES:2 * ROUTE_LANES]
          + jnp.dot(h2, w12_ref[:, 0:ROUTE_LANES], preferred_element_type=F32)) + b_ref[...]
    lane_i = lax.broadcasted_iota(jnp.int32, (1, ROUTE_LANES), 1)
    lane = lane_i.astype(F32)
    lane_group = ((lane_i - ROUTE_OFF + EXPERTS_PER_GROUP) // EXPERTS_PER_GROUP - 1).astype(F32)
    far = float(ROUTE_LANES)
    gmask = lane_i < N_GROUPS
    gl = jnp.where(gmask, lg, NEG_INF)
    gm = jnp.max(gl, -1, keepdims=True)
    p_group = 1.0 / jnp.sum(jnp.where(gmask, jnp.exp(gl - gm), 0.0), -1, keepdims=True)
    group = jnp.min(jnp.where(gl == gm, lane, far), -1, keepdims=True)
    fmask = lane_group == group
    fl = jnp.where(fmask, lg, NEG_INF)
    m1 = jnp.max(fl, -1, keepdims=True)
    ssum = jnp.sum(jnp.where(fmask, jnp.exp(fl - m1), 0.0), -1, keepdims=True)
    i1 = jnp.min(jnp.where(fl == m1, lane, far), -1, keepdims=True)
    fl2 = jnp.where(lane == i1, NEG_INF, fl)
    m2 = jnp.max(fl2, -1, keepdims=True)
    i2 = jnp.min(jnp.where(fl2 == m2, lane, far), -1, keepdims=True)
    p1 = 1.0 / ssum
    p2 = jnp.exp(m2 - m1) / ssum
    norm = p_group / (p1 + p2)
    oh = jnp.where((lane == i1) | (lane == i2), 1.0, 0.0)
    tr = lax.broadcasted_iota(jnp.int32, (TM, TM), 0)
    tc = lax.broadcasted_iota(jnp.int32, (TM, TM), 1)
    before = jnp.where(tc < tr, 1.0, 0.0).astype(BF16)
    pre = jnp.dot(before, oh.astype(BF16), preferred_element_type=F32) + run_ref[...]
    rank1 = jnp.sum(jnp.where(lane == i1, pre, 0.0), -1, keepdims=True)
    rank2 = jnp.sum(jnp.where(lane == i2, pre, 0.0), -1, keepdims=True)
    total = run_ref[...] + jnp.sum(oh, 0, keepdims=True)
    run_ref[...] = total
    cnt_ref[...] = total.astype(jnp.int32)
    cols = jnp.where(lane_i == 0, i1 - ROUTE_OFF,
                     jnp.where(lane_i == 1, i2 - ROUTE_OFF,
                               jnp.where(lane_i == 2, rank1, jnp.where(lane_i == 3, rank2, 0.0))))
    ri_ref[...] = jnp.transpose(cols)[0:SUBLANES, :].astype(jnp.int32)
    rg_ref[...] = jnp.where(lane_i == 0, p1 * norm, jnp.where(lane_i == 1, p2 * norm, 0.0))


def moe_router(x2d, g, wg1, bg1, wg2, bg2):
    T, D = x2d.shape
    wr = jnp.zeros((D, ROUTE_LANES), F32)
    wr = wr.at[:, 0:N_GROUPS].set(wg1)
    wr = wr.at[:, ROUTE_OFF:ROUTE_OFF + N_EXPERTS].set(jnp.moveaxis(wg2, 0, 1).reshape(D, N_EXPERTS))
    w1 = wr.astype(BF16)
    w2 = (wr - w1.astype(F32)).astype(BF16)
    bias = jnp.zeros((1, ROUTE_LANES), F32)
    bias = bias.at[0, 0:N_GROUPS].set(bg1).at[0, ROUTE_OFF:ROUTE_OFF + N_EXPERTS].set(bg2.reshape(-1))
    row = lambda n: pl.BlockSpec((ROW_TILE, n), lambda i: (i, 0))
    fixed = lambda r, n: pl.BlockSpec((r, n), lambda i: (0, 0))
    return pl.pallas_call(
        _router_kernel,
        grid=(T // ROW_TILE,),
        in_specs=[row(D), fixed(1, D), fixed(D, 2 * ROUTE_LANES), fixed(1, ROUTE_LANES)],
        out_specs=[pl.BlockSpec((ROW_TILE * ROW_SLABS, LANES), lambda i: (i, 0)),
                   pl.BlockSpec((SUBLANES, ROW_TILE), lambda i: (i, 0)), row(ROUTE_LANES), fixed(1, ROUTE_LANES)],
        out_shape=[jax.ShapeDtypeStruct((T * ROW_SLABS, LANES), jnp.int32),
                   jax.ShapeDtypeStruct((T // ROW_TILE * SUBLANES, ROW_TILE), jnp.int32),
                   jax.ShapeDtypeStruct((T, ROUTE_LANES), F32),
                   jax.ShapeDtypeStruct((1, ROUTE_LANES), jnp.int32)],
        scratch_shapes=[pltpu.VMEM((1, ROUTE_LANES), F32)],
        compiler_params=_cparams("arbitrary"),
        name="moe_router",
    )(x2d, g.reshape(1, D), jnp.concatenate([w1, w2], 1), bias)


ROW_SLABS = (D_MODEL // 2) // LANES


def _store_rows(ref, packed):
    n = packed.shape[0]
    for j in range(ROW_SLABS):
        ref[pl.ds(j, n, stride=ROW_SLABS), :] = packed[:, j * LANES:(j + 1) * LANES]


def _load_rows(ref, n):
    return jnp.concatenate([ref[pl.ds(j, n, stride=ROW_SLABS), :] for j in range(ROW_SLABS)], 1)


def _row_copy(src, src_row, dst, dst_row, sem):
    return pltpu.make_async_copy(src.at[pl.ds(src_row * ROW_SLABS, ROW_SLABS)],
                                 dst.at[pl.ds(dst_row * ROW_SLABS, ROW_SLABS)], sem)


DMA_UNROLL = 8


def _rows_wait(ref, nrows, sem):
    pltpu.make_async_copy(ref.at[pl.ds(0, nrows * ROW_SLABS)], ref.at[pl.ds(0, nrows * ROW_SLABS)], sem).wait()


DISPATCH_ROWS = DMA_ROWS


def _dispatch_kernel(pend_ref, dest_ref, h_ref, xs_ref, zero_ref, sem, zsem):
    i = pl.program_id(0)
    n = h_ref.shape[0] // ROW_SLABS
    blk = MOE_ROWS * ROW_SLABS
    nblk = xs_ref.shape[0] // blk

    @pl.when(i == 0)
    def _():
        zero_ref[...] = jnp.zeros_like(zero_ref)

        def block_copy(row0):
            return pltpu.make_async_copy(zero_ref, xs_ref.at[pl.ds(pl.multiple_of(row0 * ROW_SLABS, blk), blk)], zsem)

        tails = [jnp.maximum(pend_ref[e] - MOE_ROWS, 0) for e in range(N_EXPERTS)]
        for t in tails:
            block_copy(t).start()
        for t in tails:
            block_copy(t).wait()
        first_unused = pend_ref[N_EXPERTS - 1] // MOE_ROWS

        def clear(j, c):
            block_copy(j * MOE_ROWS).start()
            block_copy(j * MOE_ROWS).wait()
            return c

        lax.fori_loop(first_unused, nblk, clear, 0)

    def issue(r, c):
        for k in range(TOP_K):
            _row_copy(h_ref, r, xs_ref, dest_ref[0, 0, k * n + r], sem).start()
        return c

    lax.fori_loop(0, n, issue, 0, unroll=DMA_UNROLL)
    _rows_wait(xs_ref, TOP_K * n, sem)


def moe_dispatch(h_rows, dest, p_end, rows):
    T = h_rows.shape[0] // ROW_SLABS
    nt = T // DISPATCH_ROWS
    return pl.pallas_call(
        _dispatch_kernel,
        grid_spec=pltpu.PrefetchScalarGridSpec(
            num_scalar_prefetch=1,
            grid=(nt,),
            in_specs=[pl.BlockSpec((1, 1, TOP_K * DISPATCH_ROWS), lambda i, pe: (i, 0, 0),
                                   memory_space=pltpu.SMEM),
                      pl.BlockSpec((DISPATCH_ROWS * ROW_SLABS, LANES), lambda i, pe: (i, 0))],
            out_specs=pl.BlockSpec(memory_space=pl.ANY),
            scratch_shapes=[pltpu.VMEM((MOE_ROWS * ROW_SLABS, LANES), h_rows.dtype), pltpu.SemaphoreType.DMA(()),
                            pltpu.SemaphoreType.DMA(())]),
        out_shape=jax.ShapeDtypeStruct((rows * ROW_SLABS, LANES), h_rows.dtype),
        compiler_params=_cparams("arbitrary"),
        name="moe_dispatch",
    )(p_end, dest, h_rows)


def _expert_kernel(be_ref, nu_ref, x_ref, wg_ref, wu_ref, wd_ref, o_ref):
    i = pl.program_id(0)

    @pl.when(i < nu_ref[0])
    def _():
        xb = _unpack_bf16_pairs(_load_rows(x_ref, MOE_ROWS)).astype(BF16)
        hg = jnp.dot(xb, wg_ref[0], preferred_element_type=F32)
        hu = jnp.dot(xb, wu_ref[0], preferred_element_type=F32)
        hb = (hg * _sigmoid_tanh(hg) * hu).astype(BF16)
        _store_rows(o_ref, _pack_bf16_pairs(jnp.dot(hb, wd_ref[0], preferred_element_type=F32)))

    @pl.when(i >= nu_ref[0])
    def _():
        o_ref[...] = jnp.zeros_like(o_ref)


def moe_experts(xs, blk_exp, n_used, w_gate, w_up, w_down):
    D = D_MODEL
    blk = MOE_ROWS * ROW_SLABS
    nblk = xs.shape[0] // blk
    F = EXPERT_FF
    return pl.pallas_call(
        _expert_kernel,
        grid_spec=pltpu.PrefetchScalarGridSpec(
            num_scalar_prefetch=2,
            grid=(nblk,),
            in_specs=[pl.BlockSpec((blk, LANES), lambda i, be, nu: (jnp.minimum(i, nu[0] - 1), 0)),
                      pl.BlockSpec((1, D, F), lambda i, be, nu: (be[i], 0, 0)),
                      pl.BlockSpec((1, D, F), lambda i, be, nu: (be[i], 0, 0)),
                      pl.BlockSpec((1, F, D), lambda i, be, nu: (be[i], 0, 0))],
            out_specs=pl.BlockSpec((blk, LANES), lambda i, be, nu: (i, 0))),
        out_shape=jax.ShapeDtypeStruct(xs.shape, xs.dtype),
        compiler_params=_cparams("arbitrary"),
        name="moe_experts",
    )(blk_exp, n_used, xs, w_gate, w_up, w_down)


def _combine_kernel(dest_ref, dnext_ref, x_ref, gate_ref, eo_ref, o_ref, buf_ref, sem):
    i = pl.program_id(0)
    nt = pl.num_programs(0)
    n = x_ref.shape[0]
    slot = i % 2

    def gather(d_ref, s):
        def issue(r, c):
            for k in range(TOP_K):
                _row_copy(eo_ref, d_ref[0, 0, k * n + r], buf_ref.at[s, k], r, sem.at[s]).start()
            return c

        lax.fori_loop(0, n, issue, 0, unroll=DMA_UNROLL)

    @pl.when(i == 0)
    def _():
        gather(dest_ref, 0)

    @pl.when(i + 1 < nt)
    def _():
        gather(dnext_ref, 1 - slot)

    pltpu.make_async_copy(buf_ref.at[slot], buf_ref.at[slot], sem.at[slot]).wait()
    gate = gate_ref[...]
    ffn = gate[:, 0:1] * _unpack_bf16_pairs(_load_rows(buf_ref.at[slot, 0], n))
    for k in range(1, TOP_K):
        ffn = ffn + gate[:, k:k + 1] * _unpack_bf16_pairs(_load_rows(buf_ref.at[slot, k], n))
    o_ref[...] = x_ref[...] + ffn


def moe_combine(x2d, gates, dest, eo):
    T, D = x2d.shape
    nt = T // DMA_ROWS
    dest3 = dest
    dspec = lambda f: pl.BlockSpec((1, 1, TOP_K * DMA_ROWS), f, memory_space=pltpu.SMEM)
    return pl.pallas_call(
        _combine_kernel,
        grid=(nt,),
        in_specs=[dspec(lambda i: (i, 0, 0)),
                  dspec(lambda i: (jnp.minimum(i + 1, nt - 1), 0, 0)),
                  pl.BlockSpec((DMA_ROWS, D), lambda i: (i, 0)),
                  pl.BlockSpec((DMA_ROWS, ROUTE_LANES), lambda i: (i, 0)),
                  pl.BlockSpec(memory_space=pl.ANY)],
        out_specs=pl.BlockSpec((DMA_ROWS, D), lambda i: (i, 0)),
        out_shape=jax.ShapeDtypeStruct((T, D), F32),
        scratch_shapes=[pltpu.VMEM((2, TOP_K, DMA_ROWS * ROW_SLABS, LANES), eo.dtype),
                        pltpu.SemaphoreType.DMA((2,))],
        compiler_params=_cparams("arbitrary"),
        name="moe_combine",
    )(dest3, dest3, x2d, gates, eo)


def moe_layer(x, layer, norm_g, wg1, bg1, wg2, bg2, w_gate, w_up, w_down):
    B, S, D = x.shape
    T = B * S
    A = T * TOP_K
    x2d = x.reshape(T, D)
    h, route_i, route_g, counts = moe_router(x2d, norm_g, wg1, bg1, wg2, bg2)
    counts = counts[0, ROUTE_OFF:ROUTE_OFF + N_EXPERTS]
    padded = (counts + MOE_ROWS - 1) // MOE_ROWS * MOE_ROWS
    p_end = jnp.cumsum(padded)
    p_start = p_end - padded
    nt = T // ROW_TILE
    ri = route_i.reshape(nt, SUBLANES, ROW_TILE)
    sel = ri[:, 0:TOP_K, :, None] == jnp.arange(N_EXPERTS, dtype=jnp.int32)
    dest = jnp.sum(jnp.where(sel, p_start, 0), -1) + ri[:, TOP_K:2 * TOP_K, :]
    dest = dest.reshape(nt, 1, TOP_K * ROW_TILE).astype(jnp.int32)
    nblk = -(-A // MOE_ROWS) + N_EXPERTS
    blk_row = jnp.arange(nblk, dtype=jnp.int32) * MOE_ROWS
    blk_exp = jnp.minimum(jnp.sum(p_end[None, :] <= blk_row[:, None], -1), N_EXPERTS - 1).astype(jnp.int32)
    n_used = (p_end[-1:] // MOE_ROWS).astype(jnp.int32)
    xs = moe_dispatch(h, dest, p_end.astype(jnp.int32), nblk * MOE_ROWS)
    eo = moe_experts(xs, blk_exp + layer * N_EXPERTS, n_used, w_gate, w_up, w_down)
    return moe_combine(x2d, route_g, dest, eo).reshape(B, S, D)


def _trunk(x, p):
    x = recurrent_layer(x, p["norm_mix"][0], p["rec_w_in"][0], p["rg_conv_w"][0], p["rg_conv_b"][0],
                        p["rg_wa"][0], p["rg_ba"][0], p["rg_wx"][0], p["rg_bx"][0], p["rg_lambda"][0],
                        p["rw_mu_l"][0], p["rw_mu_r"][0], p["rw_w0"][0], p["rw_w_up"][0], p["rw_a0"][0],
                        p["rw_a_up"][0], p["rw_g_up"][0], p["rw_k_k"][0], p["rw_k_a"][0], p["rw_r_k"][0],
                        p["rw_ln_w"][0], p["rw_ln_b"][0], p["rec_w_out"][0])
    x = moe_layer(x, 0, p["norm_ffn"][0], p["moe_wg1"][0], p["moe_bg1"][0], p["moe_wg2"][0], p["moe_bg2"][0],
                  p["moe_w_gate"], p["moe_w_up"], p["moe_w_down"])
    x = attention_layer(x, p["norm_mix"][1], p["att_w_in"][0], p["att_q_norm"][0], p["att_k_norm"][0],
                        p["att_sink"][0], p["att_w_out"][0])
    x = moe_layer(x, 1, p["norm_ffn"][1], p["moe_wg1"][1], p["moe_bg1"][1], p["moe_wg2"][1], p["moe_bg2"][1],
                  p["moe_w_gate"], p["moe_w_up"], p["moe_w_down"])
    return x


def kernel(x_prompt, x_sample, norm_mix, norm_ffn, rec_w_in, rg_conv_w, rg_conv_b, rg_wa, rg_ba, rg_wx, rg_bx,
           rg_lambda, rw_mu_l, rw_mu_r, rw_w0, rw_w_up, rw_a0, rw_a_up, rw_g_up, rw_k_k, rw_k_a, rw_r_k,
           rw_ln_w, rw_ln_b, rec_w_out, att_w_in, att_q_norm, att_k_norm, att_sink, att_w_out, moe_wg1, moe_bg1,
           moe_wg2, moe_bg2, moe_w_gate, moe_w_up, moe_w_down):
    p = dict(norm_mix=norm_mix, norm_ffn=norm_ffn, rec_w_in=rec_w_in, rg_conv_w=rg_conv_w, rg_conv_b=rg_conv_b,
             rg_wa=rg_wa, rg_ba=rg_ba, rg_wx=rg_wx, rg_bx=rg_bx, rg_lambda=rg_lambda, rw_mu_l=rw_mu_l,
             rw_mu_r=rw_mu_r, rw_w0=rw_w0, rw_w_up=rw_w_up, rw_a0=rw_a0, rw_a_up=rw_a_up, rw_g_up=rw_g_up,
             rw_k_k=rw_k_k, rw_k_a=rw_k_a, rw_r_k=rw_r_k, rw_ln_w=rw_ln_w, rw_ln_b=rw_ln_b, rec_w_out=rec_w_out,
             att_w_in=att_w_in, att_q_norm=att_q_norm, att_k_norm=att_k_norm, att_sink=att_sink,
             att_w_out=att_w_out, moe_wg1=moe_wg1, moe_bg1=moe_bg1, moe_wg2=moe_wg2, moe_bg2=moe_bg2,
             moe_w_gate=moe_w_gate.astype(BF16).reshape(-1, D_MODEL, EXPERT_FF),
             moe_w_up=moe_w_up.astype(BF16).reshape(-1, D_MODEL, EXPERT_FF),
             moe_w_down=moe_w_down.astype(BF16).reshape(-1, EXPERT_FF, D_MODEL))
    return (_trunk(x_prompt, p), _trunk(x_sample, p))
```

```python
import functools
import math

import jax
import jax.numpy as jnp
from jax import lax
from jax.experimental import pallas as pl
from jax.experimental.pallas import tpu as pltpu

F32 = jnp.float32
BF16 = jnp.bfloat16

D_MODEL = 1024
RG_WIDTH = 512
RG_BLOCK_DIM = 64
CONV_WIDTH = 4
RG_C = 8.0
RW_HEAD_DIM = 64
RW_WIDTH = 512
DECAY_LORA = 32
ICL_LORA = 32
GATE_LORA = 64
RW_IN = 3 * RW_WIDTH + DECAY_LORA + ICL_LORA + GATE_LORA
RW_GN_EPS = 64e-5
ATT_HEADS = 16
ATT_KV_HEADS = 4
ATT_GROUP = 4
ATT_HEAD_DIM = 64
WINDOW = 128
ATT_BLOCK = 128
N_GROUPS = 4
EXPERTS_PER_GROUP = 8
N_EXPERTS = 32
TOP_K = 2
EXPERT_FF = 512
RMS_EPS = 1e-6
NEG_INF = -1e30

LANES = 128
SUBLANES = 8
HALO_ROWS = 16
VMEM_LIMIT_BYTES = 56 * 1024 * 1024
ROW_TILE = 512
WKV_TILE = 256
WKV_CHUNK = 64
MOE_ROWS = 512
ROUTE_LANES = 128
DMA_ROWS = ROW_TILE


def _cparams(*sem):
    return pltpu.CompilerParams(dimension_semantics=sem, vmem_limit_bytes=VMEM_LIMIT_BYTES)


def _dot(a, b):
    return jnp.dot(a.astype(BF16), b.astype(BF16), preferred_element_type=F32)


def _split2(x):
    h1 = x.astype(BF16)
    return h1, (x - h1.astype(F32)).astype(BF16)


def _dot_exact_lhs(e, x):
    h1, h2 = _split2(x)
    return jnp.dot(e, h1, preferred_element_type=F32) + jnp.dot(e, h2, preferred_element_type=F32)


def _sigmoid_tanh(x):
    return 0.5 * jnp.tanh(0.5 * x) + 0.5


def _pack_bf16_pairs(x):
    n = x.shape[1] // 2
    hi = lax.bitcast_convert_type(x[:, :n].astype(BF16).astype(F32), jnp.int32)
    lo = lax.bitcast_convert_type(x[:, n:].astype(BF16).astype(F32), jnp.int32)
    return hi | lax.shift_right_logical(lo, 16)


def _unpack_bf16_pairs(p):
    hi = lax.bitcast_convert_type(p & jnp.int32(-65536), F32)
    lo = lax.bitcast_convert_type(lax.shift_left(p, 16), F32)
    return jnp.concatenate([hi, lo], 1)


def _gelu_tanh(x):
    return 0.5 * x * (1.0 + jnp.tanh(math.sqrt(2.0 / math.pi) * (x + 0.044715 * (x * x * x))))


def _head_ones(width, head):
    r = lax.broadcasted_iota(jnp.int32, (width, width), 0) // head
    c = lax.broadcasted_iota(jnp.int32, (width, width), 1) // head
    return jnp.where(r == c, 1.0, 0.0).astype(BF16)


def _norm_proj_kernel(x_ref, g_ref, w_ref, *out_refs, splits):
    x = x_ref[...]
    h = x * lax.rsqrt(jnp.mean(x * x, -1, keepdims=True) + RMS_EPS) * g_ref[...]
    hb = h.astype(BF16)
    off = 0
    for o_ref, n in zip(out_refs, splits):
        o_ref[...] = jnp.dot(hb, w_ref[:, off:off + n], preferred_element_type=F32).astype(o_ref.dtype)
        off += n


def norm_proj(x2d, g, w, splits, out_dtype=F32):
    T, D = x2d.shape
    N = w.shape[1]
    assert sum(splits) == N and T % ROW_TILE == 0
    return pl.pallas_call(
        functools.partial(_norm_proj_kernel, splits=splits),
        grid=(T // ROW_TILE,),
        in_specs=[pl.BlockSpec((ROW_TILE, D), lambda i: (i, 0)),
                  pl.BlockSpec((1, D), lambda i: (0, 0)),
                  pl.BlockSpec((D, N), lambda i: (0, 0))],
        out_specs=[pl.BlockSpec((ROW_TILE, n), lambda i: (i, 0)) for n in splits],
        out_shape=[jax.ShapeDtypeStruct((T, n), out_dtype) for n in splits],
        compiler_params=_cparams("parallel"),
        name="norm_proj",
    )(x2d, g.reshape(1, D), w.astype(BF16))


RG_HALF = RG_WIDTH // 2
RG_ROWS = 256
RG_PAD = SUBLANES


def _rglru_kernel(x_ref, g_ref, cw_ref, cb_ref, wg_ref, bg_ref, sp_ref, o_ref,
                  xp_ref, af_ref, bf_ref, ab_ref, bb_ref):
    S = x_ref.shape[1]
    C = RG_HALF
    xp_ref[0:RG_PAD, :] = jnp.zeros((RG_PAD, C), F32)
    xp_ref[RG_PAD + S:RG_PAD + S + RG_PAD, :] = jnp.zeros((RG_PAD, C), F32)
    xp_ref[RG_PAD:RG_PAD + S, :] = x_ref[0].astype(F32)
    cw = cw_ref[...]
    left = CONV_WIDTH // 2
    for c in range(S // RG_ROWS):
        r0 = c * RG_ROWS
        xc = cb_ref[...] + cw[0:1] * xp_ref[RG_PAD + r0 - left:RG_PAD + r0 - left + RG_ROWS, :]
        for k in range(1, CONV_WIDTH):
            s0 = RG_PAD + r0 + k - left
            xc = xc + cw[k:k + 1] * xp_ref[s0:s0 + RG_ROWS, :]
        z = _dot(xc, wg_ref[0]) + bg_ref[0]
        for d, (a_ref, b_ref) in enumerate(((af_ref, bf_ref), (ab_ref, bb_ref))):
            r = _sigmoid_tanh(z[:, (2 * d) * C:(2 * d + 1) * C])
            i = _sigmoid_tanh(z[:, (2 * d + 1) * C:(2 * d + 2) * C])
            a = jnp.exp(-RG_C * r * sp_ref[0][:, d * C:(d + 1) * C])
            a_ref[r0:r0 + RG_ROWS, :] = a
            y = 1.0 - a * a
            b_ref[r0:r0 + RG_ROWS, :] = jnp.where(y > 0.0, y * lax.rsqrt(y), 0.0) * (i * xc)

    row8 = lax.broadcasted_iota(jnp.int32, (SUBLANES, C), 0)

    def tile_scan(a, b, carry, reverse):
        for s in (1, 2, 4):
            keep = (row8 < SUBLANES - s) if reverse else (row8 >= s)
            shift = SUBLANES - s if reverse else s
            b = b + a * jnp.where(keep, pltpu.roll(b, shift, 0), 0.0)
            a = a * jnp.where(keep, pltpu.roll(a, shift, 0), 1.0)
        h = b + a * carry
        last = 0 if reverse else SUBLANES - 1
        return h, h[last:last + 1]

    def body(n, carry):
        hf, hb = carry
        r0 = pl.multiple_of(n * SUBLANES, SUBLANES)
        h8, hf = tile_scan(af_ref[pl.ds(r0, SUBLANES), :], bf_ref[pl.ds(r0, SUBLANES), :], hf, False)
        bf_ref[pl.ds(r0, SUBLANES), :] = h8
        r1 = pl.multiple_of(S - SUBLANES - n * SUBLANES, SUBLANES)
        h8, hb = tile_scan(ab_ref[pl.ds(r1, SUBLANES), :], bb_ref[pl.ds(r1, SUBLANES), :], hb, True)
        bb_ref[pl.ds(r1, SUBLANES), :] = h8
        return hf, hb

    zero = jnp.zeros((1, C), F32)
    lax.fori_loop(0, S // SUBLANES, body, (zero, zero), unroll=2)
    for c in range(S // RG_ROWS):
        sl = slice(c * RG_ROWS, (c + 1) * RG_ROWS)
        o_ref[0, sl, :] = ((bf_ref[sl, :] + bb_ref[sl, :]) * _gelu_tanh(g_ref[0, sl, :].astype(F32))).astype(o_ref.dtype)


def rglru_branch(rg_x, rg_g, conv_w, conv_b, wa, ba, wx, bx, lam):
    B, S, _ = rg_x.shape
    C = RG_HALF
    nb = C // RG_BLOCK_DIM

    def bdiag(w):
        w = w.reshape(2, nb, RG_BLOCK_DIM, RG_BLOCK_DIM)
        eye = jnp.eye(nb, dtype=w.dtype)
        return jnp.einsum('hnij,nm->hnimj', w, eye).reshape(2, C, C)

    wg = jnp.concatenate([bdiag(wa[0]), bdiag(wx[0]), bdiag(wa[1]), bdiag(wx[1])], axis=-1).astype(BF16)

    def halves(v):
        return v.reshape(2, 1, C)

    bg = jnp.concatenate([halves(ba[0]), halves(bx[0]), halves(ba[1]), halves(bx[1])], axis=-1)
    sp = jax.nn.softplus(-lam.astype(F32))
    spg = jnp.concatenate([halves(sp[0]), halves(sp[1])], axis=-1)
    return pl.pallas_call(
        _rglru_kernel,
        grid=(B, 2),
        in_specs=[pl.BlockSpec((1, S, C), lambda b, c: (b, 0, c)),
                  pl.BlockSpec((1, S, C), lambda b, c: (b, 0, c)),
                  pl.BlockSpec((CONV_WIDTH, C), lambda b, c: (0, c)),
                  pl.BlockSpec((1, C), lambda b, c: (0, c)),
                  pl.BlockSpec((1, C, 4 * C), lambda b, c: (c, 0, 0)),
                  pl.BlockSpec((1, 1, 4 * C), lambda b, c: (c, 0, 0)),
                  pl.BlockSpec((1, 1, 2 * C), lambda b, c: (c, 0, 0))],
        out_specs=pl.BlockSpec((1, S, C), lambda b, c: (b, 0, c)),
        out_shape=jax.ShapeDtypeStruct((B, S, RG_WIDTH), BF16),
        scratch_shapes=[pltpu.VMEM((S + 2 * RG_PAD, C), F32)] + [pltpu.VMEM((S, C), F32)] * 4,
        compiler_params=_cparams("parallel", "parallel"),
        name="rglru",
    )(rg_x, rg_g, conv_w, conv_b.reshape(1, RG_WIDTH), wg, bg, spg)


def _rwkv_prep_kernel(u_ref, up_ref, un_ref, mul_ref, mur_ref, wl_ref, w0_ref, a0_ref, kk_ref, ka_ref,
                      rk_ref, rt_ref, kt_ref, bt_ref, kh_ref, v_ref, le_ref, bonus_ref, g_ref):
    i = pl.program_id(1)
    nt = pl.num_programs(1)
    TS = u_ref.shape[1]
    W = RW_WIDTH
    u = u_ref[0].astype(F32)
    prow = jnp.where(i == 0, 0.0, up_ref[0][HALO_ROWS - 1:HALO_ROWS, :].astype(F32))
    nrow = jnp.where(i == nt - 1, 0.0, un_ref[0][0:1, :].astype(F32))
    rows = lax.broadcasted_iota(jnp.int32, (TS, 1), 0)
    prev = jnp.where(rows == 0, prow, pltpu.roll(u, 1, 0))
    nxt = jnp.where(rows == TS - 1, nrow, pltpu.roll(u, TS - 1, 0))
    m = u + mul_ref[...] * (prev - u) + mur_ref[...] * (nxt - u)
    r = m[:, 0:W]
    k = m[:, W:2 * W]
    v = m[:, 2 * W:3 * W]
    tail = m[:, 3 * W:3 * W + LANES]
    lane = lax.broadcasted_iota(jnp.int32, (1, LANES), 1)
    z = jnp.where(lane < DECAY_LORA, jnp.tanh(tail),
                  jnp.where(lane < DECAY_LORA + ICL_LORA, tail, _sigmoid_tanh(tail)))
    lo = _dot(z, wl_ref[...])
    ones = _head_ones(W, RW_HEAD_DIM)
    kkr = k * kk_ref[...]
    kk = kkr * lax.rsqrt(jnp.maximum(_dot(kkr * kkr, ones), 1e-24))
    tr = lax.broadcasted_iota(jnp.int32, (TS, TS), 0)
    tc = lax.broadcasted_iota(jnp.int32, (TS, TS), 1)
    same = (tr // WKV_CHUNK) == (tc // WKV_CHUNK)
    kd_sum = jnp.zeros((TS, W), F32)
    nch = TS // WKV_CHUNK
    for d in range(2):
        ld = -math.exp(-0.5) * _sigmoid_tanh(w0_ref[d:d + 1, :] + lo[:, d * W:(d + 1) * W])
        a = _sigmoid_tanh(a0_ref[d:d + 1, :] + lo[:, (2 + d) * W:(3 + d) * W])
        kd = k * (1.0 + (a - 1.0) * ka_ref[...])
        kd_sum = kd_sum + kd
        tri = jnp.where(same & ((tc <= tr) if d == 0 else (tc >= tr)), 1.0, 0.0).astype(BF16)
        L = _dot_exact_lhs(tri, ld)
        en = jnp.exp(-L)
        rt_ref[d, 0] = (r * jnp.exp(L)).astype(BF16)
        kt_ref[d, 0] = (kk * jnp.exp(L - ld)).astype(BF16)
        bt_ref[d, 0] = (kk * a * en).astype(BF16)
        kh_ref[d, 0] = (kd * en).astype(BF16)
        ends = [L[(c + 1) * WKV_CHUNK - 1:(c + 1) * WKV_CHUNK] if d == 0 else L[c * WKV_CHUNK:c * WKV_CHUNK + 1]
                for c in range(nch)]
        le_ref[d, 0, 0] = jnp.concatenate(ends, 0)
    v_ref[0] = v.astype(BF16)
    bonus_ref[0] = (_dot(r * kd_sum * rk_ref[...], ones) * v).astype(BF16)
    g_ref[0] = lo[:, 4 * W:5 * W].astype(BF16)


def rwkv_prep(rw, mu_l, mu_r, w0, w_up, a0, a_up, g_up, k_k, k_a, r_k):
    B, S, _ = rw.shape
    TS = WKV_TILE
    W = RW_WIDTH
    nt = S // TS
    nch = TS // WKV_CHUNK
    hb = TS // HALO_ROWS
    wl = jnp.zeros((LANES, 5 * W), F32)
    wl = wl.at[0:DECAY_LORA, 0:W].set(w_up[0]).at[0:DECAY_LORA, W:2 * W].set(w_up[1])
    o = DECAY_LORA
    wl = wl.at[o:o + ICL_LORA, 2 * W:3 * W].set(a_up[0]).at[o:o + ICL_LORA, 3 * W:4 * W].set(a_up[1])
    o += ICL_LORA
    wl = wl.at[o:o + GATE_LORA, 4 * W:5 * W].set(g_up)
    vec = lambda n: pl.BlockSpec((1, n), lambda b, i: (0, 0))
    big = lambda: pl.BlockSpec((2, 1, TS, W), lambda b, i: (0, b, i, 0))
    one = lambda: pl.BlockSpec((1, TS, W), lambda b, i: (b, i, 0))
    return pl.pallas_call(
        _rwkv_prep_kernel,
        grid=(B, nt),
        in_specs=[pl.BlockSpec((1, TS, RW_IN), lambda b, i: (b, i, 0)),
                  pl.BlockSpec((1, HALO_ROWS, RW_IN), lambda b, i: (b, jnp.maximum(i * hb - 1, 0), 0)),
                  pl.BlockSpec((1, HALO_ROWS, RW_IN), lambda b, i: (b, jnp.minimum((i + 1) * hb, S // HALO_ROWS - 1), 0)),
                  vec(RW_IN), vec(RW_IN),
                  pl.BlockSpec((LANES, 5 * W), lambda b, i: (0, 0)),
                  pl.BlockSpec((2, W), lambda b, i: (0, 0)),
                  pl.BlockSpec((2, W), lambda b, i: (0, 0)),
                  vec(W), vec(W), vec(W)],
        out_specs=[big(), big(), big(), big(), one(),
                   pl.BlockSpec((2, 1, 1, nch, W), lambda b, i: (0, b, i, 0, 0)),
                   one(), one()],
        out_shape=[jax.ShapeDtypeStruct((2, B, S, W), BF16)] * 4
        + [jax.ShapeDtypeStruct((B, S, W), BF16),
           jax.ShapeDtypeStruct((2, B, nt, nch, W), F32),
           jax.ShapeDtypeStruct((B, S, W), BF16),
           jax.ShapeDtypeStruct((B, S, W), BF16)],
        compiler_params=_cparams("parallel", "parallel"),
        name="rwkv_prep",
    )(rw, rw, rw, mu_l.reshape(1, RW_IN), mu_r.reshape(1, RW_IN), wl.astype(BF16), w0, a0,
      k_k.reshape(1, W), k_a.reshape(1, W), r_k.reshape(1, W))


def _wkv_tiles(probs):
    TS = WKV_TILE
    C = WKV_CHUNK
    N = RW_HEAD_DIM
    P = 2 * N
    nch = TS // C
    zero = jnp.zeros((), BF16)
    head0 = lax.broadcasted_iota(jnp.int32, (1, P), 1) < N
    head0_2 = (lax.broadcasted_iota(jnp.int32, (1, 2 * P), 1) % P) < N
    head0_w = (lax.broadcasted_iota(jnp.int32, (1, nch * P), 1) % P) < N
    own = (lax.broadcasted_iota(jnp.int32, (TS, nch * P), 0) // C
           == lax.broadcasted_iota(jnp.int32, (TS, nch * P), 1) // P)
    tq = lax.broadcasted_iota(jnp.int32, (C, TS), 0)
    sq = lax.broadcasted_iota(jnp.int32, (C, TS), 1) % C
    blk = (lax.broadcasted_iota(jnp.int32, (TS, TS), 0) // C
           == lax.broadcasted_iota(jnp.int32, (TS, TS), 1) // C)
    hr = lax.broadcasted_iota(jnp.int32, (P, P), 0)
    hc = lax.broadcasted_iota(jnp.int32, (P, P), 1)
    bdiag = (hr // N) == (hc // N)
    heye = hr == hc

    def both_heads(xb, m):
        return jnp.concatenate([jnp.where(m, xb, zero), jnp.where(m, zero, xb)], 0)

    def wide(xb):
        return jnp.concatenate([xb[c * C:(c + 1) * C] for c in range(nch)], 1)

    def expand(xw):
        return jnp.where(blk, jnp.concatenate([xw] * nch, 0), zero)

    st = []
    for (rt, kt, bt, kh, v, le, H, reverse) in probs:
        ktb = kt.astype(BF16)
        vb = v.astype(BF16)
        ktw = wide(ktb)
        rtw = wide(rt.astype(BF16))
        lhs = jnp.concatenate([jnp.where(head0_w, ktw, zero), jnp.where(head0_w, rtw, zero),
                               jnp.where(head0_w, zero, ktw), jnp.where(head0_w, zero, rtw)], 0)
        rhs = jnp.concatenate([jnp.where(own, jnp.concatenate([bt.astype(BF16)] * nch, 1), zero),
                               jnp.where(own, jnp.concatenate([kh.astype(BF16)] * nch, 1), zero)], 0)
        gram = lax.dot_general(lhs, rhs, (((1,), (1,)), ((), ())), preferred_element_type=F32)
        st.append(dict(ktb=ktb, vb=vb, gram=gram, vm=both_heads(vb, head0)))

    chains = []
    for s, prob in zip(st, probs):
        reverse = prob[7]
        strict = (sq > tq) if reverse else (sq < tq)
        incl = (sq >= tq) if reverse else (sq <= tq)
        s["b_bd"], s["rb_bd"], s["rk_bd"] = [], [], []
        for h in range(2):
            g0 = s["gram"][2 * h * C:(2 * h + 1) * C]
            g1 = s["gram"][(2 * h + 1) * C:(2 * h + 2) * C]
            A = jnp.where(strict, g0[:, 0:TS], 0.0)
            s["b_bd"].append(expand(jnp.where(strict, g0[:, TS:2 * TS], 0.0).astype(BF16)))
            s["rb_bd"].append(expand(jnp.where(incl, g1[:, 0:TS], 0.0).astype(BF16)))
            s["rk_bd"].append(expand(jnp.where(incl, g1[:, TS:2 * TS], 0.0).astype(BF16)))
            chains.append(dict(Tw=jnp.where(sq == tq, 1.0, 0.0) - A, Ab=A.astype(BF16)))
    for ch in chains:
        ch["Q"] = jnp.dot(ch["Ab"], expand(ch["Ab"]), preferred_element_type=F32)
    for _ in range(int(math.log2(C)) - 2):
        for ch in chains:
            Qb = ch["Q"].astype(BF16)
            out = jnp.dot(jnp.concatenate([ch["Tw"].astype(BF16), Qb], 0), expand(Qb), preferred_element_type=F32)
            ch["Tw"] = ch["Tw"] + out[0:C]
            ch["Q"] = out[C:2 * C]
    for ch in chains:
        Tw = ch["Tw"] + jnp.dot(ch["Tw"].astype(BF16), expand(ch["Q"].astype(BF16)), preferred_element_type=F32)
        ch["t_bd"] = expand(Tw.astype(BF16))

    for n, s in enumerate(st):
        s["Bv"] = jnp.dot(jnp.concatenate(s["b_bd"], 1), s["vm"], preferred_element_type=F32)
    for n, s in enumerate(st):
        wm = both_heads(jnp.concatenate([s["ktb"], s["Bv"].astype(BF16)], 1), head0_2)
        t_bd = [chains[2 * n]["t_bd"], chains[2 * n + 1]["t_bd"]]
        s["x1"] = jnp.dot(jnp.concatenate(t_bd, 1), wm, preferred_element_type=F32)
    for s, prob in zip(st, probs):
        x1 = s["x1"]
        rhs2 = jnp.concatenate([both_heads(-x1.astype(BF16), head0_2),
                                jnp.concatenate([jnp.zeros((2 * TS, P), BF16), s["vm"]], 1)], 0)
        x2 = jnp.dot(jnp.concatenate(s["rb_bd"] + s["rk_bd"], 1), rhs2, preferred_element_type=F32)
        s["K2"] = x1[:, 0:P]
        s["V2"] = x1[:, P:2 * P]
        s["R2"] = prob[0] + x2[:, 0:P]
        s["Y2"] = x2[:, P:2 * P]
        s["H"] = prob[6]
        s["pc"] = jnp.exp(prob[5])
        s["ys"] = [None] * nch

    for ci in range(nch):
        for s, prob in zip(st, probs):
            (rt, kt, bt, kh, v, le, _, reverse) = prob
            c = nch - 1 - ci if reverse else ci
            sl = slice(c * C, (c + 1) * C)
            pc = s["pc"][c:c + 1]
            bh = (bt[sl] * pc).T
            khh = (kh[sl] * pc).T
            Mc = jnp.where(heye, pc, 0.0) - jnp.where(bdiag, _dot(bh, s["K2"][sl]), 0.0)
            Gc = jnp.where(bdiag, _dot(jnp.concatenate([khh, -bh], 1),
                                       jnp.concatenate([v[sl], s["V2"][sl]], 0)), 0.0)
            s["ys"][c] = _dot(s["R2"][sl], s["H"]) + s["Y2"][sl]
            s["H"] = _dot(Mc, s["H"]) + Gc
    return [(jnp.concatenate(s["ys"], 0), s["H"]) for s in st]


WKV_PAIRS = 4


def _wkv_kernel(rtf_ref, ktf_ref, btf_ref, khf_ref, vf_ref, lef_ref,
                rtb_ref, ktb_ref, btb_ref, khb_ref, vb_ref, leb_ref, yf_ref, yb_ref, h_ref):
    i = pl.program_id(2)

    @pl.when(i == 0)
    def _():
        h_ref[...] = jnp.zeros_like(h_ref)

    P = 2 * RW_HEAD_DIM
    dirs = ((rtf_ref, ktf_ref, btf_ref, khf_ref, vf_ref, lef_ref, yf_ref),
            (rtb_ref, ktb_ref, btb_ref, khb_ref, vb_ref, leb_ref, yb_ref))
    probs, outs = [], []
    for d, (rt_ref, kt_ref, bt_ref, kh_ref, v_ref, le_ref, y_ref) in enumerate(dirs):
        for pr in range(WKV_PAIRS):
            ln = slice(pr * P, (pr + 1) * P)
            probs.append((rt_ref[0, 0, :, ln], kt_ref[0, 0, :, ln], bt_ref[0, 0, :, ln], kh_ref[0, 0, :, ln],
                          v_ref[0, :, ln], le_ref[0, 0, 0, :, ln], h_ref[d, pr], d == 1))
            outs.append((y_ref, d, pr, ln))
    for (y, H), (y_ref, d, pr, ln) in zip(_wkv_tiles(probs), outs):
        y_ref[0, :, ln] = y.astype(y_ref.dtype)
        h_ref[d, pr] = H


def wkv_scan(rt, kt, bt, kh, v, le):
    _, B, S, W = rt.shape
    TS = WKV_TILE
    nt = S // TS
    nch = TS // WKV_CHUNK
    P = 2 * RW_HEAD_DIM
    PW = WKV_PAIRS * P
    tiles = (lambda i: i, lambda i: nt - 1 - i)

    def dir_specs(d):
        t = tiles[d]
        big = lambda: pl.BlockSpec((1, 1, TS, PW), lambda b, p, i: (d, b, t(i), p))
        return [big(), big(), big(), big(),
                pl.BlockSpec((1, TS, PW), lambda b, p, i: (b, t(i), p)),
                pl.BlockSpec((1, 1, 1, nch, PW), lambda b, p, i: (d, b, t(i), 0, p))]

    return pl.pallas_call(
        _wkv_kernel,
        grid=(B, W // PW, nt),
        in_specs=dir_specs(0) + dir_specs(1),
        out_specs=[pl.BlockSpec((1, TS, PW), lambda b, p, i: (b, tiles[0](i), p)),
                   pl.BlockSpec((1, TS, PW), lambda b, p, i: (b, tiles[1](i), p))],
        out_shape=[jax.ShapeDtypeStruct((B, S, W), BF16)] * 2,
        scratch_shapes=[pltpu.VMEM((2, WKV_PAIRS, P, P), F32)],
        compiler_params=_cparams("parallel", "parallel", "arbitrary"),
        name="wkv",
    )(rt, kt, bt, kh, v, le, rt, kt, bt, kh, v, le)


def _rec_out_kernel(x_ref, rg_ref, yf_ref, yb_ref, bonus_ref, g_ref, lnw_ref, lnb_ref, w_ref, o_ref):
    W = RW_WIDTH
    ones = _head_ones(W, RW_HEAD_DIM)
    y = yf_ref[...].astype(F32) + yb_ref[...].astype(F32)
    inv_n = 1.0 / RW_HEAD_DIM
    mu = _dot(y, ones) * inv_n
    yc = y - mu
    var = _dot(yc * yc, ones) * inv_n
    yn = yc * lax.rsqrt(var + RW_GN_EPS) * lnw_ref[...] + lnb_ref[...]
    rw_out = (yn + bonus_ref[...]) * g_ref[...]
    mix = (jnp.dot(rg_ref[...], w_ref[0:RG_WIDTH, :], preferred_element_type=F32)
           + jnp.dot(rw_out.astype(BF16), w_ref[RG_WIDTH:RG_WIDTH + W, :], preferred_element_type=F32))
    o_ref[...] = x_ref[...] + mix


def rec_out(x2d, rg_out, y_f, y_b, bonus, g, ln_w, ln_b, w_out):
    T, D = x2d.shape
    W = RW_WIDTH
    row = lambda n: pl.BlockSpec((ROW_TILE, n), lambda i: (i, 0))
    vec = lambda n: pl.BlockSpec((1, n), lambda i: (0, 0))
    return pl.pallas_call(
        _rec_out_kernel,
        grid=(T // ROW_TILE,),
        in_specs=[row(D), row(RG_WIDTH), row(W), row(W), row(W), row(W), vec(W), vec(W),
                  pl.BlockSpec((RG_WIDTH + W, D), lambda i: (0, 0))],
        out_specs=row(D),
        out_shape=jax.ShapeDtypeStruct((T, D), F32),
        compiler_params=_cparams("parallel"),
        name="rec_out",
    )(x2d, rg_out, y_f, y_b, bonus, g, ln_w.reshape(1, W), ln_b.reshape(1, W), w_out.astype(BF16))


def recurrent_layer(x, norm_g, w_in, conv_w, conv_b, rg_wa, rg_ba, rg_wx, rg_bx, rg_lambda,
                    mu_l, mu_r, w0, w_up, a0, a_up, g_up, k_k, k_a, r_k, ln_w, ln_b, w_out):
    B, S, D = x.shape
    T = B * S
    x2d = x.reshape(T, D)
    rg_x, rg_g, rw = norm_proj(x2d, norm_g, w_in, (RG_WIDTH, RG_WIDTH, RW_IN), out_dtype=BF16)
    rg_out = rglru_branch(rg_x.reshape(B, S, RG_WIDTH), rg_g.reshape(B, S, RG_WIDTH), conv_w, conv_b,
                          rg_wa, rg_ba, rg_wx, rg_bx, rg_lambda)
    rt, kt, bt, kh, v, le, bonus, g = rwkv_prep(rw.reshape(B, S, RW_IN), mu_l, mu_r, w0, w_up, a0, a_up,
                                                g_up, k_k, k_a, r_k.reshape(-1))
    y_f, y_b = wkv_scan(rt, kt, bt, kh, v, le)
    flat = lambda t: t.reshape(T, -1)
    out = rec_out(x2d, flat(rg_out), flat(y_f), flat(y_b), flat(bonus), flat(g), ln_w, ln_b, w_out)
    return out.reshape(B, S, D)


def _alibi_slope(h):
    return 2.0 ** (-8.0 * (h + 1) / ATT_HEADS)


ATT_PAIR = 2 * ATT_HEAD_DIM
ATT_KVW = ATT_KV_HEADS * ATT_PAIR


def _qkv_proj_kernel(x_ref, g_ref, w_ref, eq_ref, ek_ref, qg_ref, kg_ref, q_ref, k_ref, v_ref):
    QW = ATT_HEADS * ATT_HEAD_DIM
    x = x_ref[...]
    h = (x * lax.rsqrt(jnp.mean(x * x, -1, keepdims=True) + RMS_EPS) * g_ref[...]).astype(BF16)
    q = jnp.dot(h, w_ref[:, 0:QW], preferred_element_type=F32)
    msq = jnp.dot((q * q).astype(BF16), eq_ref[...], preferred_element_type=F32) * (1.0 / ATT_HEAD_DIM)
    q_ref[...] = (q * lax.rsqrt(msq + RMS_EPS) * qg_ref[...]).astype(BF16)
    k = jnp.dot(h, w_ref[:, QW:QW + ATT_KVW], preferred_element_type=F32)
    msk = jnp.dot((k * k).astype(BF16), ek_ref[...], preferred_element_type=F32) * (1.0 / ATT_PAIR)
    k_ref[...] = (k * lax.rsqrt(msk + RMS_EPS) * kg_ref[...]).astype(BF16)
    v_ref[...] = jnp.dot(h, w_ref[:, QW + ATT_KVW:QW + 2 * ATT_KVW], preferred_element_type=F32).astype(BF16)


def qkv_proj(x2d, g, w_in, q_norm, k_norm):
    T, D = x2d.shape
    QW = ATT_HEADS * ATT_HEAD_DIM
    KW = ATT_KV_HEADS * ATT_HEAD_DIM
    dup = lambda w: jnp.concatenate([w.reshape(D, ATT_KV_HEADS, 1, ATT_HEAD_DIM)] * 2, 2).reshape(D, ATT_KVW)
    w = jnp.concatenate([w_in[:, :QW], dup(w_in[:, QW:QW + KW]), dup(w_in[:, QW + KW:])], 1).astype(BF16)
    N = QW + 2 * ATT_KVW
    qg = jnp.tile(q_norm.astype(F32), ATT_HEADS).reshape(1, QW) * (ATT_HEAD_DIM ** -0.5)
    kg = jnp.tile(k_norm.astype(F32), 2 * ATT_KV_HEADS).reshape(1, ATT_KVW)
    row = lambda n: pl.BlockSpec((ROW_TILE, n), lambda i: (i, 0))
    fixed = lambda r, n: pl.BlockSpec((r, n), lambda i: (0, 0))
    return pl.pallas_call(
        _qkv_proj_kernel,
        grid=(T // ROW_TILE,),
        in_specs=[row(D), fixed(1, D), fixed(D, N), fixed(QW, QW), fixed(ATT_KVW, ATT_KVW),
                  fixed(1, QW), fixed(1, ATT_KVW)],
        out_specs=[row(QW), row(ATT_KVW), row(ATT_KVW)],
        out_shape=[jax.ShapeDtypeStruct((T, QW), BF16), jax.ShapeDtypeStruct((T, ATT_KVW), BF16),
                   jax.ShapeDtypeStruct((T, ATT_KVW), BF16)],
        compiler_params=_cparams("parallel"),
        name="qkv_proj",
    )(x2d, g.reshape(1, D), w, _head_ones(QW, ATT_HEAD_DIM), _head_ones(ATT_KVW, ATT_PAIR), qg, kg)


def _attn_kernel(sink_ref, x_ref, q_ref, kp_ref, kc_ref, kn_ref, vp_ref, vc_ref, vn_ref, bias_ref,
                 w_ref, o_ref):
    i = pl.program_id(1)
    nb = pl.num_programs(1)
    BLK = ATT_BLOCK
    P = ATT_PAIR
    span = 3 * BLK
    kc = jnp.concatenate([kp_ref[0], kc_ref[0], kn_ref[0]], 0)
    vc = jnp.concatenate([vp_ref[0], vc_ref[0], vn_ref[0]], 0)
    kpos = lax.broadcasted_iota(jnp.int32, (1, span), 1) + (i - 1) * BLK
    edge = jnp.where((kpos >= 0) & (kpos < nb * BLK), 0.0, NEG_INF)
    lane = lax.broadcasted_iota(jnp.int32, (1, P), 1)
    low = lane < ATT_HEAD_DIM
    zero = jnp.zeros((), BF16)
    ones = jnp.ones((span, P), BF16)
    slabs = []
    for g in range(ATT_KV_HEADS):
        kg = kc[:, g * P:(g + 1) * P]
        k_half = (jnp.where(low, kg, zero), jnp.where(low, zero, kg))
        v_ext = jnp.concatenate([vc[:, g * P:(g + 1) * P], ones], 1)
        for pr in range(ATT_GROUP // 2):
            slab = g * (ATT_GROUP // 2) + pr
            qp = q_ref[0, :, slab * P:(slab + 1) * P]
            halves = []
            for hf in range(2):
                h = 2 * slab + hf
                s = lax.dot_general(qp, k_half[hf], (((1,), (1,)), ((), ())), preferred_element_type=F32)
                s = s + bias_ref[h] + edge
                sk = sink_ref[h]
                m = jnp.maximum(jnp.max(s, -1, keepdims=True), sk)
                p = jnp.exp(s - m).astype(BF16)
                o = jnp.dot(p, v_ext, preferred_element_type=F32)
                halves.append(o[:, 0:P] / (o[:, P:2 * P] + jnp.exp(sk - m)))
            slabs.append(jnp.where(low, halves[0], halves[1]))
    o = jnp.concatenate(slabs, -1).astype(BF16)
    o_ref[0] = x_ref[0] + jnp.dot(o, w_ref[...], preferred_element_type=F32)


def attention_layer(x, norm_g, w_in, q_norm, k_norm, sink, w_out):
    B, S, D = x.shape
    T = B * S
    QW = ATT_HEADS * ATT_HEAD_DIM
    q, k, v = qkv_proj(x.reshape(T, D), norm_g, w_in, q_norm, k_norm)
    q = q.reshape(B, S, QW)
    k = k.reshape(B, S, ATT_KVW)
    v = v.reshape(B, S, ATT_KVW)
    nb = S // ATT_BLOCK
    span = 3 * ATT_BLOCK
    rel = (jnp.arange(span)[None, :] - WINDOW) - jnp.arange(ATT_BLOCK)[:, None]
    slopes = jnp.asarray([_alibi_slope(h) for h in range(ATT_HEADS)], F32)
    bias = jnp.where(jnp.abs(rel) <= WINDOW, -slopes[:, None, None] * jnp.abs(rel).astype(F32), NEG_INF)
    prev = lambda b, i: (b, jnp.maximum(i - 1, 0), 0)
    cur = lambda b, i: (b, i, 0)
    nxt = lambda b, i: (b, jnp.minimum(i + 1, nb - 1), 0)
    kv = lambda f: pl.BlockSpec((1, ATT_BLOCK, ATT_KVW), f)
    return pl.pallas_call(
        _attn_kernel,
        grid=(B, nb),
        in_specs=[pl.BlockSpec(memory_space=pltpu.SMEM),
                  pl.BlockSpec((1, ATT_BLOCK, D), cur),
                  pl.BlockSpec((1, ATT_BLOCK, QW), cur),
                  kv(prev), kv(cur), kv(nxt), kv(prev), kv(cur), kv(nxt),
                  pl.BlockSpec((ATT_HEADS, ATT_BLOCK, span), lambda b, i: (0, 0, 0)),
                  pl.BlockSpec((QW, D), lambda b, i: (0, 0))],
        out_specs=pl.BlockSpec((1, ATT_BLOCK, D), cur),
        out_shape=jax.ShapeDtypeStruct((B, S, D), F32),
        compiler_params=_cparams("parallel", "parallel"),
        name="window_attn",
    )(sink.astype(F32), x, q, k, k, k, v, v, v, bias, w_out.astype(BF16))


ROUTE_OFF = N_GROUPS


def _router_kernel(x_ref, g_ref, w12_ref, b_ref, h_ref, ri_ref, rg_ref, cnt_ref, run_ref):
    i = pl.program_id(0)

    @pl.when(i == 0)
    def _():
        run_ref[...] = jnp.zeros_like(run_ref)

    TM = x_ref.shape[0]
    x = x_ref[...]
    h = x * lax.rsqrt(jnp.mean(x * x, -1, keepdims=True) + RMS_EPS) * g_ref[...]
    _store_rows(h_ref, _pack_bf16_pairs(h))
    h1 = h.astype(BF16)
    h2 = (h - h1.astype(F32)).astype(BF16)
    hw = jnp.dot(h1, w12_ref[...], preferred_element_type=F32)
    lg = (hw[:, 0:ROUTE_LANES] + hw[:, ROUTE_LANES:2 * ROUTE_LANES]
          + jnp.dot(h2, w12_ref[:, 0:ROUTE_LANES], preferred_element_type=F32)) + b_ref[...]
    lane_i = lax.broadcasted_iota(jnp.int32, (1, ROUTE_LANES), 1)
    lane = lane_i.astype(F32)
    lane_group = ((lane_i - ROUTE_OFF + EXPERTS_PER_GROUP) // EXPERTS_PER_GROUP - 1).astype(F32)
    far = float(ROUTE_LANES)
    gmask = lane_i < N_GROUPS
    gl = jnp.where(gmask, lg, NEG_INF)
    gm = jnp.max(gl, -1, keepdims=True)
    p_group = 1.0 / jnp.sum(jnp.where(gmask, jnp.exp(gl - gm), 0.0), -1, keepdims=True)
    group = jnp.min(jnp.where(gl == gm, lane, far), -1, keepdims=True)
    fmask = lane_group == group
    fl = jnp.where(fmask, lg, NEG_INF)
    m1 = jnp.max(fl, -1, keepdims=True)
    ssum = jnp.sum(jnp.where(fmask, jnp.exp(fl - m1), 0.0), -1, keepdims=True)
    i1 = jnp.min(jnp.where(fl == m1, lane, far), -1, keepdims=True)
    fl2 = jnp.where(lane == i1, NEG_INF, fl)
    m2 = jnp.max(fl2, -1, keepdims=True)
    i2 = jnp.min(jnp.where(fl2 == m2, lane, far), -1, keepdims=True)
    p1 = 1.0 / ssum
    p2 = jnp.exp(m2 - m1) / ssum
    norm = p_group / (p1 + p2)
    oh = jnp.where((lane == i1) | (lane == i2), 1.0, 0.0)
    tr = lax.broadcasted_iota(jnp.int32, (TM, TM), 0)
    tc = lax.broadcasted_iota(jnp.int32, (TM, TM), 1)
    before = jnp.where(tc < tr, 1.0, 0.0).astype(BF16)
    pre = jnp.dot(before, oh.astype(BF16), preferred_element_type=F32) + run_ref[...]
    rank1 = jnp.sum(jnp.where(lane == i1, pre, 0.0), -1, keepdims=True)
    rank2 = jnp.sum(jnp.where(lane == i2, pre, 0.0), -1, keepdims=True)
    total = run_ref[...] + jnp.sum(oh, 0, keepdims=True)
    run_ref[...] = total
    cnt_ref[...] = total.astype(jnp.int32)
    cols = jnp.where(lane_i == 0, i1 - ROUTE_OFF,
                     jnp.where(lane_i == 1, i2 - ROUTE_OFF,
                               jnp.where(lane_i == 2, rank1, jnp.where(lane_i == 3, rank2, 0.0))))
    ri_ref[...] = jnp.transpose(cols)[0:SUBLANES, :].astype(jnp.int32)
    rg_ref[...] = jnp.where(lane_i == 0, p1 * norm, jnp.where(lane_i == 1, p2 * norm, 0.0))


def moe_router(x2d, g, wg1, bg1, wg2, bg2):
    T, D = x2d.shape
    wr = jnp.zeros((D, ROUTE_LANES), F32)
    wr = wr.at[:, 0:N_GROUPS].set(wg1)
    wr = wr.at[:, ROUTE_OFF:ROUTE_OFF + N_EXPERTS].set(jnp.moveaxis(wg2, 0, 1).reshape(D, N_EXPERTS))
    w1 = wr.astype(BF16)
    w2 = (wr - w1.astype(F32)).astype(BF16)
    bias = jnp.zeros((1, ROUTE_LANES), F32)
    bias = bias.at[0, 0:N_GROUPS].set(bg1).at[0, ROUTE_OFF:ROUTE_OFF + N_EXPERTS].set(bg2.reshape(-1))
    row = lambda n: pl.BlockSpec((ROW_TILE, n), lambda i: (i, 0))
    fixed = lambda r, n: pl.BlockSpec((r, n), lambda i: (0, 0))
    return pl.pallas_call(
        _router_kernel,
        grid=(T // ROW_TILE,),
        in_specs=[row(D), fixed(1, D), fixed(D, 2 * ROUTE_LANES), fixed(1, ROUTE_LANES)],
        out_specs=[pl.BlockSpec((ROW_TILE * ROW_SLABS, LANES), lambda i: (i, 0)),
                   pl.BlockSpec((SUBLANES, ROW_TILE), lambda i: (i, 0)), row(ROUTE_LANES), fixed(1, ROUTE_LANES)],
        out_shape=[jax.ShapeDtypeStruct((T * ROW_SLABS, LANES), jnp.int32),
                   jax.ShapeDtypeStruct((T // ROW_TILE * SUBLANES, ROW_TILE), jnp.int32),
                   jax.ShapeDtypeStruct((T, ROUTE_LANES), F32),
                   jax.ShapeDtypeStruct((1, ROUTE_LANES), jnp.int32)],
        scratch_shapes=[pltpu.VMEM((1, ROUTE_LANES), F32)],
        compiler_params=_cparams("arbitrary"),
        name="moe_router",
    )(x2d, g.reshape(1, D), jnp.concatenate([w1, w2], 1), bias)


ROW_SLABS = (D_MODEL // 2) // LANES


def _store_rows(ref, packed):
    n = packed.shape[0]
    for j in range(ROW_SLABS):
        ref[pl.ds(j, n, stride=ROW_SLABS), :] = packed[:, j * LANES:(j + 1) * LANES]


def _load_rows(ref, n):
    return jnp.concatenate([ref[pl.ds(j, n, stride=ROW_SLABS), :] for j in range(ROW_SLABS)], 1)


def _row_copy(src, src_row, dst, dst_row, sem):
    return pltpu.make_async_copy(src.at[pl.ds(src_row * ROW_SLABS, ROW_SLABS)],
                                 dst.at[pl.ds(dst_row * ROW_SLABS, ROW_SLABS)], sem)


DMA_UNROLL = 8


def _rows_wait(ref, nrows, sem):
    pltpu.make_async_copy(ref.at[pl.ds(0, nrows * ROW_SLABS)], ref.at[pl.ds(0, nrows * ROW_SLABS)], sem).wait()


DISPATCH_ROWS = DMA_ROWS


def _dispatch_kernel(pend_ref, dest_ref, h_ref, xs_ref, zero_ref, sem, zsem):
    i = pl.program_id(0)
    n = h_ref.shape[0] // ROW_SLABS
    blk = MOE_ROWS * ROW_SLABS
    nblk = xs_ref.shape[0] // blk

    @pl.when(i == 0)
    def _():
        zero_ref[...] = jnp.zeros_like(zero_ref)

        def block_copy(row0):
            return pltpu.make_async_copy(zero_ref, xs_ref.at[pl.ds(pl.multiple_of(row0 * ROW_SLABS, blk), blk)], zsem)

        tails = [jnp.maximum(pend_ref[e] - MOE_ROWS, 0) for e in range(N_EXPERTS)]
        for t in tails:
            block_copy(t).start()
        for t in tails:
            block_copy(t).wait()
        first_unused = pend_ref[N_EXPERTS - 1] // MOE_ROWS

        def clear(j, c):
            block_copy(j * MOE_ROWS).start()
            block_copy(j * MOE_ROWS).wait()
            return c

        lax.fori_loop(first_unused, nblk, clear, 0)

    def issue(r, c):
        for k in range(TOP_K):
            _row_copy(h_ref, r, xs_ref, dest_ref[0, 0, k * n + r], sem).start()
        return c

    lax.fori_loop(0, n, issue, 0, unroll=DMA_UNROLL)
    _rows_wait(xs_ref, TOP_K * n, sem)


def moe_dispatch(h_rows, dest, p_end, rows):
    T = h_rows.shape[0] // ROW_SLABS
    nt = T // DISPATCH_ROWS
    return pl.pallas_call(
        _dispatch_kernel,
        grid_spec=pltpu.PrefetchScalarGridSpec(
            num_scalar_prefetch=1,
            grid=(nt,),
            in_specs=[pl.BlockSpec((1, 1, TOP_K * DISPATCH_ROWS), lambda i, pe: (i, 0, 0),
                                   memory_space=pltpu.SMEM),
                      pl.BlockSpec((DISPATCH_ROWS * ROW_SLABS, LANES), lambda i, pe: (i, 0))],
            out_specs=pl.BlockSpec(memory_space=pl.ANY),
            scratch_shapes=[pltpu.VMEM((MOE_ROWS * ROW_SLABS, LANES), h_rows.dtype), pltpu.SemaphoreType.DMA(()),
                            pltpu.SemaphoreType.DMA(())]),
        out_shape=jax.ShapeDtypeStruct((rows * ROW_SLABS, LANES), h_rows.dtype),
        compiler_params=_cparams("arbitrary"),
        name="moe_dispatch",
    )(p_end, dest, h_rows)


def _expert_kernel(be_ref, nu_ref, x_ref, wg_ref, wu_ref, wd_ref, o_ref):
    i = pl.program_id(0)

    @pl.when(i < nu_ref[0])
    def _():
        xb = _unpack_bf16_pairs(_load_rows(x_ref, MOE_ROWS)).astype(BF16)
        hg = jnp.dot(xb, wg_ref[0], preferred_element_type=F32)
        hu = jnp.dot(xb, wu_ref[0], preferred_element_type=F32)
        hb = (hg * _sigmoid_tanh(hg) * hu).astype(BF16)
        _store_rows(o_ref, _pack_bf16_pairs(jnp.dot(hb, wd_ref[0], preferred_element_type=F32)))

    @pl.when(i >= nu_ref[0])
    def _():
        o_ref[...] = jnp.zeros_like(o_ref)


def moe_experts(xs, blk_exp, n_used, w_gate, w_up, w_down):
    D = D_MODEL
    blk = MOE_ROWS * ROW_SLABS
    nblk = xs.shape[0] // blk
    F = EXPERT_FF
    return pl.pallas_call(
        _expert_kernel,
        grid_spec=pltpu.PrefetchScalarGridSpec(
            num_scalar_prefetch=2,
            grid=(nblk,),
            in_specs=[pl.BlockSpec((blk, LANES), lambda i, be, nu: (jnp.minimum(i, nu[0] - 1), 0)),
                      pl.BlockSpec((1, D, F), lambda i, be, nu: (be[i], 0, 0)),
                      pl.BlockSpec((1, D, F), lambda i, be, nu: (be[i], 0, 0)),
                      pl.BlockSpec((1, F, D), lambda i, be, nu: (be[i], 0, 0))],
            out_specs=pl.BlockSpec((blk, LANES), lambda i, be, nu: (i, 0))),
        out_shape=jax.ShapeDtypeStruct(xs.shape, xs.dtype),
        compiler_params=_cparams("arbitrary"),
        name="moe_experts",
    )(blk_exp, n_used, xs, w_gate, w_up, w_down)


def _combine_kernel(dest_ref, dnext_ref, x_ref, gate_ref, eo_ref, o_ref, buf_ref, sem):
    i = pl.program_id(0)
    nt = pl.num_programs(0)
    n = x_ref.shape[0]
    slot = i % 2

    def gather(d_ref, s):
        def issue(r, c):
            for k in range(TOP_K):
                _row_copy(eo_ref, d_ref[0, 0, k * n + r], buf_ref.at[s, k], r, sem.at[s]).start()
            return c

        lax.fori_loop(0, n, issue, 0, unroll=DMA_UNROLL)

    @pl.when(i == 0)
    def _():
        gather(dest_ref, 0)

    @pl.when(i + 1 < nt)
    def _():
        gather(dnext_ref, 1 - slot)

    pltpu.make_async_copy(buf_ref.at[slot], buf_ref.at[slot], sem.at[slot]).wait()
    gate = gate_ref[...]
    ffn = gate[:, 0:1] * _unpack_bf16_pairs(_load_rows(buf_ref.at[slot, 0], n))
    for k in range(1, TOP_K):
        ffn = ffn + gate[:, k:k + 1] * _unpack_bf16_pairs(_load_rows(buf_ref.at[slot, k], n))
    o_ref[...] = x_ref[...] + ffn


def moe_combine(x2d, gates, dest, eo):
    T, D = x2d.shape
    nt = T // DMA_ROWS
    dest3 = dest
    dspec = lambda f: pl.BlockSpec((1, 1, TOP_K * DMA_ROWS), f, memory_space=pltpu.SMEM)
    return pl.pallas_call(
        _combine_kernel,
        grid=(nt,),
        in_specs=[dspec(lambda i: (i, 0, 0)),
                  dspec(lambda i: (jnp.minimum(i + 1, nt - 1), 0, 0)),
                  pl.BlockSpec((DMA_ROWS, D), lambda i: (i, 0)),
                  pl.BlockSpec((DMA_ROWS, ROUTE_LANES), lambda i: (i, 0)),
                  pl.BlockSpec(memory_space=pl.ANY)],
        out_specs=pl.BlockSpec((DMA_ROWS, D), lambda i: (i, 0)),
        out_shape=jax.ShapeDtypeStruct((T, D), F32),
        scratch_shapes=[pltpu.VMEM((2, TOP_K, DMA_ROWS * ROW_SLABS, LANES), eo.dtype),
                        pltpu.SemaphoreType.DMA((2,))],
        compiler_params=_cparams("arbitrary"),
        name="moe_combine",
    )(dest3, dest3, x2d, gates, eo)


def moe_layer(x, layer, norm_g, wg1, bg1, wg2, bg2, w_gate, w_up, w_down):
    B, S, D = x.shape
    T = B * S
    A = T * TOP_K
    x2d = x.reshape(T, D)
    h, route_i, route_g, counts = moe_router(x2d, norm_g, wg1, bg1, wg2, bg2)
    counts = counts[0, ROUTE_OFF:ROUTE_OFF + N_EXPERTS]
    padded = (counts + MOE_ROWS - 1) // MOE_ROWS * MOE_ROWS
    p_end = jnp.cumsum(padded)
    p_start = p_end - padded
    nt = T // ROW_TILE
    ri = route_i.reshape(nt, SUBLANES, ROW_TILE)
    sel = ri[:, 0:TOP_K, :, None] == jnp.arange(N_EXPERTS, dtype=jnp.int32)
    dest = jnp.sum(jnp.where(sel, p_start, 0), -1) + ri[:, TOP_K:2 * TOP_K, :]
    dest = dest.reshape(nt, 1, TOP_K * ROW_TILE).astype(jnp.int32)
    nblk = -(-A // MOE_ROWS) + N_EXPERTS
    blk_row = jnp.arange(nblk, dtype=jnp.int32) * MOE_ROWS
    blk_exp = jnp.minimum(jnp.sum(p_end[None, :] <= blk_row[:, None], -1), N_EXPERTS - 1).astype(jnp.int32)
    n_used = (p_end[-1:] // MOE_ROWS).astype(jnp.int32)
    xs = moe_dispatch(h, dest, p_end.astype(jnp.int32), nblk * MOE_ROWS)
    eo = moe_experts(xs, blk_exp + layer * N_EXPERTS, n_used, w_gate, w_up, w_down)
    return moe_combine(x2d, route_g, dest, eo).reshape(B, S, D)


def _trunk(x, p):
    x = recurrent_layer(x, p["norm_mix"][0], p["rec_w_in"][0], p["rg_conv_w"][0], p["rg_conv_b"][0],
                        p["rg_wa"][0], p["rg_ba"][0], p["rg_wx"][0], p["rg_bx"][0], p["rg_lambda"][0],
                        p["rw_mu_l"][0], p["rw_mu_r"][0], p["rw_w0"][0], p["rw_w_up"][0], p["rw_a0"][0],
                        p["rw_a_up"][0], p["rw_g_up"][0], p["rw_k_k"][0], p["rw_k_a"][0], p["rw_r_k"][0],
                        p["rw_ln_w"][0], p["rw_ln_b"][0], p["rec_w_out"][0])
    x = moe_layer(x, 0, p["norm_ffn"][0], p["moe_wg1"][0], p["moe_bg1"][0], p["moe_wg2"][0], p["moe_bg2"][0],
                  p["moe_w_gate"], p["moe_w_up"], p["moe_w_down"])
    x = attention_layer(x, p["norm_mix"][1], p["att_w_in"][0], p["att_q_norm"][0], p["att_k_norm"][0],
                        p["att_sink"][0], p["att_w_out"][0])
    x = moe_layer(x, 1, p["norm_ffn"][1], p["moe_wg1"][1], p["moe_bg1"][1], p["moe_wg2"][1], p["moe_bg2"][1],
                  p["moe_w_gate"], p["moe_w_up"], p["moe_w_down"])
    return x


def kernel(x_prompt, x_sample, norm_mix, norm_ffn, rec_w_in, rg_conv_w, rg_conv_b, rg_wa, rg_ba, rg_wx, rg_bx,
           rg_lambda, rw_mu_l, rw_mu_r, rw_w0, rw_w_up, rw_a0, rw_a_up, rw_g_up, rw_k_k, rw_k_a, rw_r_k,
           rw_ln_w, rw_ln_b, rec_w_out, att_w_in, att_q_norm, att_k_norm, att_sink, att_w_out, moe_wg1, moe_bg1,
           moe_wg2, moe_bg2, moe_w_gate, moe_w_up, moe_w_down):
    p = dict(norm_mix=norm_mix, norm_ffn=norm_ffn, rec_w_in=rec_w_in, rg_conv_w=rg_conv_w, rg_conv_b=rg_conv_b,
             rg_wa=rg_wa, rg_ba=rg_ba, rg_wx=rg_wx, rg_bx=rg_bx, rg_lambda=rg_lambda, rw_mu_l=rw_mu_l,
             rw_mu_r=rw_mu_r, rw_w0=rw_w0, rw_w_up=rw_w_up, rw_a0=rw_a0, rw_a_up=rw_a_up, rw_g_up=rw_g_up,
             rw_k_k=rw_k_k, rw_k_a=rw_k_a, rw_r_k=rw_r_k, rw_ln_w=rw_ln_w, rw_ln_b=rw_ln_b, rec_w_out=rec_w_out,
             att_w_in=att_w_in, att_q_norm=att_q_norm, att_k_norm=att_k_norm, att_sink=att_sink,
             att_w_out=att_w_out, moe_wg1=moe_wg1, moe_bg1=moe_bg1, moe_wg2=moe_wg2, moe_bg2=moe_bg2,
             moe_w_gate=moe_w_gate.astype(BF16).reshape(-1, D_MODEL, EXPERT_FF),
             moe_w_up=moe_w_up.astype(BF16).reshape(-1, D_MODEL, EXPERT_FF),
             moe_w_down=moe_w_down.astype(BF16).reshape(-1, EXPERT_FF, D_MODEL))
    return (_trunk(x_prompt, p), _trunk(x_sample, p))
```

```python
import functools
import math

import jax
import jax.numpy as jnp
from jax import lax
from jax.experimental import pallas as pl
from jax.experimental.pallas import tpu as pltpu

F32 = jnp.float32
BF16 = jnp.bfloat16

D_MODEL = 1024
RG_WIDTH = 512
RG_BLOCK_DIM = 64
CONV_WIDTH = 4
RG_C = 8.0
RW_HEAD_DIM = 64
RW_WIDTH = 512
DECAY_LORA = 32
ICL_LORA = 32
GATE_LORA = 64
RW_IN = 3 * RW_WIDTH + DECAY_LORA + ICL_LORA + GATE_LORA
RW_GN_EPS = 64e-5
ATT_HEADS = 16
ATT_KV_HEADS = 4
ATT_GROUP = 4
ATT_HEAD_DIM = 64
WINDOW = 128
ATT_BLOCK = 128
N_GROUPS = 4
EXPERTS_PER_GROUP = 8
N_EXPERTS = 32
TOP_K = 2
EXPERT_FF = 512
RMS_EPS = 1e-6
NEG_INF = -1e30

LANES = 128
SUBLANES = 8
HALO_ROWS = 16
VMEM_LIMIT_BYTES = 56 * 1024 * 1024
ROW_TILE = 512
WKV_TILE = 256
WKV_CHUNK = 64
MOE_ROWS = 512
ROUTE_LANES = 128
DMA_ROWS = ROW_TILE


def _cparams(*sem):
    return pltpu.CompilerParams(dimension_semantics=sem, vmem_limit_bytes=VMEM_LIMIT_BYTES)


def _dot(a, b):
    return jnp.dot(a.astype(BF16), b.astype(BF16), preferred_element_type=F32)


def _split2(x):
    h1 = x.astype(BF16)
    return h1, (x - h1.astype(F32)).astype(BF16)


def _dot_exact_lhs(e, x):
    h1, h2 = _split2(x)
    return jnp.dot(e, h1, preferred_element_type=F32) + jnp.dot(e, h2, preferred_element_type=F32)


def _sigmoid_tanh(x):
    return 0.5 * jnp.tanh(0.5 * x) + 0.5


def _pack_bf16_pairs(x):
    n = x.shape[1] // 2
    hi = lax.bitcast_convert_type(x[:, :n].astype(BF16).astype(F32), jnp.int32)
    lo = lax.bitcast_convert_type(x[:, n:].astype(BF16).astype(F32), jnp.int32)
    return hi | lax.shift_right_logical(lo, 16)


def _unpack_bf16_pairs(p):
    hi = lax.bitcast_convert_type(p & jnp.int32(-65536), F32)
    lo = lax.bitcast_convert_type(lax.shift_left(p, 16), F32)
    return jnp.concatenate([hi, lo], 1)


def _gelu_tanh(x):
    return 0.5 * x * (1.0 + jnp.tanh(math.sqrt(2.0 / math.pi) * (x + 0.044715 * (x * x * x))))


def _head_ones(width, head):
    r = lax.broadcasted_iota(jnp.int32, (width, width), 0) // head
    c = lax.broadcasted_iota(jnp.int32, (width, width), 1) // head
    return jnp.where(r == c, 1.0, 0.0).astype(BF16)


def _norm_proj_kernel(x_ref, g_ref, w_ref, *out_refs, splits):
    x = x_ref[...]
    h = x * lax.rsqrt(jnp.mean(x * x, -1, keepdims=True) + RMS_EPS) * g_ref[...]
    hb = h.astype(BF16)
    off = 0
    for o_ref, n in zip(out_refs, splits):
        o_ref[...] = jnp.dot(hb, w_ref[:, off:off + n], preferred_element_type=F32).astype(o_ref.dtype)
        off += n


def norm_proj(x2d, g, w, splits, out_dtype=F32):
    T, D = x2d.shape
    N = w.shape[1]
    assert sum(splits) == N and T % ROW_TILE == 0
    return pl.pallas_call(
        functools.partial(_norm_proj_kernel, splits=splits),
        grid=(T // ROW_TILE,),
        in_specs=[pl.BlockSpec((ROW_TILE, D), lambda i: (i, 0)),
                  pl.BlockSpec((1, D), lambda i: (0, 0)),
                  pl.BlockSpec((D, N), lambda i: (0, 0))],
        out_specs=[pl.BlockSpec((ROW_TILE, n), lambda i: (i, 0)) for n in splits],
        out_shape=[jax.ShapeDtypeStruct((T, n), out_dtype) for n in splits],
        compiler_params=_cparams("parallel"),
        name="norm_proj",
    )(x2d, g.reshape(1, D), w.astype(BF16))


RG_HALF = RG_WIDTH // 2
RG_ROWS = 256
RG_PAD = SUBLANES


def _rglru_kernel(x_ref, g_ref, cw_ref, cb_ref, wg_ref, bg_ref, sp_ref, o_ref,
                  xp_ref, af_ref, bf_ref, ab_ref, bb_ref):
    S = x_ref.shape[1]
    C = RG_HALF
    xp_ref[0:RG_PAD, :] = jnp.zeros((RG_PAD, C), F32)
    xp_ref[RG_PAD + S:RG_PAD + S + RG_PAD, :] = jnp.zeros((RG_PAD, C), F32)
    xp_ref[RG_PAD:RG_PAD + S, :] = x_ref[0].astype(F32)
    cw = cw_ref[...]
    left = CONV_WIDTH // 2
    for c in range(S // RG_ROWS):
        r0 = c * RG_ROWS
        xc = cb_ref[...] + cw[0:1] * xp_ref[RG_PAD + r0 - left:RG_PAD + r0 - left + RG_ROWS, :]
        for k in range(1, CONV_WIDTH):
            s0 = RG_PAD + r0 + k - left
            xc = xc + cw[k:k + 1] * xp_ref[s0:s0 + RG_ROWS, :]
        z = _dot(xc, wg_ref[0]) + bg_ref[0]
        for d, (a_ref, b_ref) in enumerate(((af_ref, bf_ref), (ab_ref, bb_ref))):
            r = _sigmoid_tanh(z[:, (2 * d) * C:(2 * d + 1) * C])
            i = _sigmoid_tanh(z[:, (2 * d + 1) * C:(2 * d + 2) * C])
            a = jnp.exp(-RG_C * r * sp_ref[0][:, d * C:(d + 1) * C])
            a_ref[r0:r0 + RG_ROWS, :] = a
            y = 1.0 - a * a
            b_ref[r0:r0 + RG_ROWS, :] = jnp.where(y > 0.0, y * lax.rsqrt(y), 0.0) * (i * xc)

    row8 = lax.broadcasted_iota(jnp.int32, (SUBLANES, C), 0)

    def tile_scan(a, b, carry, reverse):
        for s in (1, 2, 4):
            keep = (row8 < SUBLANES - s) if reverse else (row8 >= s)
            shift = SUBLANES - s if reverse else s
            b = b + a * jnp.where(keep, pltpu.roll(b, shift, 0), 0.0)
            a = a * jnp.where(keep, pltpu.roll(a, shift, 0), 1.0)
        h = b + a * carry
        last = 0 if reverse else SUBLANES - 1
        return h, h[last:last + 1]

    def body(n, carry):
        hf, hb = carry
        r0 = pl.multiple_of(n * SUBLANES, SUBLANES)
        h8, hf = tile_scan(af_ref[pl.ds(r0, SUBLANES), :], bf_ref[pl.ds(r0, SUBLANES), :], hf, False)
        bf_ref[pl.ds(r0, SUBLANES), :] = h8
        r1 = pl.multiple_of(S - SUBLANES - n * SUBLANES, SUBLANES)
        h8, hb = tile_scan(ab_ref[pl.ds(r1, SUBLANES), :], bb_ref[pl.ds(r1, SUBLANES), :], hb, True)
        bb_ref[pl.ds(r1, SUBLANES), :] = h8
        return hf, hb

    zero = jnp.zeros((1, C), F32)
    lax.fori_loop(0, S // SUBLANES, body, (zero, zero), unroll=2)
    for c in range(S // RG_ROWS):
        sl = slice(c * RG_ROWS, (c + 1) * RG_ROWS)
        o_ref[0, sl, :] = ((bf_ref[sl, :] + bb_ref[sl, :]) * _gelu_tanh(g_ref[0, sl, :].astype(F32))).astype(o_ref.dtype)


def rglru_branch(rg_x, rg_g, conv_w, conv_b, wa, ba, wx, bx, lam):
    B, S, _ = rg_x.shape
    C = RG_HALF
    nb = C // RG_BLOCK_DIM

    def bdiag(w):
        w = w.reshape(2, nb, RG_BLOCK_DIM, RG_BLOCK_DIM)
        eye = jnp.eye(nb, dtype=w.dtype)
        return jnp.einsum('hnij,nm->hnimj', w, eye).reshape(2, C, C)

    wg = jnp.concatenate([bdiag(wa[0]), bdiag(wx[0]), bdiag(wa[1]), bdiag(wx[1])], axis=-1).astype(BF16)

    def halves(v):
        return v.reshape(2, 1, C)

    bg = jnp.concatenate([halves(ba[0]), halves(bx[0]), halves(ba[1]), halves(bx[1])], axis=-1)
    sp = jax.nn.softplus(-lam.astype(F32))
    spg = jnp.concatenate([halves(sp[0]), halves(sp[1])], axis=-1)
    return pl.pallas_call(
        _rglru_kernel,
        grid=(B, 2),
        in_specs=[pl.BlockSpec((1, S, C), lambda b, c: (b, 0, c)),
                  pl.BlockSpec((1, S, C), lambda b, c: (b, 0, c)),
                  pl.BlockSpec((CONV_WIDTH, C), lambda b, c: (0, c)),
                  pl.BlockSpec((1, C), lambda b, c: (0, c)),
                  pl.BlockSpec((1, C, 4 * C), lambda b, c: (c, 0, 0)),
                  pl.BlockSpec((1, 1, 4 * C), lambda b, c: (c, 0, 0)),
                  pl.BlockSpec((1, 1, 2 * C), lambda b, c: (c, 0, 0))],
        out_specs=pl.BlockSpec((1, S, C), lambda b, c: (b, 0, c)),
        out_shape=jax.ShapeDtypeStruct((B, S, RG_WIDTH), BF16),
        scratch_shapes=[pltpu.VMEM((S + 2 * RG_PAD, C), F32)] + [pltpu.VMEM((S, C), F32)] * 4,
        compiler_params=_cparams("parallel", "parallel"),
        name="rglru",
    )(rg_x, rg_g, conv_w, conv_b.reshape(1, RG_WIDTH), wg, bg, spg)


def _rwkv_prep_kernel(u_ref, up_ref, un_ref, mul_ref, mur_ref, wl_ref, w0_ref, a0_ref, kk_ref, ka_ref,
                      rk_ref, rt_ref, kt_ref, bt_ref, kh_ref, v_ref, le_ref, bonus_ref, g_ref):
    i = pl.program_id(1)
    nt = pl.num_programs(1)
    TS = u_ref.shape[1]
    W = RW_WIDTH
    u = u_ref[0].astype(F32)
    prow = jnp.where(i == 0, 0.0, up_ref[0][HALO_ROWS - 1:HALO_ROWS, :].astype(F32))
    nrow = jnp.where(i == nt - 1, 0.0, un_ref[0][0:1, :].astype(F32))
    rows = lax.broadcasted_iota(jnp.int32, (TS, 1), 0)
    prev = jnp.where(rows == 0, prow, pltpu.roll(u, 1, 0))
    nxt = jnp.where(rows == TS - 1, nrow, pltpu.roll(u, TS - 1, 0))
    m = u + mul_ref[...] * (prev - u) + mur_ref[...] * (nxt - u)
    r = m[:, 0:W]
    k = m[:, W:2 * W]
    v = m[:, 2 * W:3 * W]
    tail = m[:, 3 * W:3 * W + LANES]
    lane = lax.broadcasted_iota(jnp.int32, (1, LANES), 1)
    z = jnp.where(lane < DECAY_LORA, jnp.tanh(tail),
                  jnp.where(lane < DECAY_LORA + ICL_LORA, tail, _sigmoid_tanh(tail)))
    lo = _dot(z, wl_ref[...])
    ones = _head_ones(W, RW_HEAD_DIM)
    kkr = k * kk_ref[...]
    kk = kkr * lax.rsqrt(jnp.maximum(_dot(kkr * kkr, ones), 1e-24))
    tr = lax.broadcasted_iota(jnp.int32, (TS, TS), 0)
    tc = lax.broadcasted_iota(jnp.int32, (TS, TS), 1)
    same = (tr // WKV_CHUNK) == (tc // WKV_CHUNK)
    kd_sum = jnp.zeros((TS, W), F32)
    nch = TS // WKV_CHUNK
    for d in range(2):
        ld = -math.exp(-0.5) * _sigmoid_tanh(w0_ref[d:d + 1, :] + lo[:, d * W:(d + 1) * W])
        a = _sigmoid_tanh(a0_ref[d:d + 1, :] + lo[:, (2 + d) * W:(3 + d) * W])
        kd = k * (1.0 + (a - 1.0) * ka_ref[...])
        kd_sum = kd_sum + kd
        tri = jnp.where(same & ((tc <= tr) if d == 0 else (tc >= tr)), 1.0, 0.0).astype(BF16)
        L = _dot_exact_lhs(tri, ld)
        en = jnp.exp(-L)
        rt_ref[d, 0] = (r * jnp.exp(L)).astype(BF16)
        kt_ref[d, 0] = (kk * jnp.exp(L - ld)).astype(BF16)
        bt_ref[d, 0] = (kk * a * en).astype(BF16)
        kh_ref[d, 0] = (kd * en).astype(BF16)
        ends = [L[(c + 1) * WKV_CHUNK - 1:(c + 1) * WKV_CHUNK] if d == 0 else L[c * WKV_CHUNK:c * WKV_CHUNK + 1]
                for c in range(nch)]
        le_ref[d, 0, 0] = jnp.concatenate(ends, 0)
    v_ref[0] = v.astype(BF16)
    bonus_ref[0] = (_dot(r * kd_sum * rk_ref[...], ones) * v).astype(BF16)
    g_ref[0] = lo[:, 4 * W:5 * W].astype(BF16)


def rwkv_prep(rw, mu_l, mu_r, w0, w_up, a0, a_up, g_up, k_k, k_a, r_k):
    B, S, _ = rw.shape
    TS = WKV_TILE
    W = RW_WIDTH
    nt = S // TS
    nch = TS // WKV_CHUNK
    hb = TS // HALO_ROWS
    wl = jnp.zeros((LANES, 5 * W), F32)
    wl = wl.at[0:DECAY_LORA, 0:W].set(w_up[0]).at[0:DECAY_LORA, W:2 * W].set(w_up[1])
    o = DECAY_LORA
    wl = wl.at[o:o + ICL_LORA, 2 * W:3 * W].set(a_up[0]).at[o:o + ICL_LORA, 3 * W:4 * W].set(a_up[1])
    o += ICL_LORA
    wl = wl.at[o:o + GATE_LORA, 4 * W:5 * W].set(g_up)
    vec = lambda n: pl.BlockSpec((1, n), lambda b, i: (0, 0))
    big = lambda: pl.BlockSpec((2, 1, TS, W), lambda b, i: (0, b, i, 0))
    one = lambda: pl.BlockSpec((1, TS, W), lambda b, i: (b, i, 0))
    return pl.pallas_call(
        _rwkv_prep_kernel,
        grid=(B, nt),
        in_specs=[pl.BlockSpec((1, TS, RW_IN), lambda b, i: (b, i, 0)),
                  pl.BlockSpec((1, HALO_ROWS, RW_IN), lambda b, i: (b, jnp.maximum(i * hb - 1, 0), 0)),
                  pl.BlockSpec((1, HALO_ROWS, RW_IN), lambda b, i: (b, jnp.minimum((i + 1) * hb, S // HALO_ROWS - 1), 0)),
                  vec(RW_IN), vec(RW_IN),
                  pl.BlockSpec((LANES, 5 * W), lambda b, i: (0, 0)),
                  pl.BlockSpec((2, W), lambda b, i: (0, 0)),
                  pl.BlockSpec((2, W), lambda b, i: (0, 0)),
                  vec(W), vec(W), vec(W)],
        out_specs=[big(), big(), big(), big(), one(),
                   pl.BlockSpec((2, 1, 1, nch, W), lambda b, i: (0, b, i, 0, 0)),
                   one(), one()],
        out_shape=[jax.ShapeDtypeStruct((2, B, S, W), BF16)] * 4
        + [jax.ShapeDtypeStruct((B, S, W), BF16),
           jax.ShapeDtypeStruct((2, B, nt, nch, W), F32),
           jax.ShapeDtypeStruct((B, S, W), BF16),
           jax.ShapeDtypeStruct((B, S, W), BF16)],
        compiler_params=_cparams("parallel", "parallel"),
        name="rwkv_prep",
    )(rw, rw, rw, mu_l.reshape(1, RW_IN), mu_r.reshape(1, RW_IN), wl.astype(BF16), w0, a0,
      k_k.reshape(1, W), k_a.reshape(1, W), r_k.reshape(1, W))


def _wkv_tiles(probs):
    TS = WKV_TILE
    C = WKV_CHUNK
    N = RW_HEAD_DIM
    P = 2 * N
    nch = TS // C
    zero = jnp.zeros((), BF16)
    head0 = lax.broadcasted_iota(jnp.int32, (1, P), 1) < N
    head0_2 = (lax.broadcasted_iota(jnp.int32, (1, 2 * P), 1) % P) < N
    head0_w = (lax.broadcasted_iota(jnp.int32, (1, nch * P), 1) % P) < N
    own = (lax.broadcasted_iota(jnp.int32, (TS, nch * P), 0) // C
           == lax.broadcasted_iota(jnp.int32, (TS, nch * P), 1) // P)
    tq = lax.broadcasted_iota(jnp.int32, (C, TS), 0)
    sq = lax.broadcasted_iota(jnp.int32, (C, TS), 1) % C
    blk = (lax.broadcasted_iota(jnp.int32, (TS, TS), 0) // C
           == lax.broadcasted_iota(jnp.int32, (TS, TS), 1) // C)
    hr = lax.broadcasted_iota(jnp.int32, (P, P), 0)
    hc = lax.broadcasted_iota(jnp.int32, (P, P), 1)
    bdiag = (hr // N) == (hc // N)
    heye = hr == hc

    def both_heads(xb, m):
        return jnp.concatenate([jnp.where(m, xb, zero), jnp.where(m, zero, xb)], 0)

    def wide(xb):
        return jnp.concatenate([xb[c * C:(c + 1) * C] for c in range(nch)], 1)

    def expand(xw):
        return jnp.where(blk, jnp.concatenate([xw] * nch, 0), zero)

    st = []
    for (rt, kt, bt, kh, v, le, H, reverse) in probs:
        ktb = kt.astype(BF16)
        vb = v.astype(BF16)
        ktw = wide(ktb)
        rtw = wide(rt.astype(BF16))
        lhs = jnp.concatenate([jnp.where(head0_w, ktw, zero), jnp.where(head0_w, rtw, zero),
                               jnp.where(head0_w, zero, ktw), jnp.where(head0_w, zero, rtw)], 0)
        rhs = jnp.concatenate([jnp.where(own, jnp.concatenate([bt.astype(BF16)] * nch, 1), zero),
                               jnp.where(own, jnp.concatenate([kh.astype(BF16)] * nch, 1), zero)], 0)
        gram = lax.dot_general(lhs, rhs, (((1,), (1,)), ((), ())), preferred_element_type=F32)
        st.append(dict(ktb=ktb, vb=vb, gram=gram, vm=both_heads(vb, head0)))

    chains = []
    for s, prob in zip(st, probs):
        reverse = prob[7]
        strict = (sq > tq) if reverse else (sq < tq)
        incl = (sq >= tq) if reverse else (sq <= tq)
        s["b_bd"], s["rb_bd"], s["rk_bd"] = [], [], []
        for h in range(2):
            g0 = s["gram"][2 * h * C:(2 * h + 1) * C]
            g1 = s["gram"][(2 * h + 1) * C:(2 * h + 2) * C]
            A = jnp.where(strict, g0[:, 0:TS], 0.0)
            s["b_bd"].append(expand(jnp.where(strict, g0[:, TS:2 * TS], 0.0).astype(BF16)))
            s["rb_bd"].append(expand(jnp.where(incl, g1[:, 0:TS], 0.0).astype(BF16)))
            s["rk_bd"].append(expand(jnp.where(incl, g1[:, TS:2 * TS], 0.0).astype(BF16)))
            chains.append(dict(Tw=jnp.where(sq == tq, 1.0, 0.0) - A, Ab=A.astype(BF16)))
    for ch in chains:
        ch["Q"] = jnp.dot(ch["Ab"], expand(ch["Ab"]), preferred_element_type=F32)
    for _ in range(int(math.log2(C)) - 2):
        for ch in chains:
            Qb = ch["Q"].astype(BF16)
            out = jnp.dot(jnp.concatenate([ch["Tw"].astype(BF16), Qb], 0), expand(Qb), preferred_element_type=F32)
            ch["Tw"] = ch["Tw"] + out[0:C]
            ch["Q"] = out[C:2 * C]
    for ch in chains:
        Tw = ch["Tw"] + jnp.dot(ch["Tw"].astype(BF16), expand(ch["Q"].astype(BF16)), preferred_element_type=F32)
        ch["t_bd"] = expand(Tw.astype(BF16))

    for n, s in enumerate(st):
        s["Bv"] = jnp.dot(jnp.concatenate(s["b_bd"], 1), s["vm"], preferred_element_type=F32)
    for n, s in enumerate(st):
        wm = both_heads(jnp.concatenate([s["ktb"], s["Bv"].astype(BF16)], 1), head0_2)
        t_bd = [chains[2 * n]["t_bd"], chains[2 * n + 1]["t_bd"]]
        s["x1"] = jnp.dot(jnp.concatenate(t_bd, 1), wm, preferred_element_type=F32)
    for s, prob in zip(st, probs):
        x1 = s["x1"]
        rhs2 = jnp.concatenate([both_heads(-x1.astype(BF16), head0_2),
                                jnp.concatenate([jnp.zeros((2 * TS, P), BF16), s["vm"]], 1)], 0)
        x2 = jnp.dot(jnp.concatenate(s["rb_bd"] + s["rk_bd"], 1), rhs2, preferred_element_type=F32)
        s["K2"] = x1[:, 0:P]
        s["V2"] = x1[:, P:2 * P]
        s["R2"] = prob[0] + x2[:, 0:P]
        s["Y2"] = x2[:, P:2 * P]
        s["H"] = prob[6]
        s["pc"] = jnp.exp(prob[5])
        s["ys"] = [None] * nch

    for ci in range(nch):
        for s, prob in zip(st, probs):
            (rt, kt, bt, kh, v, le, _, reverse) = prob
            c = nch - 1 - ci if reverse else ci
            sl = slice(c * C, (c + 1) * C)
            pc = s["pc"][c:c + 1]
            bh = (bt[sl] * pc).T
            khh = (kh[sl] * pc).T
            Mc = jnp.where(heye, pc, 0.0) - jnp.where(bdiag, _dot(bh, s["K2"][sl]), 0.0)
            Gc = jnp.where(bdiag, _dot(jnp.concatenate([khh, -bh], 1),
                                       jnp.concatenate([v[sl], s["V2"][sl]], 0)), 0.0)
            s["ys"][c] = _dot(s["R2"][sl], s["H"]) + s["Y2"][sl]
            s["H"] = _dot(Mc, s["H"]) + Gc
    return [(jnp.concatenate(s["ys"], 0), s["H"]) for s in st]


WKV_PAIRS = 4


def _wkv_kernel(rtf_ref, ktf_ref, btf_ref, khf_ref, vf_ref, lef_ref,
                rtb_ref, ktb_ref, btb_ref, khb_ref, vb_ref, leb_ref, yf_ref, yb_ref, h_ref):
    i = pl.program_id(2)

    @pl.when(i == 0)
    def _():
        h_ref[...] = jnp.zeros_like(h_ref)

    P = 2 * RW_HEAD_DIM
    dirs = ((rtf_ref, ktf_ref, btf_ref, khf_ref, vf_ref, lef_ref, yf_ref),
            (rtb_ref, ktb_ref, btb_ref, khb_ref, vb_ref, leb_ref, yb_ref))
    probs, outs = [], []
    for d, (rt_ref, kt_ref, bt_ref, kh_ref, v_ref, le_ref, y_ref) in enumerate(dirs):
        for pr in range(WKV_PAIRS):
            ln = slice(pr * P, (pr + 1) * P)
            probs.append((rt_ref[0, 0, :, ln], kt_ref[0, 0, :, ln], bt_ref[0, 0, :, ln], kh_ref[0, 0, :, ln],
                          v_ref[0, :, ln], le_ref[0, 0, 0, :, ln], h_ref[d, pr], d == 1))
            outs.append((y_ref, d, pr, ln))
    for (y, H), (y_ref, d, pr, ln) in zip(_wkv_tiles(probs), outs):
        y_ref[0, :, ln] = y.astype(y_ref.dtype)
        h_ref[d, pr] = H


def wkv_scan(rt, kt, bt, kh, v, le):
    _, B, S, W = rt.shape
    TS = WKV_TILE
    nt = S // TS
    nch = TS // WKV_CHUNK
    P = 2 * RW_HEAD_DIM
    PW = WKV_PAIRS * P
    tiles = (lambda i: i, lambda i: nt - 1 - i)

    def dir_specs(d):
        t = tiles[d]
        big = lambda: pl.BlockSpec((1, 1, TS, PW), lambda b, p, i: (d, b, t(i), p))
        return [big(), big(), big(), big(),
                pl.BlockSpec((1, TS, PW), lambda b, p, i: (b, t(i), p)),
                pl.BlockSpec((1, 1, 1, nch, PW), lambda b, p, i: (d, b, t(i), 0, p))]

    return pl.pallas_call(
        _wkv_kernel,
        grid=(B, W // PW, nt),
        in_specs=dir_specs(0) + dir_specs(1),
        out_specs=[pl.BlockSpec((1, TS, PW), lambda b, p, i: (b, tiles[0](i), p)),
                   pl.BlockSpec((1, TS, PW), lambda b, p, i: (b, tiles[1](i), p))],
        out_shape=[jax.ShapeDtypeStruct((B, S, W), BF16)] * 2,
        scratch_shapes=[pltpu.VMEM((2, WKV_PAIRS, P, P), F32)],
        compiler_params=_cparams("parallel", "parallel", "arbitrary"),
        name="wkv",
    )(rt, kt, bt, kh, v, le, rt, kt, bt, kh, v, le)


def _rec_out_kernel(x_ref, rg_ref, yf_ref, yb_ref, bonus_ref, g_ref, lnw_ref, lnb_ref, w_ref, o_ref):
    W = RW_WIDTH
    ones = _head_ones(W, RW_HEAD_DIM)
    y = yf_ref[...].astype(F32) + yb_ref[...].astype(F32)
    inv_n = 1.0 / RW_HEAD_DIM
    mu = _dot(y, ones) * inv_n
    yc = y - mu
    var = _dot(yc * yc, ones) * inv_n
    yn = yc * lax.rsqrt(var + RW_GN_EPS) * lnw_ref[...] + lnb_ref[...]
    rw_out = (yn + bonus_ref[...]) * g_ref[...]
    mix = (jnp.dot(rg_ref[...], w_ref[0:RG_WIDTH, :], preferred_element_type=F32)
           + jnp.dot(rw_out.astype(BF16), w_ref[RG_WIDTH:RG_WIDTH + W, :], preferred_element_type=F32))
    o_ref[...] = x_ref[...] + mix


def rec_out(x2d, rg_out, y_f, y_b, bonus, g, ln_w, ln_b, w_out):
    T, D = x2d.shape
    W = RW_WIDTH
    row = lambda n: pl.BlockSpec((ROW_TILE, n), lambda i: (i, 0))
    vec = lambda n: pl.BlockSpec((1, n), lambda i: (0, 0))
    return pl.pallas_call(
        _rec_out_kernel,
        grid=(T // ROW_TILE,),
        in_specs=[row(D), row(RG_WIDTH), row(W), row(W), row(W), row(W), vec(W), vec(W),
                  pl.BlockSpec((RG_WIDTH + W, D), lambda i: (0, 0))],
        out_specs=row(D),
        out_shape=jax.ShapeDtypeStruct((T, D), F32),
        compiler_params=_cparams("parallel"),
        name="rec_out",
    )(x2d, rg_out, y_f, y_b, bonus, g, ln_w.reshape(1, W), ln_b.reshape(1, W), w_out.astype(BF16))


def recurrent_layer(x, norm_g, w_in, conv_w, conv_b, rg_wa, rg_ba, rg_wx, rg_bx, rg_lambda,
                    mu_l, mu_r, w0, w_up, a0, a_up, g_up, k_k, k_a, r_k, ln_w, ln_b, w_out):
    B, S, D = x.shape
    T = B * S
    x2d = x.reshape(T, D)
    rg_x, rg_g, rw = norm_proj(x2d, norm_g, w_in, (RG_WIDTH, RG_WIDTH, RW_IN), out_dtype=BF16)
    rg_out = rglru_branch(rg_x.reshape(B, S, RG_WIDTH), rg_g.reshape(B, S, RG_WIDTH), conv_w, conv_b,
                          rg_wa, rg_ba, rg_wx, rg_bx, rg_lambda)
    rt, kt, bt, kh, v, le, bonus, g = rwkv_prep(rw.reshape(B, S, RW_IN), mu_l, mu_r, w0, w_up, a0, a_up,
                                                g_up, k_k, k_a, r_k.reshape(-1))
    y_f, y_b = wkv_scan(rt, kt, bt, kh, v, le)
    flat = lambda t: t.reshape(T, -1)
    out = rec_out(x2d, flat(rg_out), flat(y_f), flat(y_b), flat(bonus), flat(g), ln_w, ln_b, w_out)
    return out.reshape(B, S, D)


def _alibi_slope(h):
    return 2.0 ** (-8.0 * (h + 1) / ATT_HEADS)


ATT_PAIR = 2 * ATT_HEAD_DIM
ATT_KVW = ATT_KV_HEADS * ATT_PAIR


def _qkv_proj_kernel(x_ref, g_ref, w_ref, eq_ref, ek_ref, qg_ref, kg_ref, q_ref, k_ref, v_ref):
    QW = ATT_HEADS * ATT_HEAD_DIM
    x = x_ref[...]
    h = (x * lax.rsqrt(jnp.mean(x * x, -1, keepdims=True) + RMS_EPS) * g_ref[...]).astype(BF16)
    q = jnp.dot(h, w_ref[:, 0:QW], preferred_element_type=F32)
    msq = jnp.dot((q * q).astype(BF16), eq_ref[...], preferred_element_type=F32) * (1.0 / ATT_HEAD_DIM)
    q_ref[...] = (q * lax.rsqrt(msq + RMS_EPS) * qg_ref[...]).astype(BF16)
    k = jnp.dot(h, w_ref[:, QW:QW + ATT_KVW], preferred_element_type=F32)
    msk = jnp.dot((k * k).astype(BF16), ek_ref[...], preferred_element_type=F32) * (1.0 / ATT_PAIR)
    k_ref[...] = (k * lax.rsqrt(msk + RMS_EPS) * kg_ref[...]).astype(BF16)
    v_ref[...] = jnp.dot(h, w_ref[:, QW + ATT_KVW:QW + 2 * ATT_KVW], preferred_element_type=F32).astype(BF16)


def qkv_proj(x2d, g, w_in, q_norm, k_norm):
    T, D = x2d.shape
    QW = ATT_HEADS * ATT_HEAD_DIM
    KW = ATT_KV_HEADS * ATT_HEAD_DIM
    dup = lambda w: jnp.concatenate([w.reshape(D, ATT_KV_HEADS, 1, ATT_HEAD_DIM)] * 2, 2).reshape(D, ATT_KVW)
    w = jnp.concatenate([w_in[:, :QW], dup(w_in[:, QW:QW + KW]), dup(w_in[:, QW + KW:])], 1).astype(BF16)
    N = QW + 2 * ATT_KVW
    qg = jnp.tile(q_norm.astype(F32), ATT_HEADS).reshape(1, QW) * (ATT_HEAD_DIM ** -0.5)
    kg = jnp.tile(k_norm.astype(F32), 2 * ATT_KV_HEADS).reshape(1, ATT_KVW)
    row = lambda n: pl.BlockSpec((ROW_TILE, n), lambda i: (i, 0))
    fixed = lambda r, n: pl.BlockSpec((r, n), lambda i: (0, 0))
    return pl.pallas_call(
        _qkv_proj_kernel,
        grid=(T // ROW_TILE,),
        in_specs=[row(D), fixed(1, D), fixed(D, N), fixed(QW, QW), fixed(ATT_KVW, ATT_KVW),
                  fixed(1, QW), fixed(1, ATT_KVW)],
        out_specs=[row(QW), row(ATT_KVW), row(ATT_KVW)],
        out_shape=[jax.ShapeDtypeStruct((T, QW), BF16), jax.ShapeDtypeStruct((T, ATT_KVW), BF16),
                   jax.ShapeDtypeStruct((T, ATT_KVW), BF16)],
        compiler_params=_cparams("parallel"),
        name="qkv_proj",
    )(x2d, g.reshape(1, D), w, _head_ones(QW, ATT_HEAD_DIM), _head_ones(ATT_KVW, ATT_PAIR), qg, kg)


def _attn_kernel(sink_ref, x_ref, q_ref, kp_ref, kc_ref, kn_ref, vp_ref, vc_ref, vn_ref, bias_ref,
                 w_ref, o_ref):
    i = pl.program_id(1)
    nb = pl.num_programs(1)
    BLK = ATT_BLOCK
    P = ATT_PAIR
    span = 3 * BLK
    kc = jnp.concatenate([kp_ref[0], kc_ref[0], kn_ref[0]], 0)
    vc = jnp.concatenate([vp_ref[0], vc_ref[0], vn_ref[0]], 0)
    kpos = lax.broadcasted_iota(jnp.int32, (1, span), 1) + (i - 1) * BLK
    edge = jnp.where((kpos >= 0) & (kpos < nb * BLK), 0.0, NEG_INF)
    lane = lax.broadcasted_iota(jnp.int32, (1, P), 1)
    low = lane < ATT_HEAD_DIM
    zero = jnp.zeros((), BF16)
    ones = jnp.ones((span, P), BF16)
    slabs = []
    for g in range(ATT_KV_HEADS):
        kg = kc[:, g * P:(g + 1) * P]
        k_half = (jnp.where(low, kg, zero), jnp.where(low, zero, kg))
        v_ext = jnp.concatenate([vc[:, g * P:(g + 1) * P], ones], 1)
        for pr in range(ATT_GROUP // 2):
            slab = g * (ATT_GROUP // 2) + pr
            qp = q_ref[0, :, slab * P:(slab + 1) * P]
            halves = []
            for hf in range(2):
                h = 2 * slab + hf
                s = lax.dot_general(qp, k_half[hf], (((1,), (1,)), ((), ())), preferred_element_type=F32)
                s = s + bias_ref[h] + edge
                sk = sink_ref[h]
                m = jnp.maximum(jnp.max(s, -1, keepdims=True), sk)
                p = jnp.exp(s - m).astype(BF16)
                o = jnp.dot(p, v_ext, preferred_element_type=F32)
                halves.append(o[:, 0:P] / (o[:, P:2 * P] + jnp.exp(sk - m)))
            slabs.append(jnp.where(low, halves[0], halves[1]))
    o = jnp.concatenate(slabs, -1).astype(BF16)
    o_ref[0] = x_ref[0] + jnp.dot(o, w_ref[...], preferred_element_type=F32)


def attention_layer(x, norm_g, w_in, q_norm, k_norm, sink, w_out):
    B, S, D = x.shape
    T = B * S
    QW = ATT_HEADS * ATT_HEAD_DIM
    q, k, v = qkv_proj(x.reshape(T, D), norm_g, w_in, q_norm, k_norm)
    q = q.reshape(B, S, QW)
    k = k.reshape(B, S, ATT_KVW)
    v = v.reshape(B, S, ATT_KVW)
    nb = S // ATT_BLOCK
    span = 3 * ATT_BLOCK
    rel = (jnp.arange(span)[None, :] - WINDOW) - jnp.arange(ATT_BLOCK)[:, None]
    slopes = jnp.asarray([_alibi_slope(h) for h in range(ATT_HEADS)], F32)
    bias = jnp.where(jnp.abs(rel) <= WINDOW, -slopes[:, None, None] * jnp.abs(rel).astype(F32), NEG_INF)
    prev = lambda b, i: (b, jnp.maximum(i - 1, 0), 0)
    cur = lambda b, i: (b, i, 0)
    nxt = lambda b, i: (b, jnp.minimum(i + 1, nb - 1), 0)
    kv = lambda f: pl.BlockSpec((1, ATT_BLOCK, ATT_KVW), f)
    return pl.pallas_call(
        _attn_kernel,
        grid=(B, nb),
        in_specs=[pl.BlockSpec(memory_space=pltpu.SMEM),
                  pl.BlockSpec((1, ATT_BLOCK, D), cur),
                  pl.BlockSpec((1, ATT_BLOCK, QW), cur),
                  kv(prev), kv(cur), kv(nxt), kv(prev), kv(cur), kv(nxt),
                  pl.BlockSpec((ATT_HEADS, ATT_BLOCK, span), lambda b, i: (0, 0, 0)),
                  pl.BlockSpec((QW, D), lambda b, i: (0, 0))],
        out_specs=pl.BlockSpec((1, ATT_BLOCK, D), cur),
        out_shape=jax.ShapeDtypeStruct((B, S, D), F32),
        compiler_params=_cparams("parallel", "parallel"),
        name="window_attn",
    )(sink.astype(F32), x, q, k, k, k, v, v, v, bias, w_out.astype(BF16))


ROUTE_OFF = N_GROUPS


def _router_kernel(x_ref, g_ref, w12_ref, b_ref, h_ref, ri_ref, rg_ref, cnt_ref, run_ref):
    i = pl.program_id(0)

    @pl.when(i == 0)
    def _():
        run_ref[...] = jnp.zeros_like(run_ref)

    TM = x_ref.shape[0]
    x = x_ref[...]
    h = x * lax.rsqrt(jnp.mean(x * x, -1, keepdims=True) + RMS_EPS) * g_ref[...]
    _store_rows(h_ref, _pack_bf16_pairs(h))
    h1 = h.astype(BF16)
    h2 = (h - h1.astype(F32)).astype(BF16)
    hw = jnp.dot(h1, w12_ref[...], preferred_element_type=F32)
    lg = (hw[:, 0:ROUTE_LANES] + hw[:, ROUTE_LANES:2 * ROUTE_LANES]
          + jnp.dot(h2, w12_ref[:, 0:ROUTE_LANES], preferred_element_type=F32)) + b_ref[...]
    lane_i = lax.broadcasted_iota(jnp.int32, (1, ROUTE_LANES), 1)
    lane = lane_i.astype(F32)
    lane_group = ((lane_i - ROUTE_OFF + EXPERTS_PER_GROUP) // EXPERTS_PER_GROUP - 1).astype(F32)
    far = float(ROUTE_LANES)
    gmask = lane_i < N_GROUPS
    gl = jnp.where(gmask, lg, NEG_INF)
    gm = jnp.max(gl, -1, keepdims=True)
    p_group = 1.0 / jnp.sum(jnp.where(gmask, jnp.exp(gl - gm), 0.0), -1, keepdims=True)
    group = jnp.min(jnp.where(gl == gm, lane, far), -1, keepdims=True)
    fmask = lane_group == group
    fl = jnp.where(fmask, lg, NEG_INF)
    m1 = jnp.max(fl, -1, keepdims=True)
    ssum = jnp.sum(jnp.where(fmask, jnp.exp(fl - m1), 0.0), -1, keepdims=True)
    i1 = jnp.min(jnp.where(fl == m1, lane, far), -1, keepdims=True)
    fl2 = jnp.where(lane == i1, NEG_INF, fl)
    m2 = jnp.max(fl2, -1, keepdims=True)
    i2 = jnp.min(jnp.where(fl2 == m2, lane, far), -1, keepdims=True)
    p1 = 1.0 / ssum
    p2 = jnp.exp(m2 - m1) / ssum
    norm = p_group / (p1 + p2)
    oh = jnp.where((lane == i1) | (lane == i2), 1.0, 0.0)
    tr = lax.broadcasted_iota(jnp.int32, (TM, TM), 0)
    tc = lax.broadcasted_iota(jnp.int32, (TM, TM), 1)
    before = jnp.where(tc < tr, 1.0, 0.0).astype(BF16)
    pre = jnp.dot(before, oh.astype(BF16), preferred_element_type=F32) + run_ref[...]
    rank1 = jnp.sum(jnp.where(lane == i1, pre, 0.0), -1, keepdims=True)
    rank2 = jnp.sum(jnp.where(lane == i2, pre, 0.0), -1, keepdims=True)
    total = run_ref[...] + jnp.sum(oh, 0, keepdims=True)
    run_ref[...] = total
    cnt_ref[...] = total.astype(jnp.int32)
    cols = jnp.where(lane_i == 0, i1 - ROUTE_OFF,
                     jnp.where(lane_i == 1, i2 - ROUTE_OFF,
                               jnp.where(lane_i == 2, rank1, jnp.where(lane_i == 3, rank2, 0.0))))
    ri_ref[...] = jnp.transpose(cols)[0:SUBLANES, :].astype(jnp.int32)
    rg_ref[...] = jnp.where(lane_i == 0, p1 * norm, jnp.where(lane_i == 1, p2 * norm, 0.0))


def moe_router(x2d, g, wg1, bg1, wg2, bg2):
    T, D = x2d.shape
    wr = jnp.zeros((D, ROUTE_LANES), F32)
    wr = wr.at[:, 0:N_GROUPS].set(wg1)
    wr = wr.at[:, ROUTE_OFF:ROUTE_OFF + N_EXPERTS].set(jnp.moveaxis(wg2, 0, 1).reshape(D, N_EXPERTS))
    w1 = wr.astype(BF16)
    w2 = (wr - w1.astype(F32)).astype(BF16)
    bias = jnp.zeros((1, ROUTE_LANES), F32)
    bias = bias.at[0, 0:N_GROUPS].set(bg1).at[0, ROUTE_OFF:ROUTE_OFF + N_EXPERTS].set(bg2.reshape(-1))
    row = lambda n: pl.BlockSpec((ROW_TILE, n), lambda i: (i, 0))
    fixed = lambda r, n: pl.BlockSpec((r, n), lambda i: (0, 0))
    return pl.pallas_call(
        _router_kernel,
        grid=(T // ROW_TILE,),
        in_specs=[row(D), fixed(1, D), fixed(D, 2 * ROUTE_LANES), fixed(1, ROUTE_LANES)],
        out_specs=[pl.BlockSpec((ROW_TILE * ROW_SLABS, LANES), lambda i: (i, 0)),
                   pl.BlockSpec((SUBLANES, ROW_TILE), lambda i: (i, 0)), row(ROUTE_LANES), fixed(1, ROUTE_LANES)],
        out_shape=[jax.ShapeDtypeStruct((T * ROW_SLABS, LANES), jnp.int32),
                   jax.ShapeDtypeStruct((T // ROW_TILE * SUBLANES, ROW_TILE), jnp.int32),
                   jax.ShapeDtypeStruct((T, ROUTE_LANES), F32),
                   jax.ShapeDtypeStruct((1, ROUTE_LANES), jnp.int32)],
        scratch_shapes=[pltpu.VMEM((1, ROUTE_LANES), F32)],
        compiler_params=_cparams("arbitrary"),
        name="moe_router",
    )(x2d, g.reshape(1, D), jnp.concatenate([w1, w2], 1), bias)


ROW_SLABS = (D_MODEL // 2) // LANES


def _store_rows(ref, packed):
    n = packed.shape[0]
    for j in range(ROW_SLABS):
        ref[pl.ds(j, n, stride=ROW_SLABS), :] = packed[:, j * LANES:(j + 1) * LANES]


def _load_rows(ref, n):
    return jnp.concatenate([ref[pl.ds(j, n, stride=ROW_SLABS), :] for j in range(ROW_SLABS)], 1)


def _row_copy(src, src_row, dst, dst_row, sem):
    return pltpu.make_async_copy(src.at[pl.ds(src_row * ROW_SLABS, ROW_SLABS)],
                                 dst.at[pl.ds(dst_row * ROW_SLABS, ROW_SLABS)], sem)


DMA_UNROLL = 8


def _rows_wait(ref, nrows, sem):
    pltpu.make_async_copy(ref.at[pl.ds(0, nrows * ROW_SLABS)], ref.at[pl.ds(0, nrows * ROW_SLABS)], sem).wait()


DISPATCH_ROWS = DMA_ROWS


def _dispatch_kernel(pend_ref, dest_ref, h_ref, xs_ref, zero_ref, sem, zsem):
    i = pl.program_id(0)
    n = h_ref.shape[0] // ROW_SLABS
    blk = MOE_ROWS * ROW_SLABS
    nblk = xs_ref.shape[0] // blk

    @pl.when(i == 0)
    def _():
        zero_ref[...] = jnp.zeros_like(zero_ref)

        def block_copy(row0):
            return pltpu.make_async_copy(zero_ref, xs_ref.at[pl.ds(pl.multiple_of(row0 * ROW_SLABS, blk), blk)], zsem)

        tails = [jnp.maximum(pend_ref[e] - MOE_ROWS, 0) for e in range(N_EXPERTS)]
        for t in tails:
            block_copy(t).start()
        for t in tails:
            block_copy(t).wait()
        first_unused = pend_ref[N_EXPERTS - 1] // MOE_ROWS

        def clear(j, c):
            block_copy(j * MOE_ROWS).start()
            block_copy(j * MOE_ROWS).wait()
            return c

        lax.fori_loop(first_unused, nblk, clear, 0)

    def issue(r, c):
        for k in range(TOP_K):
            _row_copy(h_ref, r, xs_ref, dest_ref[0, 0, k * n + r], sem).start()
        return c

    lax.fori_loop(0, n, issue, 0, unroll=DMA_UNROLL)
    _rows_wait(xs_ref, TOP_K * n, sem)


def moe_dispatch(h_rows, dest, p_end, rows):
    T = h_rows.shape[0] // ROW_SLABS
    nt = T // DISPATCH_ROWS
    return pl.pallas_call(
        _dispatch_kernel,
        grid_spec=pltpu.PrefetchScalarGridSpec(
            num_scalar_prefetch=1,
            grid=(nt,),
            in_specs=[pl.BlockSpec((1, 1, TOP_K * DISPATCH_ROWS), lambda i, pe: (i, 0, 0),
                                   memory_space=pltpu.SMEM),
                      pl.BlockSpec((DISPATCH_ROWS * ROW_SLABS, LANES), lambda i, pe: (i, 0))],
            out_specs=pl.BlockSpec(memory_space=pl.ANY),
            scratch_shapes=[pltpu.VMEM((MOE_ROWS * ROW_SLABS, LANES), h_rows.dtype), pltpu.SemaphoreType.DMA(()),
                            pltpu.SemaphoreType.DMA(())]),
        out_shape=jax.ShapeDtypeStruct((rows * ROW_SLABS, LANES), h_rows.dtype),
        compiler_params=_cparams("arbitrary"),
        name="moe_dispatch",
    )(p_end, dest, h_rows)


def _expert_kernel(be_ref, nu_ref, x_ref, wg_ref, wu_ref, wd_ref, o_ref, wg_bf, wu_bf, wd_bf):
    i = pl.program_id(0)
    used = i < nu_ref[0]
    new_expert = jnp.logical_or(i == 0, be_ref[i] != be_ref[jnp.maximum(i - 1, 0)])

    @pl.when(jnp.logical_and(used, new_expert))
    def _():
        wg_bf[...] = wg_ref[0].astype(BF16)
        wu_bf[...] = wu_ref[0].astype(BF16)
        wd_bf[...] = wd_ref[0].astype(BF16)

    @pl.when(used)
    def _():
        xb = _unpack_bf16_pairs(_load_rows(x_ref, MOE_ROWS)).astype(BF16)
        hg = jnp.dot(xb, wg_bf[...], preferred_element_type=F32)
        hu = jnp.dot(xb, wu_bf[...], preferred_element_type=F32)
        hb = (hg * _sigmoid_tanh(hg) * hu).astype(BF16)
        _store_rows(o_ref, _pack_bf16_pairs(jnp.dot(hb, wd_bf[...], preferred_element_type=F32)))

    @pl.when(i >= nu_ref[0])
    def _():
        o_ref[...] = jnp.zeros_like(o_ref)


def moe_experts(xs, blk_exp, n_used, w_gate, w_up, w_down):
    D = D_MODEL
    blk = MOE_ROWS * ROW_SLABS
    nblk = xs.shape[0] // blk
    F = EXPERT_FF
    return pl.pallas_call(
        _expert_kernel,
        grid_spec=pltpu.PrefetchScalarGridSpec(
            num_scalar_prefetch=2,
            grid=(nblk,),
            in_specs=[pl.BlockSpec((blk, LANES), lambda i, be, nu: (jnp.minimum(i, nu[0] - 1), 0)),
                      pl.BlockSpec((1, D, F), lambda i, be, nu: (be[i], 0, 0)),
                      pl.BlockSpec((1, D, F), lambda i, be, nu: (be[i], 0, 0)),
                      pl.BlockSpec((1, F, D), lambda i, be, nu: (be[i], 0, 0))],
            out_specs=pl.BlockSpec((blk, LANES), lambda i, be, nu: (i, 0)),
            scratch_shapes=[pltpu.VMEM((D, F), BF16), pltpu.VMEM((D, F), BF16), pltpu.VMEM((F, D), BF16)]),
        out_shape=jax.ShapeDtypeStruct(xs.shape, xs.dtype),
        compiler_params=_cparams("arbitrary"),
        name="moe_experts",
    )(blk_exp, n_used, xs, w_gate, w_up, w_down)


def _combine_kernel(dest_ref, dnext_ref, x_ref, gate_ref, eo_ref, o_ref, buf_ref, sem):
    i = pl.program_id(0)
    nt = pl.num_programs(0)
    n = x_ref.shape[0]
    slot = i % 2

    def gather(d_ref, s):
        def issue(r, c):
            for k in range(TOP_K):
                _row_copy(eo_ref, d_ref[0, 0, k * n + r], buf_ref.at[s, k], r, sem.at[s]).start()
            return c

        lax.fori_loop(0, n, issue, 0, unroll=DMA_UNROLL)

    @pl.when(i == 0)
    def _():
        gather(dest_ref, 0)

    @pl.when(i + 1 < nt)
    def _():
        gather(dnext_ref, 1 - slot)

    pltpu.make_async_copy(buf_ref.at[slot], buf_ref.at[slot], sem.at[slot]).wait()
    gate = gate_ref[...]
    ffn = gate[:, 0:1] * _unpack_bf16_pairs(_load_rows(buf_ref.at[slot, 0], n))
    for k in range(1, TOP_K):
        ffn = ffn + gate[:, k:k + 1] * _unpack_bf16_pairs(_load_rows(buf_ref.at[slot, k], n))
    o_ref[...] = x_ref[...] + ffn


def moe_combine(x2d, gates, dest, eo):
    T, D = x2d.shape
    nt = T // DMA_ROWS
    dest3 = dest
    dspec = lambda f: pl.BlockSpec((1, 1, TOP_K * DMA_ROWS), f, memory_space=pltpu.SMEM)
    return pl.pallas_call(
        _combine_kernel,
        grid=(nt,),
        in_specs=[dspec(lambda i: (i, 0, 0)),
                  dspec(lambda i: (jnp.minimum(i + 1, nt - 1), 0, 0)),
                  pl.BlockSpec((DMA_ROWS, D), lambda i: (i, 0)),
                  pl.BlockSpec((DMA_ROWS, ROUTE_LANES), lambda i: (i, 0)),
                  pl.BlockSpec(memory_space=pl.ANY)],
        out_specs=pl.BlockSpec((DMA_ROWS, D), lambda i: (i, 0)),
        out_shape=jax.ShapeDtypeStruct((T, D), F32),
        scratch_shapes=[pltpu.VMEM((2, TOP_K, DMA_ROWS * ROW_SLABS, LANES), eo.dtype),
                        pltpu.SemaphoreType.DMA((2,))],
        compiler_params=_cparams("arbitrary"),
        name="moe_combine",
    )(dest3, dest3, x2d, gates, eo)


def moe_layer(x, layer, norm_g, wg1, bg1, wg2, bg2, w_gate, w_up, w_down):
    B, S, D = x.shape
    T = B * S
    A = T * TOP_K
    x2d = x.reshape(T, D)
    h, route_i, route_g, counts = moe_router(x2d, norm_g, wg1, bg1, wg2, bg2)
    counts = counts[0, ROUTE_OFF:ROUTE_OFF + N_EXPERTS]
    padded = (counts + MOE_ROWS - 1) // MOE_ROWS * MOE_ROWS
    p_end = jnp.cumsum(padded)
    p_start = p_end - padded
    nt = T // ROW_TILE
    ri = route_i.reshape(nt, SUBLANES, ROW_TILE)
    sel = ri[:, 0:TOP_K, :, None] == jnp.arange(N_EXPERTS, dtype=jnp.int32)
    dest = jnp.sum(jnp.where(sel, p_start, 0), -1) + ri[:, TOP_K:2 * TOP_K, :]
    dest = dest.reshape(nt, 1, TOP_K * ROW_TILE).astype(jnp.int32)
    nblk = -(-A // MOE_ROWS) + N_EXPERTS
    blk_row = jnp.arange(nblk, dtype=jnp.int32) * MOE_ROWS
    blk_exp = jnp.minimum(jnp.sum(p_end[None, :] <= blk_row[:, None], -1), N_EXPERTS - 1).astype(jnp.int32)
    n_used = (p_end[-1:] // MOE_ROWS).astype(jnp.int32)
    xs = moe_dispatch(h, dest, p_end.astype(jnp.int32), nblk * MOE_ROWS)
    eo = moe_experts(xs, blk_exp + layer * N_EXPERTS, n_used, w_gate, w_up, w_down)
    return moe_combine(x2d, route_g, dest, eo).reshape(B, S, D)


def _trunk(x, p):
    x = recurrent_layer(x, p["norm_mix"][0], p["rec_w_in"][0], p["rg_conv_w"][0], p["rg_conv_b"][0],
                        p["rg_wa"][0], p["rg_ba"][0], p["rg_wx"][0], p["rg_bx"][0], p["rg_lambda"][0],
                        p["rw_mu_l"][0], p["rw_mu_r"][0], p["rw_w0"][0], p["rw_w_up"][0], p["rw_a0"][0],
                        p["rw_a_up"][0], p["rw_g_up"][0], p["rw_k_k"][0], p["rw_k_a"][0], p["rw_r_k"][0],
                        p["rw_ln_w"][0], p["rw_ln_b"][0], p["rec_w_out"][0])
    x = moe_layer(x, 0, p["norm_ffn"][0], p["moe_wg1"][0], p["moe_bg1"][0], p["moe_wg2"][0], p["moe_bg2"][0],
                  p["moe_w_gate"], p["moe_w_up"], p["moe_w_down"])
    x = attention_layer(x, p["norm_mix"][1], p["att_w_in"][0], p["att_q_norm"][0], p["att_k_norm"][0],
                        p["att_sink"][0], p["att_w_out"][0])
    x = moe_layer(x, 1, p["norm_ffn"][1], p["moe_wg1"][1], p["moe_bg1"][1], p["moe_wg2"][1], p["moe_bg2"][1],
                  p["moe_w_gate"], p["moe_w_up"], p["moe_w_down"])
    return x


def kernel(x_prompt, x_sample, norm_mix, norm_ffn, rec_w_in, rg_conv_w, rg_conv_b, rg_wa, rg_ba, rg_wx, rg_bx,
           rg_lambda, rw_mu_l, rw_mu_r, rw_w0, rw_w_up, rw_a0, rw_a_up, rw_g_up, rw_k_k, rw_k_a, rw_r_k,
           rw_ln_w, rw_ln_b, rec_w_out, att_w_in, att_q_norm, att_k_norm, att_sink, att_w_out, moe_wg1, moe_bg1,
           moe_wg2, moe_bg2, moe_w_gate, moe_w_up, moe_w_down):
    p = dict(norm_mix=norm_mix, norm_ffn=norm_ffn, rec_w_in=rec_w_in, rg_conv_w=rg_conv_w, rg_conv_b=rg_conv_b,
             rg_wa=rg_wa, rg_ba=rg_ba, rg_wx=rg_wx, rg_bx=rg_bx, rg_lambda=rg_lambda, rw_mu_l=rw_mu_l,
             rw_mu_r=rw_mu_r, rw_w0=rw_w0, rw_w_up=rw_w_up, rw_a0=rw_a0, rw_a_up=rw_a_up, rw_g_up=rw_g_up,
             rw_k_k=rw_k_k, rw_k_a=rw_k_a, rw_r_k=rw_r_k, rw_ln_w=rw_ln_w, rw_ln_b=rw_ln_b, rec_w_out=rec_w_out,
             att_w_in=att_w_in, att_q_norm=att_q_norm, att_k_norm=att_k_norm, att_sink=att_sink,
             att_w_out=att_w_out, moe_wg1=moe_wg1, moe_bg1=moe_bg1, moe_wg2=moe_wg2, moe_bg2=moe_bg2,
             moe_w_gate=moe_w_gate.reshape(-1, D_MODEL, EXPERT_FF),
             moe_w_up=moe_w_up.reshape(-1, D_MODEL, EXPERT_FF),
             moe_w_down=moe_w_down.reshape(-1, EXPERT_FF, D_MODEL))
    return (_trunk(x_prompt, p), _trunk(x_sample, p))
```

```python
import functools
import math

import jax
import jax.numpy as jnp
from jax import lax
from jax.experimental import pallas as pl
from jax.experimental.pallas import tpu as pltpu

F32 = jnp.float32
BF16 = jnp.bfloat16

D_MODEL = 1024
RG_WIDTH = 512
RG_BLOCK_DIM = 64
CONV_WIDTH = 4
RG_C = 8.0
RW_HEAD_DIM = 64
RW_WIDTH = 512
DECAY_LORA = 32
ICL_LORA = 32
GATE_LORA = 64
RW_IN = 3 * RW_WIDTH + DECAY_LORA + ICL_LORA + GATE_LORA
RW_GN_EPS = 64e-5
ATT_HEADS = 16
ATT_KV_HEADS = 4
ATT_GROUP = 4
ATT_HEAD_DIM = 64
WINDOW = 128
ATT_BLOCK = 128
N_GROUPS = 4
EXPERTS_PER_GROUP = 8
N_EXPERTS = 32
TOP_K = 2
EXPERT_FF = 512
RMS_EPS = 1e-6
NEG_INF = -1e30

LANES = 128
SUBLANES = 8
HALO_ROWS = 16
VMEM_LIMIT_BYTES = 56 * 1024 * 1024
ROW_TILE = 512
WKV_TILE = 256
WKV_CHUNK = 64
MOE_ROWS = 512
ROUTE_LANES = 128
DMA_ROWS = ROW_TILE


def _cparams(*sem):
    return pltpu.CompilerParams(dimension_semantics=sem, vmem_limit_bytes=VMEM_LIMIT_BYTES)


def _dot(a, b):
    return jnp.dot(a.astype(BF16), b.astype(BF16), preferred_element_type=F32)


def _split2(x):
    h1 = x.astype(BF16)
    return h1, (x - h1.astype(F32)).astype(BF16)


def _dot_exact_lhs(e, x):
    h1, h2 = _split2(x)
    return jnp.dot(e, h1, preferred_element_type=F32) + jnp.dot(e, h2, preferred_element_type=F32)


def _sigmoid_tanh(x):
    return 0.5 * jnp.tanh(0.5 * x) + 0.5


def _pack_bf16_pairs(x):
    n = x.shape[1] // 2
    hi = lax.bitcast_convert_type(x[:, :n].astype(BF16).astype(F32), jnp.int32)
    lo = lax.bitcast_convert_type(x[:, n:].astype(BF16).astype(F32), jnp.int32)
    return hi | lax.shift_right_logical(lo, 16)


def _unpack_bf16_pairs(p):
    hi = lax.bitcast_convert_type(p & jnp.int32(-65536), F32)
    lo = lax.bitcast_convert_type(lax.shift_left(p, 16), F32)
    return jnp.concatenate([hi, lo], 1)


def _gelu_tanh(x):
    return 0.5 * x * (1.0 + jnp.tanh(math.sqrt(2.0 / math.pi) * (x + 0.044715 * (x * x * x))))


def _head_ones(width, head):
    r = lax.broadcasted_iota(jnp.int32, (width, width), 0) // head
    c = lax.broadcasted_iota(jnp.int32, (width, width), 1) // head
    return jnp.where(r == c, 1.0, 0.0).astype(BF16)


def _norm_proj_kernel(x_ref, g_ref, w_ref, *out_refs, splits):
    x = x_ref[...]
    h = x * lax.rsqrt(jnp.mean(x * x, -1, keepdims=True) + RMS_EPS) * g_ref[...]
    hb = h.astype(BF16)
    off = 0
    for o_ref, n in zip(out_refs, splits):
        o_ref[...] = jnp.dot(hb, w_ref[:, off:off + n], preferred_element_type=F32).astype(o_ref.dtype)
        off += n


def norm_proj(x2d, g, w, splits, out_dtype=F32):
    T, D = x2d.shape
    N = w.shape[1]
    assert sum(splits) == N and T % ROW_TILE == 0
    return pl.pallas_call(
        functools.partial(_norm_proj_kernel, splits=splits),
        grid=(T // ROW_TILE,),
        in_specs=[pl.BlockSpec((ROW_TILE, D), lambda i: (i, 0)),
                  pl.BlockSpec((1, D), lambda i: (0, 0)),
                  pl.BlockSpec((D, N), lambda i: (0, 0))],
        out_specs=[pl.BlockSpec((ROW_TILE, n), lambda i: (i, 0)) for n in splits],
        out_shape=[jax.ShapeDtypeStruct((T, n), out_dtype) for n in splits],
        compiler_params=_cparams("parallel"),
        name="norm_proj",
    )(x2d, g.reshape(1, D), w.astype(BF16))


RG_HALF = RG_WIDTH // 2
RG_ROWS = 256
RG_PAD = SUBLANES


def _rglru_kernel(x_ref, g_ref, cw_ref, cb_ref, wg_ref, bg_ref, sp_ref, o_ref,
                  xp_ref, af_ref, bf_ref, ab_ref, bb_ref):
    S = x_ref.shape[1]
    C = RG_HALF
    xp_ref[0:RG_PAD, :] = jnp.zeros((RG_PAD, C), F32)
    xp_ref[RG_PAD + S:RG_PAD + S + RG_PAD, :] = jnp.zeros((RG_PAD, C), F32)
    xp_ref[RG_PAD:RG_PAD + S, :] = x_ref[0].astype(F32)
    cw = cw_ref[...]
    left = CONV_WIDTH // 2
    for c in range(S // RG_ROWS):
        r0 = c * RG_ROWS
        xc = cb_ref[...] + cw[0:1] * xp_ref[RG_PAD + r0 - left:RG_PAD + r0 - left + RG_ROWS, :]
        for k in range(1, CONV_WIDTH):
            s0 = RG_PAD + r0 + k - left
            xc = xc + cw[k:k + 1] * xp_ref[s0:s0 + RG_ROWS, :]
        z = _dot(xc, wg_ref[0]) + bg_ref[0]
        for d, (a_ref, b_ref) in enumerate(((af_ref, bf_ref), (ab_ref, bb_ref))):
            r = _sigmoid_tanh(z[:, (2 * d) * C:(2 * d + 1) * C])
            i = _sigmoid_tanh(z[:, (2 * d + 1) * C:(2 * d + 2) * C])
            a = jnp.exp(-RG_C * r * sp_ref[0][:, d * C:(d + 1) * C])
            a_ref[r0:r0 + RG_ROWS, :] = a
            y = 1.0 - a * a
            b_ref[r0:r0 + RG_ROWS, :] = jnp.where(y > 0.0, y * lax.rsqrt(y), 0.0) * (i * xc)

    row8 = lax.broadcasted_iota(jnp.int32, (SUBLANES, C), 0)

    def tile_scan(a, b, carry, reverse):
        for s in (1, 2, 4):
            keep = (row8 < SUBLANES - s) if reverse else (row8 >= s)
            shift = SUBLANES - s if reverse else s
            b = b + a * jnp.where(keep, pltpu.roll(b, shift, 0), 0.0)
            a = a * jnp.where(keep, pltpu.roll(a, shift, 0), 1.0)
        h = b + a * carry
        last = 0 if reverse else SUBLANES - 1
        return h, h[last:last + 1]

    def body(n, carry):
        hf, hb = carry
        r0 = pl.multiple_of(n * SUBLANES, SUBLANES)
        h8, hf = tile_scan(af_ref[pl.ds(r0, SUBLANES), :], bf_ref[pl.ds(r0, SUBLANES), :], hf, False)
        bf_ref[pl.ds(r0, SUBLANES), :] = h8
        r1 = pl.multiple_of(S - SUBLANES - n * SUBLANES, SUBLANES)
        h8, hb = tile_scan(ab_ref[pl.ds(r1, SUBLANES), :], bb_ref[pl.ds(r1, SUBLANES), :], hb, True)
        bb_ref[pl.ds(r1, SUBLANES), :] = h8
        return hf, hb

    zero = jnp.zeros((1, C), F32)
    lax.fori_loop(0, S // SUBLANES, body, (zero, zero), unroll=2)
    for c in range(S // RG_ROWS):
        sl = slice(c * RG_ROWS, (c + 1) * RG_ROWS)
        o_ref[0, sl, :] = ((bf_ref[sl, :] + bb_ref[sl, :]) * _gelu_tanh(g_ref[0, sl, :].astype(F32))).astype(o_ref.dtype)


def rglru_branch(rg_x, rg_g, conv_w, conv_b, wa, ba, wx, bx, lam):
    B, S, _ = rg_x.shape
    C = RG_HALF
    nb = C // RG_BLOCK_DIM

    def bdiag(w):
        w = w.reshape(2, nb, RG_BLOCK_DIM, RG_BLOCK_DIM)
        eye = jnp.eye(nb, dtype=w.dtype)
        return jnp.einsum('hnij,nm->hnimj', w, eye).reshape(2, C, C)

    wg = jnp.concatenate([bdiag(wa[0]), bdiag(wx[0]), bdiag(wa[1]), bdiag(wx[1])], axis=-1).astype(BF16)

    def halves(v):
        return v.reshape(2, 1, C)

    bg = jnp.concatenate([halves(ba[0]), halves(bx[0]), halves(ba[1]), halves(bx[1])], axis=-1)
    sp = jax.nn.softplus(-lam.astype(F32))
    spg = jnp.concatenate([halves(sp[0]), halves(sp[1])], axis=-1)
    return pl.pallas_call(
        _rglru_kernel,
        grid=(B, 2),
        in_specs=[pl.BlockSpec((1, S, C), lambda b, c: (b, 0, c)),
                  pl.BlockSpec((1, S, C), lambda b, c: (b, 0, c)),
                  pl.BlockSpec((CONV_WIDTH, C), lambda b, c: (0, c)),
                  pl.BlockSpec((1, C), lambda b, c: (0, c)),
                  pl.BlockSpec((1, C, 4 * C), lambda b, c: (c, 0, 0)),
                  pl.BlockSpec((1, 1, 4 * C), lambda b, c: (c, 0, 0)),
                  pl.BlockSpec((1, 1, 2 * C), lambda b, c: (c, 0, 0))],
        out_specs=pl.BlockSpec((1, S, C), lambda b, c: (b, 0, c)),
        out_shape=jax.ShapeDtypeStruct((B, S, RG_WIDTH), BF16),
        scratch_shapes=[pltpu.VMEM((S + 2 * RG_PAD, C), F32)] + [pltpu.VMEM((S, C), F32)] * 4,
        compiler_params=_cparams("parallel", "parallel"),
        name="rglru",
    )(rg_x, rg_g, conv_w, conv_b.reshape(1, RG_WIDTH), wg, bg, spg)


def _rwkv_prep_kernel(u_ref, up_ref, un_ref, mul_ref, mur_ref, wl_ref, w0_ref, a0_ref, kk_ref, ka_ref,
                      rk_ref, rt_ref, kt_ref, bt_ref, kh_ref, v_ref, le_ref, bonus_ref, g_ref):
    i = pl.program_id(1)
    nt = pl.num_programs(1)
    TS = u_ref.shape[1]
    W = RW_WIDTH
    u = u_ref[0].astype(F32)
    prow = jnp.where(i == 0, 0.0, up_ref[0][HALO_ROWS - 1:HALO_ROWS, :].astype(F32))
    nrow = jnp.where(i == nt - 1, 0.0, un_ref[0][0:1, :].astype(F32))
    rows = lax.broadcasted_iota(jnp.int32, (TS, 1), 0)
    prev = jnp.where(rows == 0, prow, pltpu.roll(u, 1, 0))
    nxt = jnp.where(rows == TS - 1, nrow, pltpu.roll(u, TS - 1, 0))
    m = u + mul_ref[...] * (prev - u) + mur_ref[...] * (nxt - u)
    r = m[:, 0:W]
    k = m[:, W:2 * W]
    v = m[:, 2 * W:3 * W]
    tail = m[:, 3 * W:3 * W + LANES]
    lane = lax.broadcasted_iota(jnp.int32, (1, LANES), 1)
    z = jnp.where(lane < DECAY_LORA, jnp.tanh(tail),
                  jnp.where(lane < DECAY_LORA + ICL_LORA, tail, _sigmoid_tanh(tail)))
    lo = _dot(z, wl_ref[...])
    ones = _head_ones(W, RW_HEAD_DIM)
    kkr = k * kk_ref[...]
    kk = kkr * lax.rsqrt(jnp.maximum(_dot(kkr * kkr, ones), 1e-24))
    tr = lax.broadcasted_iota(jnp.int32, (TS, TS), 0)
    tc = lax.broadcasted_iota(jnp.int32, (TS, TS), 1)
    same = (tr // WKV_CHUNK) == (tc // WKV_CHUNK)
    kd_sum = jnp.zeros((TS, W), F32)
    nch = TS // WKV_CHUNK
    for d in range(2):
        ld = -math.exp(-0.5) * _sigmoid_tanh(w0_ref[d:d + 1, :] + lo[:, d * W:(d + 1) * W])
        a = _sigmoid_tanh(a0_ref[d:d + 1, :] + lo[:, (2 + d) * W:(3 + d) * W])
        kd = k * (1.0 + (a - 1.0) * ka_ref[...])
        kd_sum = kd_sum + kd
        tri = jnp.where(same & ((tc <= tr) if d == 0 else (tc >= tr)), 1.0, 0.0).astype(BF16)
        L = _dot_exact_lhs(tri, ld)
        en = jnp.exp(-L)
        rt_ref[d, 0] = (r * jnp.exp(L)).astype(BF16)
        kt_ref[d, 0] = (kk * jnp.exp(L - ld)).astype(BF16)
        bt_ref[d, 0] = (kk * a * en).astype(BF16)
        kh_ref[d, 0] = (kd * en).astype(BF16)
        ends = [L[(c + 1) * WKV_CHUNK - 1:(c + 1) * WKV_CHUNK] if d == 0 else L[c * WKV_CHUNK:c * WKV_CHUNK + 1]
                for c in range(nch)]
        le_ref[d, 0, 0] = jnp.concatenate(ends, 0)
    v_ref[0] = v.astype(BF16)
    bonus_ref[0] = (_dot(r * kd_sum * rk_ref[...], ones) * v).astype(BF16)
    g_ref[0] = lo[:, 4 * W:5 * W].astype(BF16)


def rwkv_prep(rw, mu_l, mu_r, w0, w_up, a0, a_up, g_up, k_k, k_a, r_k):
    B, S, _ = rw.shape
    TS = WKV_TILE
    W = RW_WIDTH
    nt = S // TS
    nch = TS // WKV_CHUNK
    hb = TS // HALO_ROWS
    wl = jnp.zeros((LANES, 5 * W), F32)
    wl = wl.at[0:DECAY_LORA, 0:W].set(w_up[0]).at[0:DECAY_LORA, W:2 * W].set(w_up[1])
    o = DECAY_LORA
    wl = wl.at[o:o + ICL_LORA, 2 * W:3 * W].set(a_up[0]).at[o:o + ICL_LORA, 3 * W:4 * W].set(a_up[1])
    o += ICL_LORA
    wl = wl.at[o:o + GATE_LORA, 4 * W:5 * W].set(g_up)
    vec = lambda n: pl.BlockSpec((1, n), lambda b, i: (0, 0))
    big = lambda: pl.BlockSpec((2, 1, TS, W), lambda b, i: (0, b, i, 0))
    one = lambda: pl.BlockSpec((1, TS, W), lambda b, i: (b, i, 0))
    return pl.pallas_call(
        _rwkv_prep_kernel,
        grid=(B, nt),
        in_specs=[pl.BlockSpec((1, TS, RW_IN), lambda b, i: (b, i, 0)),
                  pl.BlockSpec((1, HALO_ROWS, RW_IN), lambda b, i: (b, jnp.maximum(i * hb - 1, 0), 0)),
                  pl.BlockSpec((1, HALO_ROWS, RW_IN), lambda b, i: (b, jnp.minimum((i + 1) * hb, S // HALO_ROWS - 1), 0)),
                  vec(RW_IN), vec(RW_IN),
                  pl.BlockSpec((LANES, 5 * W), lambda b, i: (0, 0)),
                  pl.BlockSpec((2, W), lambda b, i: (0, 0)),
                  pl.BlockSpec((2, W), lambda b, i: (0, 0)),
                  vec(W), vec(W), vec(W)],
        out_specs=[big(), big(), big(), big(), one(),
                   pl.BlockSpec((2, 1, 1, nch, W), lambda b, i: (0, b, i, 0, 0)),
                   one(), one()],
        out_shape=[jax.ShapeDtypeStruct((2, B, S, W), BF16)] * 4
        + [jax.ShapeDtypeStruct((B, S, W), BF16),
           jax.ShapeDtypeStruct((2, B, nt, nch, W), F32),
           jax.ShapeDtypeStruct((B, S, W), BF16),
           jax.ShapeDtypeStruct((B, S, W), BF16)],
        compiler_params=_cparams("parallel", "parallel"),
        name="rwkv_prep",
    )(rw, rw, rw, mu_l.reshape(1, RW_IN), mu_r.reshape(1, RW_IN), wl.astype(BF16), w0, a0,
      k_k.reshape(1, W), k_a.reshape(1, W), r_k.reshape(1, W))


def _wkv_tiles(probs):
    TS = WKV_TILE
    C = WKV_CHUNK
    N = RW_HEAD_DIM
    P = 2 * N
    nch = TS // C
    zero = jnp.zeros((), BF16)
    head0 = lax.broadcasted_iota(jnp.int32, (1, P), 1) < N
    head0_2 = (lax.broadcasted_iota(jnp.int32, (1, 2 * P), 1) % P) < N
    head0_w = (lax.broadcasted_iota(jnp.int32, (1, nch * P), 1) % P) < N
    own = (lax.broadcasted_iota(jnp.int32, (TS, nch * P), 0) // C
           == lax.broadcasted_iota(jnp.int32, (TS, nch * P), 1) // P)
    tq = lax.broadcasted_iota(jnp.int32, (C, TS), 0)
    sq = lax.broadcasted_iota(jnp.int32, (C, TS), 1) % C
    blk = (lax.broadcasted_iota(jnp.int32, (TS, TS), 0) // C
           == lax.broadcasted_iota(jnp.int32, (TS, TS), 1) // C)
    hr = lax.broadcasted_iota(jnp.int32, (P, P), 0)
    hc = lax.broadcasted_iota(jnp.int32, (P, P), 1)
    bdiag = (hr // N) == (hc // N)
    heye = hr == hc

    def both_heads(xb, m):
        return jnp.concatenate([jnp.where(m, xb, zero), jnp.where(m, zero, xb)], 0)

    def wide(xb):
        return jnp.concatenate([xb[c * C:(c + 1) * C] for c in range(nch)], 1)

    def expand(xw):
        return jnp.where(blk, jnp.concatenate([xw] * nch, 0), zero)

    st = []
    for (rt, kt, bt, kh, v, le, H, reverse) in probs:
        ktb = kt.astype(BF16)
        vb = v.astype(BF16)
        ktw = wide(ktb)
        rtw = wide(rt.astype(BF16))
        lhs = jnp.concatenate([jnp.where(head0_w, ktw, zero), jnp.where(head0_w, rtw, zero),
                               jnp.where(head0_w, zero, ktw), jnp.where(head0_w, zero, rtw)], 0)
        rhs = jnp.concatenate([jnp.where(own, jnp.concatenate([bt.astype(BF16)] * nch, 1), zero),
                               jnp.where(own, jnp.concatenate([kh.astype(BF16)] * nch, 1), zero)], 0)
        gram = lax.dot_general(lhs, rhs, (((1,), (1,)), ((), ())), preferred_element_type=F32)
        st.append(dict(ktb=ktb, vb=vb, gram=gram, vm=both_heads(vb, head0)))

    chains = []
    for s, prob in zip(st, probs):
        reverse = prob[7]
        strict = (sq > tq) if reverse else (sq < tq)
        incl = (sq >= tq) if reverse else (sq <= tq)
        s["b_bd"], s["rb_bd"], s["rk_bd"] = [], [], []
        for h in range(2):
            g0 = s["gram"][2 * h * C:(2 * h + 1) * C]
            g1 = s["gram"][(2 * h + 1) * C:(2 * h + 2) * C]
            A = jnp.where(strict, g0[:, 0:TS], 0.0)
            s["b_bd"].append(expand(jnp.where(strict, g0[:, TS:2 * TS], 0.0).astype(BF16)))
            s["rb_bd"].append(expand(jnp.where(incl, g1[:, 0:TS], 0.0).astype(BF16)))
            s["rk_bd"].append(expand(jnp.where(incl, g1[:, TS:2 * TS], 0.0).astype(BF16)))
            chains.append(dict(Tw=jnp.where(sq == tq, 1.0, 0.0) - A, Ab=A.astype(BF16)))
    for ch in chains:
        ch["Q"] = jnp.dot(ch["Ab"], expand(ch["Ab"]), preferred_element_type=F32)
    for _ in range(int(math.log2(C)) - 2):
        for ch in chains:
            Qb = ch["Q"].astype(BF16)
            out = jnp.dot(jnp.concatenate([ch["Tw"].astype(BF16), Qb], 0), expand(Qb), preferred_element_type=F32)
            ch["Tw"] = ch["Tw"] + out[0:C]
            ch["Q"] = out[C:2 * C]
    for ch in chains:
        Tw = ch["Tw"] + jnp.dot(ch["Tw"].astype(BF16), expand(ch["Q"].astype(BF16)), preferred_element_type=F32)
        ch["t_bd"] = expand(Tw.astype(BF16))

    for n, s in enumerate(st):
        s["Bv"] = jnp.dot(jnp.concatenate(s["b_bd"], 1), s["vm"], preferred_element_type=F32)
    for n, s in enumerate(st):
        wm = both_heads(jnp.concatenate([s["ktb"], s["Bv"].astype(BF16)], 1), head0_2)
        t_bd = [chains[2 * n]["t_bd"], chains[2 * n + 1]["t_bd"]]
        s["x1"] = jnp.dot(jnp.concatenate(t_bd, 1), wm, preferred_element_type=F32)
    for s, prob in zip(st, probs):
        x1 = s["x1"]
        rhs2 = jnp.concatenate([both_heads(-x1.astype(BF16), head0_2),
                                jnp.concatenate([jnp.zeros((2 * TS, P), BF16), s["vm"]], 1)], 0)
        x2 = jnp.dot(jnp.concatenate(s["rb_bd"] + s["rk_bd"], 1), rhs2, preferred_element_type=F32)
        s["K2"] = x1[:, 0:P]
        s["V2"] = x1[:, P:2 * P]
        s["R2"] = prob[0] + x2[:, 0:P]
        s["Y2"] = x2[:, P:2 * P]
        s["H"] = prob[6]
        s["pc"] = jnp.exp(prob[5])
        s["ys"] = [None] * nch

    for ci in range(nch):
        for s, prob in zip(st, probs):
            (rt, kt, bt, kh, v, le, _, reverse) = prob
            c = nch - 1 - ci if reverse else ci
            sl = slice(c * C, (c + 1) * C)
            pc = s["pc"][c:c + 1]
            bh = (bt[sl] * pc).T
            khh = (kh[sl] * pc).T
            Mc = jnp.where(heye, pc, 0.0) - jnp.where(bdiag, _dot(bh, s["K2"][sl]), 0.0)
            Gc = jnp.where(bdiag, _dot(jnp.concatenate([khh, -bh], 1),
                                       jnp.concatenate([v[sl], s["V2"][sl]], 0)), 0.0)
            s["ys"][c] = _dot(s["R2"][sl], s["H"]) + s["Y2"][sl]
            s["H"] = _dot(Mc, s["H"]) + Gc
    return [(jnp.concatenate(s["ys"], 0), s["H"]) for s in st]


WKV_PAIRS = 4


def _wkv_kernel(rtf_ref, ktf_ref, btf_ref, khf_ref, vf_ref, lef_ref,
                rtb_ref, ktb_ref, btb_ref, khb_ref, vb_ref, leb_ref, yf_ref, yb_ref, h_ref):
    i = pl.program_id(2)

    @pl.when(i == 0)
    def _():
        h_ref[...] = jnp.zeros_like(h_ref)

    P = 2 * RW_HEAD_DIM
    dirs = ((rtf_ref, ktf_ref, btf_ref, khf_ref, vf_ref, lef_ref, yf_ref),
            (rtb_ref, ktb_ref, btb_ref, khb_ref, vb_ref, leb_ref, yb_ref))
    probs, outs = [], []
    for d, (rt_ref, kt_ref, bt_ref, kh_ref, v_ref, le_ref, y_ref) in enumerate(dirs):
        for pr in range(WKV_PAIRS):
            ln = slice(pr * P, (pr + 1) * P)
            probs.append((rt_ref[0, 0, :, ln], kt_ref[0, 0, :, ln], bt_ref[0, 0, :, ln], kh_ref[0, 0, :, ln],
                          v_ref[0, :, ln], le_ref[0, 0, 0, :, ln], h_ref[d, pr], d == 1))
            outs.append((y_ref, d, pr, ln))
    for (y, H), (y_ref, d, pr, ln) in zip(_wkv_tiles(probs), outs):
        y_ref[0, :, ln] = y.astype(y_ref.dtype)
        h_ref[d, pr] = H


def wkv_scan(rt, kt, bt, kh, v, le):
    _, B, S, W = rt.shape
    TS = WKV_TILE
    nt = S // TS
    nch = TS // WKV_CHUNK
    P = 2 * RW_HEAD_DIM
    PW = WKV_PAIRS * P
    tiles = (lambda i: i, lambda i: nt - 1 - i)

    def dir_specs(d):
        t = tiles[d]
        big = lambda: pl.BlockSpec((1, 1, TS, PW), lambda b, p, i: (d, b, t(i), p))
        return [big(), big(), big(), big(),
                pl.BlockSpec((1, TS, PW), lambda b, p, i: (b, t(i), p)),
                pl.BlockSpec((1, 1, 1, nch, PW), lambda b, p, i: (d, b, t(i), 0, p))]

    return pl.pallas_call(
        _wkv_kernel,
        grid=(B, W // PW, nt),
        in_specs=dir_specs(0) + dir_specs(1),
        out_specs=[pl.BlockSpec((1, TS, PW), lambda b, p, i: (b, tiles[0](i), p)),
                   pl.BlockSpec((1, TS, PW), lambda b, p, i: (b, tiles[1](i), p))],
        out_shape=[jax.ShapeDtypeStruct((B, S, W), BF16)] * 2,
        scratch_shapes=[pltpu.VMEM((2, WKV_PAIRS, P, P), F32)],
        compiler_params=_cparams("parallel", "parallel", "arbitrary"),
        name="wkv",
    )(rt, kt, bt, kh, v, le, rt, kt, bt, kh, v, le)


def _rec_out_kernel(x_ref, rg_ref, yf_ref, yb_ref, bonus_ref, g_ref, lnw_ref, lnb_ref, w_ref, o_ref):
    W = RW_WIDTH
    ones = _head_ones(W, RW_HEAD_DIM)
    y = yf_ref[...].astype(F32) + yb_ref[...].astype(F32)
    inv_n = 1.0 / RW_HEAD_DIM
    mu = _dot(y, ones) * inv_n
    yc = y - mu
    var = _dot(yc * yc, ones) * inv_n
    yn = yc * lax.rsqrt(var + RW_GN_EPS) * lnw_ref[...] + lnb_ref[...]
    rw_out = (yn + bonus_ref[...]) * g_ref[...]
    mix = (jnp.dot(rg_ref[...], w_ref[0:RG_WIDTH, :], preferred_element_type=F32)
           + jnp.dot(rw_out.astype(BF16), w_ref[RG_WIDTH:RG_WIDTH + W, :], preferred_element_type=F32))
    o_ref[...] = x_ref[...] + mix


def rec_out(x2d, rg_out, y_f, y_b, bonus, g, ln_w, ln_b, w_out):
    T, D = x2d.shape
    W = RW_WIDTH
    row = lambda n: pl.BlockSpec((ROW_TILE, n), lambda i: (i, 0))
    vec = lambda n: pl.BlockSpec((1, n), lambda i: (0, 0))
    return pl.pallas_call(
        _rec_out_kernel,
        grid=(T // ROW_TILE,),
        in_specs=[row(D), row(RG_WIDTH), row(W), row(W), row(W), row(W), vec(W), vec(W),
                  pl.BlockSpec((RG_WIDTH + W, D), lambda i: (0, 0))],
        out_specs=row(D),
        out_shape=jax.ShapeDtypeStruct((T, D), F32),
        compiler_params=_cparams("parallel"),
        name="rec_out",
    )(x2d, rg_out, y_f, y_b, bonus, g, ln_w.reshape(1, W), ln_b.reshape(1, W), w_out.astype(BF16))


def recurrent_layer(x, norm_g, w_in, conv_w, conv_b, rg_wa, rg_ba, rg_wx, rg_bx, rg_lambda,
                    mu_l, mu_r, w0, w_up, a0, a_up, g_up, k_k, k_a, r_k, ln_w, ln_b, w_out):
    B, S, D = x.shape
    T = B * S
    x2d = x.reshape(T, D)
    rg_x, rg_g, rw = norm_proj(x2d, norm_g, w_in, (RG_WIDTH, RG_WIDTH, RW_IN), out_dtype=BF16)
    rg_out = rglru_branch(rg_x.reshape(B, S, RG_WIDTH), rg_g.reshape(B, S, RG_WIDTH), conv_w, conv_b,
                          rg_wa, rg_ba, rg_wx, rg_bx, rg_lambda)
    rt, kt, bt, kh, v, le, bonus, g = rwkv_prep(rw.reshape(B, S, RW_IN), mu_l, mu_r, w0, w_up, a0, a_up,
                                                g_up, k_k, k_a, r_k.reshape(-1))
    y_f, y_b = wkv_scan(rt, kt, bt, kh, v, le)
    flat = lambda t: t.reshape(T, -1)
    out = rec_out(x2d, flat(rg_out), flat(y_f), flat(y_b), flat(bonus), flat(g), ln_w, ln_b, w_out)
    return out.reshape(B, S, D)


def _alibi_slope(h):
    return 2.0 ** (-8.0 * (h + 1) / ATT_HEADS)


ATT_PAIR = 2 * ATT_HEAD_DIM
ATT_KVW = ATT_KV_HEADS * ATT_PAIR


def _qkv_proj_kernel(x_ref, g_ref, w_ref, eq_ref, ek_ref, qg_ref, kg_ref, q_ref, k_ref, v_ref):
    QW = ATT_HEADS * ATT_HEAD_DIM
    x = x_ref[...]
    h = (x * lax.rsqrt(jnp.mean(x * x, -1, keepdims=True) + RMS_EPS) * g_ref[...]).astype(BF16)
    q = jnp.dot(h, w_ref[:, 0:QW], preferred_element_type=F32)
    msq = jnp.dot((q * q).astype(BF16), eq_ref[...], preferred_element_type=F32) * (1.0 / ATT_HEAD_DIM)
    q_ref[...] = (q * lax.rsqrt(msq + RMS_EPS) * qg_ref[...]).astype(BF16)
    k = jnp.dot(h, w_ref[:, QW:QW + ATT_KVW], preferred_element_type=F32)
    msk = jnp.dot((k * k).astype(BF16), ek_ref[...], preferred_element_type=F32) * (1.0 / ATT_PAIR)
    k_ref[...] = (k * lax.rsqrt(msk + RMS_EPS) * kg_ref[...]).astype(BF16)
    v_ref[...] = jnp.dot(h, w_ref[:, QW + ATT_KVW:QW + 2 * ATT_KVW], preferred_element_type=F32).astype(BF16)


def qkv_proj(x2d, g, w_in, q_norm, k_norm):
    T, D = x2d.shape
    QW = ATT_HEADS * ATT_HEAD_DIM
    KW = ATT_KV_HEADS * ATT_HEAD_DIM
    dup = lambda w: jnp.concatenate([w.reshape(D, ATT_KV_HEADS, 1, ATT_HEAD_DIM)] * 2, 2).reshape(D, ATT_KVW)
    w = jnp.concatenate([w_in[:, :QW], dup(w_in[:, QW:QW + KW]), dup(w_in[:, QW + KW:])], 1).astype(BF16)
    N = QW + 2 * ATT_KVW
    qg = jnp.tile(q_norm.astype(F32), ATT_HEADS).reshape(1, QW) * (ATT_HEAD_DIM ** -0.5)
    kg = jnp.tile(k_norm.astype(F32), 2 * ATT_KV_HEADS).reshape(1, ATT_KVW)
    row = lambda n: pl.BlockSpec((ROW_TILE, n), lambda i: (i, 0))
    fixed = lambda r, n: pl.BlockSpec((r, n), lambda i: (0, 0))
    return pl.pallas_call(
        _qkv_proj_kernel,
        grid=(T // ROW_TILE,),
        in_specs=[row(D), fixed(1, D), fixed(D, N), fixed(QW, QW), fixed(ATT_KVW, ATT_KVW),
                  fixed(1, QW), fixed(1, ATT_KVW)],
        out_specs=[row(QW), row(ATT_KVW), row(ATT_KVW)],
        out_shape=[jax.ShapeDtypeStruct((T, QW), BF16), jax.ShapeDtypeStruct((T, ATT_KVW), BF16),
                   jax.ShapeDtypeStruct((T, ATT_KVW), BF16)],
        compiler_params=_cparams("parallel"),
        name="qkv_proj",
    )(x2d, g.reshape(1, D), w, _head_ones(QW, ATT_HEAD_DIM), _head_ones(ATT_KVW, ATT_PAIR), qg, kg)


def _attn_kernel(sink_ref, x_ref, q_ref, kp_ref, kc_ref, kn_ref, vp_ref, vc_ref, vn_ref, bias_ref,
                 w_ref, o_ref):
    i = pl.program_id(1)
    nb = pl.num_programs(1)
    BLK = ATT_BLOCK
    P = ATT_PAIR
    span = 3 * BLK
    kc = jnp.concatenate([kp_ref[0], kc_ref[0], kn_ref[0]], 0)
    vc = jnp.concatenate([vp_ref[0], vc_ref[0], vn_ref[0]], 0)
    kpos = lax.broadcasted_iota(jnp.int32, (1, span), 1) + (i - 1) * BLK
    edge = jnp.where((kpos >= 0) & (kpos < nb * BLK), 0.0, NEG_INF)
    lane = lax.broadcasted_iota(jnp.int32, (1, P), 1)
    low = lane < ATT_HEAD_DIM
    zero = jnp.zeros((), BF16)
    ones = jnp.ones((span, P), BF16)
    slabs = []
    for g in range(ATT_KV_HEADS):
        kg = kc[:, g * P:(g + 1) * P]
        k_half = (jnp.where(low, kg, zero), jnp.where(low, zero, kg))
        v_ext = jnp.concatenate([vc[:, g * P:(g + 1) * P], ones], 1)
        for pr in range(ATT_GROUP // 2):
            slab = g * (ATT_GROUP // 2) + pr
            qp = q_ref[0, :, slab * P:(slab + 1) * P]
            halves = []
            for hf in range(2):
                h = 2 * slab + hf
                s = lax.dot_general(qp, k_half[hf], (((1,), (1,)), ((), ())), preferred_element_type=F32)
                s = s + bias_ref[h] + edge
                sk = sink_ref[h]
                m = jnp.maximum(jnp.max(s, -1, keepdims=True), sk)
                p = jnp.exp(s - m).astype(BF16)
                o = jnp.dot(p, v_ext, preferred_element_type=F32)
                halves.append(o[:, 0:P] / (o[:, P:2 * P] + jnp.exp(sk - m)))
            slabs.append(jnp.where(low, halves[0], halves[1]))
    o = jnp.concatenate(slabs, -1).astype(BF16)
    o_ref[0] = x_ref[0] + jnp.dot(o, w_ref[...], preferred_element_type=F32)


def attention_layer(x, norm_g, w_in, q_norm, k_norm, sink, w_out):
    B, S, D = x.shape
    T = B * S
    QW = ATT_HEADS * ATT_HEAD_DIM
    q, k, v = qkv_proj(x.reshape(T, D), norm_g, w_in, q_norm, k_norm)
    q = q.reshape(B, S, QW)
    k = k.reshape(B, S, ATT_KVW)
    v = v.reshape(B, S, ATT_KVW)
    nb = S // ATT_BLOCK
    span = 3 * ATT_BLOCK
    rel = (jnp.arange(span)[None, :] - WINDOW) - jnp.arange(ATT_BLOCK)[:, None]
    slopes = jnp.asarray([_alibi_slope(h) for h in range(ATT_HEADS)], F32)
    bias = jnp.where(jnp.abs(rel) <= WINDOW, -slopes[:, None, None] * jnp.abs(rel).astype(F32), NEG_INF)
    prev = lambda b, i: (b, jnp.maximum(i - 1, 0), 0)
    cur = lambda b, i: (b, i, 0)
    nxt = lambda b, i: (b, jnp.minimum(i + 1, nb - 1), 0)
    kv = lambda f: pl.BlockSpec((1, ATT_BLOCK, ATT_KVW), f)
    return pl.pallas_call(
        _attn_kernel,
        grid=(B, nb),
        in_specs=[pl.BlockSpec(memory_space=pltpu.SMEM),
                  pl.BlockSpec((1, ATT_BLOCK, D), cur),
                  pl.BlockSpec((1, ATT_BLOCK, QW), cur),
                  kv(prev), kv(cur), kv(nxt), kv(prev), kv(cur), kv(nxt),
                  pl.BlockSpec((ATT_HEADS, ATT_BLOCK, span), lambda b, i: (0, 0, 0)),
                  pl.BlockSpec((QW, D), lambda b, i: (0, 0))],
        out_specs=pl.BlockSpec((1, ATT_BLOCK, D), cur),
        out_shape=jax.ShapeDtypeStruct((B, S, D), F32),
        compiler_params=_cparams("parallel", "parallel"),
        name="window_attn",
    )(sink.astype(F32), x, q, k, k, k, v, v, v, bias, w_out.astype(BF16))


ROUTE_OFF = N_GROUPS


def _router_kernel(x_ref, g_ref, w12_ref, b_ref, h_ref, ri_ref, rg_ref, cnt_ref, run_ref):
    i = pl.program_id(0)

    @pl.when(i == 0)
    def _():
        run_ref[...] = jnp.zeros_like(run_ref)

    TM = x_ref.shape[0]
    x = x_ref[...]
    h = x * lax.rsqrt(jnp.mean(x * x, -1, keepdims=True) + RMS_EPS) * g_ref[...]
    _store_rows(h_ref, _pack_bf16_pairs(h))
    h1 = h.astype(BF16)
    h2 = (h - h1.astype(F32)).astype(BF16)
    hw = jnp.dot(h1, w12_ref[...], preferred_element_type=F32)
    lg = (hw[:, 0:ROUTE_LANES] + hw[:, ROUTE_LANES:2 * ROUTE_LANES]
          + jnp.dot(h2, w12_ref[:, 0:ROUTE_LANES], preferred_element_type=F32)) + b_ref[...]
    lane_i = lax.broadcasted_iota(jnp.int32, (1, ROUTE_LANES), 1)
    lane = lane_i.astype(F32)
    lane_group = ((lane_i - ROUTE_OFF + EXPERTS_PER_GROUP) // EXPERTS_PER_GROUP - 1).astype(F32)
    far = float(ROUTE_LANES)
    gmask = lane_i < N_GROUPS
    gl = jnp.where(gmask, lg, NEG_INF)
    gm = jnp.max(gl, -1, keepdims=True)
    p_group = 1.0 / jnp.sum(jnp.where(gmask, jnp.exp(gl - gm), 0.0), -1, keepdims=True)
    group = jnp.min(jnp.where(gl == gm, lane, far), -1, keepdims=True)
    fmask = lane_group == group
    fl = jnp.where(fmask, lg, NEG_INF)
    m1 = jnp.max(fl, -1, keepdims=True)
    ssum = jnp.sum(jnp.where(fmask, jnp.exp(fl - m1), 0.0), -1, keepdims=True)
    i1 = jnp.min(jnp.where(fl == m1, lane, far), -1, keepdims=True)
    fl2 = jnp.where(lane == i1, NEG_INF, fl)
    m2 = jnp.max(fl2, -1, keepdims=True)
    i2 = jnp.min(jnp.where(fl2 == m2, lane, far), -1, keepdims=True)
    p1 = 1.0 / ssum
    p2 = jnp.exp(m2 - m1) / ssum
    norm = p_group / (p1 + p2)
    oh = jnp.where((lane == i1) | (lane == i2), 1.0, 0.0)
    tr = lax.broadcasted_iota(jnp.int32, (TM, TM), 0)
    tc = lax.broadcasted_iota(jnp.int32, (TM, TM), 1)
    before = jnp.where(tc < tr, 1.0, 0.0).astype(BF16)
    pre = jnp.dot(before, oh.astype(BF16), preferred_element_type=F32) + run_ref[...]
    rank1 = jnp.sum(jnp.where(lane == i1, pre, 0.0), -1, keepdims=True)
    rank2 = jnp.sum(jnp.where(lane == i2, pre, 0.0), -1, keepdims=True)
    total = run_ref[...] + jnp.sum(oh, 0, keepdims=True)
    run_ref[...] = total
    cnt_ref[...] = total.astype(jnp.int32)
    cols = jnp.where(lane_i == 0, i1 - ROUTE_OFF,
                     jnp.where(lane_i == 1, i2 - ROUTE_OFF,
                               jnp.where(lane_i == 2, rank1, jnp.where(lane_i == 3, rank2, 0.0))))
    ri_ref[...] = jnp.transpose(cols)[0:SUBLANES, :].astype(jnp.int32)
    rg_ref[...] = jnp.where(lane_i == 0, p1 * norm, jnp.where(lane_i == 1, p2 * norm, 0.0))


def moe_router(x2d, g, wg1, bg1, wg2, bg2):
    T, D = x2d.shape
    wr = jnp.zeros((D, ROUTE_LANES), F32)
    wr = wr.at[:, 0:N_GROUPS].set(wg1)
    wr = wr.at[:, ROUTE_OFF:ROUTE_OFF + N_EXPERTS].set(jnp.moveaxis(wg2, 0, 1).reshape(D, N_EXPERTS))
    w1 = wr.astype(BF16)
    w2 = (wr - w1.astype(F32)).astype(BF16)
    bias = jnp.zeros((1, ROUTE_LANES), F32)
    bias = bias.at[0, 0:N_GROUPS].set(bg1).at[0, ROUTE_OFF:ROUTE_OFF + N_EXPERTS].set(bg2.reshape(-1))
    row = lambda n: pl.BlockSpec((ROW_TILE, n), lambda i: (i, 0))
    fixed = lambda r, n: pl.BlockSpec((r, n), lambda i: (0, 0))
    return pl.pallas_call(
        _router_kernel,
        grid=(T // ROW_TILE,),
        in_specs=[row(D), fixed(1, D), fixed(D, 2 * ROUTE_LANES), fixed(1, ROUTE_LANES)],
        out_specs=[pl.BlockSpec((ROW_TILE * ROW_SLABS, LANES), lambda i: (i, 0)),
                   pl.BlockSpec((SUBLANES, ROW_TILE), lambda i: (i, 0)), row(ROUTE_LANES), fixed(1, ROUTE_LANES)],
        out_shape=[jax.ShapeDtypeStruct((T * ROW_SLABS, LANES), jnp.int32),
                   jax.ShapeDtypeStruct((T // ROW_TILE * SUBLANES, ROW_TILE), jnp.int32),
                   jax.ShapeDtypeStruct((T, ROUTE_LANES), F32),
                   jax.ShapeDtypeStruct((1, ROUTE_LANES), jnp.int32)],
        scratch_shapes=[pltpu.VMEM((1, ROUTE_LANES), F32)],
        compiler_params=_cparams("arbitrary"),
        name="moe_router",
    )(x2d, g.reshape(1, D), jnp.concatenate([w1, w2], 1), bias)


ROW_SLABS = (D_MODEL // 2) // LANES


def _store_rows(ref, packed):
    n = packed.shape[0]
    for j in range(ROW_SLABS):
        ref[pl.ds(j, n, stride=ROW_SLABS), :] = packed[:, j * LANES:(j + 1) * LANES]


def _load_rows(ref, n):
    return jnp.concatenate([ref[pl.ds(j, n, stride=ROW_SLABS), :] for j in range(ROW_SLABS)], 1)


def _row_copy(src, src_row, dst, dst_row, sem):
    return pltpu.make_async_copy(src.at[pl.ds(src_row * ROW_SLABS, ROW_SLABS)],
                                 dst.at[pl.ds(dst_row * ROW_SLABS, ROW_SLABS)], sem)


DMA_UNROLL = 8


def _rows_wait(ref, nrows, sem):
    pltpu.make_async_copy(ref.at[pl.ds(0, nrows * ROW_SLABS)], ref.at[pl.ds(0, nrows * ROW_SLABS)], sem).wait()


DISPATCH_ROWS = DMA_ROWS


def _dispatch_kernel(pend_ref, dest_ref, h_ref, xs_ref, zero_ref, sem, zsem):
    i = pl.program_id(0)
    n = h_ref.shape[0] // ROW_SLABS
    blk = MOE_ROWS * ROW_SLABS
    nblk = xs_ref.shape[0] // blk

    @pl.when(i == 0)
    def _():
        zero_ref[...] = jnp.zeros_like(zero_ref)

        def block_copy(row0):
            return pltpu.make_async_copy(zero_ref, xs_ref.at[pl.ds(pl.multiple_of(row0 * ROW_SLABS, blk), blk)], zsem)

        tails = [jnp.maximum(pend_ref[e] - MOE_ROWS, 0) for e in range(N_EXPERTS)]
        for t in tails:
            block_copy(t).start()
        for t in tails:
            block_copy(t).wait()
        first_unused = pend_ref[N_EXPERTS - 1] // MOE_ROWS

        def clear(j, c):
            block_copy(j * MOE_ROWS).start()
            block_copy(j * MOE_ROWS).wait()
            return c

        lax.fori_loop(first_unused, nblk, clear, 0)

    def issue(r, c):
        for k in range(TOP_K):
            _row_copy(h_ref, r, xs_ref, dest_ref[0, 0, k * n + r], sem).start(priority=k % 2)
        return c

    lax.fori_loop(0, n, issue, 0, unroll=DMA_UNROLL)
    _rows_wait(xs_ref, TOP_K * n, sem)


def moe_dispatch(h_rows, dest, p_end, rows):
    T = h_rows.shape[0] // ROW_SLABS
    nt = T // DISPATCH_ROWS
    return pl.pallas_call(
        _dispatch_kernel,
        grid_spec=pltpu.PrefetchScalarGridSpec(
            num_scalar_prefetch=1,
            grid=(nt,),
            in_specs=[pl.BlockSpec((1, 1, TOP_K * DISPATCH_ROWS), lambda i, pe: (i, 0, 0),
                                   memory_space=pltpu.SMEM),
                      pl.BlockSpec((DISPATCH_ROWS * ROW_SLABS, LANES), lambda i, pe: (i, 0))],
            out_specs=pl.BlockSpec(memory_space=pl.ANY),
            scratch_shapes=[pltpu.VMEM((MOE_ROWS * ROW_SLABS, LANES), h_rows.dtype), pltpu.SemaphoreType.DMA(()),
                            pltpu.SemaphoreType.DMA(())]),
        out_shape=jax.ShapeDtypeStruct((rows * ROW_SLABS, LANES), h_rows.dtype),
        compiler_params=_cparams("arbitrary"),
        name="moe_dispatch",
    )(p_end, dest, h_rows)


def _expert_kernel(be_ref, nu_ref, x_ref, wg_ref, wu_ref, wd_ref, o_ref):
    i = pl.program_id(0)

    @pl.when(i < nu_ref[0])
    def _():
        xb = _unpack_bf16_pairs(_load_rows(x_ref, MOE_ROWS)).astype(BF16)
        hg = jnp.dot(xb, wg_ref[0], preferred_element_type=F32)
        hu = jnp.dot(xb, wu_ref[0], preferred_element_type=F32)
        hb = (hg * _sigmoid_tanh(hg) * hu).astype(BF16)
        _store_rows(o_ref, _pack_bf16_pairs(jnp.dot(hb, wd_ref[0], preferred_element_type=F32)))

    @pl.when(i >= nu_ref[0])
    def _():
        o_ref[...] = jnp.zeros_like(o_ref)


def moe_experts(xs, blk_exp, n_used, w_gate, w_up, w_down):
    D = D_MODEL
    blk = MOE_ROWS * ROW_SLABS
    nblk = xs.shape[0] // blk
    F = EXPERT_FF
    return pl.pallas_call(
        _expert_kernel,
        grid_spec=pltpu.PrefetchScalarGridSpec(
            num_scalar_prefetch=2,
            grid=(nblk,),
            in_specs=[pl.BlockSpec((blk, LANES), lambda i, be, nu: (jnp.minimum(i, nu[0] - 1), 0)),
                      pl.BlockSpec((1, D, F), lambda i, be, nu: (be[i], 0, 0)),
                      pl.BlockSpec((1, D, F), lambda i, be, nu: (be[i], 0, 0)),
                      pl.BlockSpec((1, F, D), lambda i, be, nu: (be[i], 0, 0))],
            out_specs=pl.BlockSpec((blk, LANES), lambda i, be, nu: (i, 0))),
        out_shape=jax.ShapeDtypeStruct(xs.shape, xs.dtype),
        compiler_params=_cparams("arbitrary"),
        name="moe_experts",
    )(blk_exp, n_used, xs, w_gate, w_up, w_down)


def _combine_kernel(dest_ref, dnext_ref, x_ref, gate_ref, eo_ref, o_ref, buf_ref, sem):
    i = pl.program_id(0)
    nt = pl.num_programs(0)
    n = x_ref.shape[0]
    slot = i % 2

    def gather(d_ref, s):
        def issue(r, c):
            for k in range(TOP_K):
                _row_copy(eo_ref, d_ref[0, 0, k * n + r], buf_ref.at[s, k], r, sem.at[s]).start(priority=k % 2)
            return c

        lax.fori_loop(0, n, issue, 0, unroll=DMA_UNROLL)

    @pl.when(i == 0)
    def _():
        gather(dest_ref, 0)

    @pl.when(i + 1 < nt)
    def _():
        gather(dnext_ref, 1 - slot)

    pltpu.make_async_copy(buf_ref.at[slot], buf_ref.at[slot], sem.at[slot]).wait()
    gate = gate_ref[...]
    ffn = gate[:, 0:1] * _unpack_bf16_pairs(_load_rows(buf_ref.at[slot, 0], n))
    for k in range(1, TOP_K):
        ffn = ffn + gate[:, k:k + 1] * _unpack_bf16_pairs(_load_rows(buf_ref.at[slot, k], n))
    o_ref[...] = x_ref[...] + ffn


def moe_combine(x2d, gates, dest, eo):
    T, D = x2d.shape
    nt = T // DMA_ROWS
    dest3 = dest
    dspec = lambda f: pl.BlockSpec((1, 1, TOP_K * DMA_ROWS), f, memory_space=pltpu.SMEM)
    return pl.pallas_call(
        _combine_kernel,
        grid=(nt,),
        in_specs=[dspec(lambda i: (i, 0, 0)),
                  dspec(lambda i: (jnp.minimum(i + 1, nt - 1), 0, 0)),
                  pl.BlockSpec((DMA_ROWS, D), lambda i: (i, 0)),
                  pl.BlockSpec((DMA_ROWS, ROUTE_LANES), lambda i: (i, 0)),
                  pl.BlockSpec(memory_space=pl.ANY)],
        out_specs=pl.BlockSpec((DMA_ROWS, D), lambda i: (i, 0)),
        out_shape=jax.ShapeDtypeStruct((T, D), F32),
        scratch_shapes=[pltpu.VMEM((2, TOP_K, DMA_ROWS * ROW_SLABS, LANES), eo.dtype),
                        pltpu.SemaphoreType.DMA((2,))],
        compiler_params=_cparams("arbitrary"),
        name="moe_combine",
    )(dest3, dest3, x2d, gates, eo)


def moe_layer(x, layer, norm_g, wg1, bg1, wg2, bg2, w_gate, w_up, w_down):
    B, S, D = x.shape
    T = B * S
    A = T * TOP_K
    x2d = x.reshape(T, D)
    h, route_i, route_g, counts = moe_router(x2d, norm_g, wg1, bg1, wg2, bg2)
    counts = counts[0, ROUTE_OFF:ROUTE_OFF + N_EXPERTS]
    padded = (counts + MOE_ROWS - 1) // MOE_ROWS * MOE_ROWS
    p_end = jnp.cumsum(padded)
    p_start = p_end - padded
    nt = T // ROW_TILE
    ri = route_i.reshape(nt, SUBLANES, ROW_TILE)
    sel = ri[:, 0:TOP_K, :, None] == jnp.arange(N_EXPERTS, dtype=jnp.int32)
    dest = jnp.sum(jnp.where(sel, p_start, 0), -1) + ri[:, TOP_K:2 * TOP_K, :]
    dest = dest.reshape(nt, 1, TOP_K * ROW_TILE).astype(jnp.int32)
    nblk = -(-A // MOE_ROWS) + N_EXPERTS
    blk_row = jnp.arange(nblk, dtype=jnp.int32) * MOE_ROWS
    blk_exp = jnp.minimum(jnp.sum(p_end[None, :] <= blk_row[:, None], -1), N_EXPERTS - 1).astype(jnp.int32)
    n_used = (p_end[-1:] // MOE_ROWS).astype(jnp.int32)
    xs = moe_dispatch(h, dest, p_end.astype(jnp.int32), nblk * MOE_ROWS)
    eo = moe_experts(xs, blk_exp + layer * N_EXPERTS, n_used, w_gate, w_up, w_down)
    return moe_combine(x2d, route_g, dest, eo).reshape(B, S, D)


def _trunk(x, p):
    x = recurrent_layer(x, p["norm_mix"][0], p["rec_w_in"][0], p["rg_conv_w"][0], p["rg_conv_b"][0],
                        p["rg_wa"][0], p["rg_ba"][0], p["rg_wx"][0], p["rg_bx"][0], p["rg_lambda"][0],
                        p["rw_mu_l"][0], p["rw_mu_r"][0], p["rw_w0"][0], p["rw_w_up"][0], p["rw_a0"][0],
                        p["rw_a_up"][0], p["rw_g_up"][0], p["rw_k_k"][0], p["rw_k_a"][0], p["rw_r_k"][0],
                        p["rw_ln_w"][0], p["rw_ln_b"][0], p["rec_w_out"][0])
    x = moe_layer(x, 0, p["norm_ffn"][0], p["moe_wg1"][0], p["moe_bg1"][0], p["moe_wg2"][0], p["moe_bg2"][0],
                  p["moe_w_gate"], p["moe_w_up"], p["moe_w_down"])
    x = attention_layer(x, p["norm_mix"][1], p["att_w_in"][0], p["att_q_norm"][0], p["att_k_norm"][0],
                        p["att_sink"][0], p["att_w_out"][0])
    x = moe_layer(x, 1, p["norm_ffn"][1], p["moe_wg1"][1], p["moe_bg1"][1], p["moe_wg2"][1], p["moe_bg2"][1],
                  p["moe_w_gate"], p["moe_w_up"], p["moe_w_down"])
    return x


def kernel(x_prompt, x_sample, norm_mix, norm_ffn, rec_w_in, rg_conv_w, rg_conv_b, rg_wa, rg_ba, rg_wx, rg_bx,
           rg_lambda, rw_mu_l, rw_mu_r, rw_w0, rw_w_up, rw_a0, rw_a_up, rw_g_up, rw_k_k, rw_k_a, rw_r_k,
           rw_ln_w, rw_ln_b, rec_w_out, att_w_in, att_q_norm, att_k_norm, att_sink, att_w_out, moe_wg1, moe_bg1,
           moe_wg2, moe_bg2, moe_w_gate, moe_w_up, moe_w_down):
    p = dict(norm_mix=norm_mix, norm_ffn=norm_ffn, rec_w_in=rec_w_in, rg_conv_w=rg_conv_w, rg_conv_b=rg_conv_b,
             rg_wa=rg_wa, rg_ba=rg_ba, rg_wx=rg_wx, rg_bx=rg_bx, rg_lambda=rg_lambda, rw_mu_l=rw_mu_l,
             rw_mu_r=rw_mu_r, rw_w0=rw_w0, rw_w_up=rw_w_up, rw_a0=rw_a0, rw_a_up=rw_a_up, rw_g_up=rw_g_up,
             rw_k_k=rw_k_k, rw_k_a=rw_k_a, rw_r_k=rw_r_k, rw_ln_w=rw_ln_w, rw_ln_b=rw_ln_b, rec_w_out=rec_w_out,
             att_w_in=att_w_in, att_q_norm=att_q_norm, att_k_norm=att_k_norm, att_sink=att_sink,
             att_w_out=att_w_out, moe_wg1=moe_wg1, moe_bg1=moe_bg1, moe_wg2=moe_wg2, moe_bg2=moe_bg2,
             moe_w_gate=moe_w_gate.astype(BF16).reshape(-1, D_MODEL, EXPERT_FF),
             moe_w_up=moe_w_up.astype(BF16).reshape(-1, D_MODEL, EXPERT_FF),
             moe_w_down=moe_w_down.astype(BF16).reshape(-1, EXPERT_FF, D_MODEL))
    return (_trunk(x_prompt, p), _trunk(x_sample, p))
```
